```python
import math
import jax
import jax.numpy as jnp
from jax import lax
import numpy as np

D_MODEL = 4096
BATCH = 2
SEQ = 4096
DEPTH = 1
DEC_BATCH = 128
DEC_SEQ = 8
PAST_LEN = 2048
PAGE_SIZE = 128

POOL_WIDTH = D_MODEL // 2
POOL_WINDOWS = (2, 4, 8, 16)
N_POOL_GROUPS = len(POOL_WINDOWS)
POOL_GROUP = POOL_WIDTH // N_POOL_GROUPS
POOL_PREFIX = max(POOL_WINDOWS) - 1
HEAD_DIM = 128
N_HEADS = (D_MODEL // 2) // HEAD_DIM
ATTN_WIDTH = N_HEADS * HEAD_DIM
IDX_HEADS = 32
IDX_DIM = 64
TOPK_MAX = 256
Q_BLOCK = 128
NUM_BUCKETS = 32
MAX_DISTANCE = 128
D_FF = 4 * D_MODEL
EPS = 1e-6
SPLITS = (POOL_WIDTH, ATTN_WIDTH, ATTN_WIDTH, ATTN_WIDTH, IDX_HEADS * IDX_DIM, IDX_DIM, IDX_HEADS, D_MODEL, D_MODEL)
N_IN = sum(SPLITS)

kernel_name = "hybrid_pool_dsa_gated_decoder_step"


def rmsnorm(x, g):
    xf = x.astype(jnp.float32)
    y = xf * lax.rsqrt(jnp.mean(xf * xf, axis=-1, keepdims=True) + EPS)
    return (y * g.astype(jnp.float32)).astype(x.dtype)


def project(x, ln1, w_in):
    n, t = x.shape[:2]
    z = rmsnorm(x, ln1) @ w_in
    pts, acc = [], 0
    for s in SPLITS[:-1]:
        acc += s
        pts.append(acc)
    u, q, k, v, qi, ki, wi, gp, ga = jnp.split(z, pts, axis=-1)
    q = q.reshape(n, t, N_HEADS, HEAD_DIM)
    k = k.reshape(n, t, N_HEADS, HEAD_DIM)
    v = v.reshape(n, t, N_HEADS, HEAD_DIM)
    qi = qi.reshape(n, t, IDX_HEADS, IDX_DIM)
    return u, q, k, v, qi, ki, wi, gp, ga


def pool_mix(u, prefix, pos0, pool_w, pool_scale):
    n, t = u.shape[:2]
    ext = jnp.concatenate([prefix.astype(u.dtype), u], axis=1)
    ext32 = ext.astype(jnp.float32)
    cs0 = jnp.concatenate([jnp.zeros_like(ext32[:, :1]), jnp.cumsum(ext32, axis=1)], axis=1)
    pos = pos0 + jnp.arange(t)
    cur = ext32[:, POOL_PREFIX:]
    groups = []
    for g, w in enumerate(POOL_WINDOWS):
        c = slice(g * POOL_GROUP, (g + 1) * POOL_GROUP)
        hi = cs0[:, POOL_PREFIX + 1:POOL_PREFIX + 1 + t, c]
        lo = cs0[:, POOL_PREFIX + 1 - w:POOL_PREFIX + 1 - w + t, c]
        cnt = jnp.minimum(w, pos + 1).astype(jnp.float32)[None, :, None]
        groups.append((hi - lo) / cnt - cur[..., c])
    y = jnp.stack(groups, axis=2)
    y = jnp.einsum('ntgc,gcd->ntgd', y, pool_w.astype(jnp.float32)).reshape(n, t, POOL_WIDTH)
    y = y * pool_scale.astype(jnp.float32)
    return y.astype(u.dtype), ext[:, -POOL_PREFIX:]


def rel_bucket(dist):
    dist = jnp.maximum(dist, 0)
    max_exact = NUM_BUCKETS // 2
    d = jnp.maximum(dist, 1).astype(jnp.float32)
    large = max_exact + (jnp.log(d / max_exact) / math.log(MAX_DISTANCE / max_exact)
                         * (NUM_BUCKETS - max_exact)).astype(jnp.int32)
    large = jnp.minimum(large, NUM_BUCKETS - 1)
    return jnp.where(dist < max_exact, dist, large)


def indexer_topk(qi, wi, kidx, qpos, kpos, topk):
    dots = jnp.einsum('...qhd,...sd->...qhs', qi, kidx,
                      preferred_element_type=jnp.float32) * (IDX_DIM ** -0.5)
    wts = wi.astype(jnp.float32) * (IDX_HEADS ** -0.5)
    scores = jnp.einsum('...qh,...qhs->...qs', wts, jax.nn.relu(dots))
    causal = kpos[None, :] <= qpos[:, None]
    scores = jnp.where(causal, scores, -jnp.inf)
    _, idx = lax.top_k(scores, topk)
    valid = idx <= qpos[:, None]
    return idx, valid


def attend_selected(q, k_sel, v_sel, qpos, idx, valid, rel_bias):
    logits = jnp.einsum('...qhd,...qkhd->...qhk', q, k_sel,
                        preferred_element_type=jnp.float32) * (HEAD_DIM ** -0.5)
    bias = rel_bias.astype(jnp.float32)[rel_bucket(qpos[:, None] - idx)]
    logits = logits + jnp.swapaxes(bias, -1, -2)
    logits = jnp.where(valid[..., None, :], logits, -jnp.inf)
    p = jax.nn.softmax(logits, axis=-1)
    out = jnp.einsum('...qhk,...qkhd->...qhd', p, v_sel.astype(jnp.float32))
    return out.astype(q.dtype)


def prompt_attention(q, k, v, qi, ki, wi, rel_bias):
    bn, s = q.shape[:2]
    topk = min(TOPK_MAX, s // 4)
    nb = s // Q_BLOCK
    kpos = jnp.arange(s)

    def blocks(a):
        return jnp.moveaxis(a.reshape((bn, nb, Q_BLOCK) + a.shape[2:]), 1, 0)

    gather = jax.vmap(lambda a, i: a[i])

    def one_block(args):
        qb, qib, wib, start = args
        qpos = start + jnp.arange(Q_BLOCK)
        idx, valid = indexer_topk(qib, wib, ki, qpos, kpos, topk)
        return attend_selected(qb, gather(k, idx), gather(v, idx), qpos, idx, valid, rel_bias)

    out = lax.map(one_block, (blocks(q), blocks(qi), blocks(wi), jnp.arange(nb) * Q_BLOCK))
    return jnp.moveaxis(out, 0, 1).reshape(bn, s, ATTN_WIDTH)


def sample_attention(q, k, v, qi, ki, wi, cache_k, cache_v, cache_kidx, page_table, rel_bias):
    nseq, t = q.shape[:2]
    total = PAST_LEN + t
    topk = min(TOPK_MAX, total // 4)
    qpos = PAST_LEN + jnp.arange(t)
    kpos = jnp.arange(total)

    def one_seq(args):
        qs, ks, vs, qis, kis, wis, pt = args
        ki_past = cache_kidx[pt].reshape(PAST_LEN, IDX_DIM).astype(kis.dtype)
        ki_all = jnp.concatenate([ki_past, kis], axis=0)
        idx, valid = indexer_topk(qis, wis, ki_all, qpos, kpos, topk)
        in_past = idx < PAST_LEN
        pi = jnp.minimum(idx, PAST_LEN - 1)
        page = pt[pi // PAGE_SIZE]
        row = pi % PAGE_SIZE
        ni = jnp.clip(idx - PAST_LEN, 0, t - 1)

        def sel(cache, new):
            return jnp.where(in_past[..., None, None], cache[page, row].astype(new.dtype), new[ni])

        return attend_selected(qs, sel(cache_k, ks), sel(cache_v, vs), qpos, idx, valid, rel_bias)

    out = lax.map(one_seq, (q, k, v, qi, ki, wi, page_table))
    return out.reshape(nseq, t, ATTN_WIDTH)


def merge_and_mlp(x, pool_o, attn_o, gp, ga, w_bp, w_ba, w_out, ln2, w1, w2):
    a = pool_o @ w_bp
    b = attn_o @ w_ba
    m = jax.nn.sigmoid(gp) * a + jax.nn.sigmoid(ga) * b
    x = x + (m.astype(x.dtype) @ w_out)
    h = jnp.square(jax.nn.relu(rmsnorm(x, ln2) @ w1))
    return x + h @ w2


def setup_inputs(seed: int = 0) -> dict:
    key = jax.random.key(seed)
    ks = jax.random.split(key, 24)
    n_pages = PAST_LEN // PAGE_SIZE
    n_used = DEC_BATCH * n_pages
    n_phys = n_used + (n_used + 3) // 4
    f32 = jnp.float32

    def nrm(k, shape, scale):
        return jax.random.normal(k, shape, f32) * scale

    page_table = jax.random.permutation(ks[0], n_phys)[:n_used].reshape(DEC_BATCH, n_pages).astype(jnp.int32)
    return {
        "x_prompt": nrm(ks[1], (BATCH, SEQ, D_MODEL), 1.0),
        "x_sample": nrm(ks[2], (DEC_BATCH, DEC_SEQ, D_MODEL), 1.0),
        "cache_k": nrm(ks[3], (DEPTH, n_phys, PAGE_SIZE, N_HEADS, HEAD_DIM), 1.0),
        "cache_v": nrm(ks[4], (DEPTH, n_phys, PAGE_SIZE, N_HEADS, HEAD_DIM), 1.0),
        "cache_kidx": nrm(ks[5], (DEPTH, n_phys, PAGE_SIZE, IDX_DIM), 1.0),
        "state_pool": nrm(ks[6], (DEPTH, DEC_BATCH, POOL_PREFIX, POOL_WIDTH), 1.0),
        "page_table": page_table,
        "ln1": 1.0 + nrm(ks[7], (DEPTH, D_MODEL), 0.02),
        "w_in": nrm(ks[8], (DEPTH, D_MODEL, N_IN), D_MODEL ** -0.5),
        "pool_w": nrm(ks[9], (DEPTH, N_POOL_GROUPS, POOL_GROUP, POOL_GROUP), POOL_GROUP ** -0.5),
        "pool_scale": 1.0 + nrm(ks[10], (DEPTH, POOL_WIDTH), 0.02),
        "rel_bias": nrm(ks[11], (NUM_BUCKETS, N_HEADS), 0.5),
        "w_branch_pool": nrm(ks[12], (DEPTH, POOL_WIDTH, D_MODEL), POOL_WIDTH ** -0.5),
        "w_branch_attn": nrm(ks[13], (DEPTH, ATTN_WIDTH, D_MODEL), ATTN_WIDTH ** -0.5),
        "w_out": nrm(ks[14], (DEPTH, D_MODEL, D_MODEL), D_MODEL ** -0.5),
        "ln2": 1.0 + nrm(ks[15], (DEPTH, D_MODEL), 0.02),
        "w_mlp_in": nrm(ks[16], (DEPTH, D_MODEL, D_FF), D_MODEL ** -0.5),
        "w_mlp_out": nrm(ks[17], (DEPTH, D_FF, D_MODEL), D_FF ** -0.5),
        "ln_f": 1.0 + nrm(ks[18], (D_MODEL,), 0.02),
    }


def reference(x_prompt, x_sample, cache_k, cache_v, cache_kidx, state_pool, page_table,
              ln1, w_in, pool_w, pool_scale, rel_bias, w_branch_pool, w_branch_attn, w_out,
              ln2, w_mlp_in, w_mlp_out, ln_f):
    xp, xs = x_prompt, x_sample
    kp_l, vp_l, kip_l, pp_l = [], [], [], []
    ks_l, vs_l, kis_l, ps_l = [], [], [], []
    for l in range(DEPTH):
        u, q, k, v, qi, ki, wi, gp, ga = project(xp, ln1[l], w_in[l])
        prefix = jnp.zeros((xp.shape[0], POOL_PREFIX, POOL_WIDTH), xp.dtype)
        pool_o, pool_st = pool_mix(u, prefix, 0, pool_w[l], pool_scale[l])
        attn_o = prompt_attention(q, k, v, qi, ki, wi, rel_bias)
        xp_next = merge_and_mlp(xp, pool_o, attn_o, gp, ga, w_branch_pool[l], w_branch_attn[l],
                                w_out[l], ln2[l], w_mlp_in[l], w_mlp_out[l])
        kp_l.append(k)
        vp_l.append(v)
        kip_l.append(ki)
        pp_l.append(pool_st)
        u, q, k, v, qi, ki, wi, gp, ga = project(xs, ln1[l], w_in[l])
        pool_o, pool_st = pool_mix(u, state_pool[l], PAST_LEN, pool_w[l], pool_scale[l])
        attn_o = sample_attention(q, k, v, qi, ki, wi, cache_k[l], cache_v[l], cache_kidx[l],
                                  page_table, rel_bias)
        xs_next = merge_and_mlp(xs, pool_o, attn_o, gp, ga, w_branch_pool[l], w_branch_attn[l],
                                w_out[l], ln2[l], w_mlp_in[l], w_mlp_out[l])
        ks_l.append(k)
        vs_l.append(v)
        kis_l.append(ki)
        ps_l.append(pool_st)
        xp, xs = xp_next, xs_next
    y_prompt = rmsnorm(xp, ln_f)
    y_sample = rmsnorm(xs, ln_f)
    return (y_prompt, y_sample,
            jnp.stack(kp_l), jnp.stack(vp_l), jnp.stack(kip_l), jnp.stack(pp_l),
            jnp.stack(ks_l), jnp.stack(vs_l), jnp.stack(kis_l), jnp.stack(ps_l))
```

```python
import functools
import math

import numpy as np
import jax
import jax.numpy as jnp
from jax import lax
from jax.experimental import pallas as pl
from jax.experimental.pallas import tpu as pltpu

D_MODEL = 4096
PAST_LEN = 2048
PAGE_SIZE = 128
POOL_WIDTH = D_MODEL // 2
POOL_WINDOWS = (2, 4, 8, 16)
N_POOL_GROUPS = len(POOL_WINDOWS)
POOL_GROUP = POOL_WIDTH // N_POOL_GROUPS
POOL_PREFIX = max(POOL_WINDOWS) - 1
HEAD_DIM = 128
N_HEADS = (D_MODEL // 2) // HEAD_DIM
ATTN_WIDTH = N_HEADS * HEAD_DIM
IDX_HEADS = 32
IDX_DIM = 64
TOPK_MAX = 256
NUM_BUCKETS = 32
MAX_DISTANCE = 128
D_FF = 4 * D_MODEL
EPS = 1e-6

F32 = jnp.float32
BF16 = jnp.bfloat16
I32 = jnp.int32

LANES = 128
VMEM_LIMIT = 56 * 1024 * 1024
INT_MIN = -(2 ** 31)
NEG = -1e30
HALO = 16
PROMPT_HALO = 128
IDX_W_SCALE = (IDX_HEADS ** -0.5) * (IDX_DIM ** -0.5)
ATTN_SCALE = HEAD_DIM ** -0.5


def _cparams(sem):
    return pltpu.CompilerParams(dimension_semantics=sem, vmem_limit_bytes=VMEM_LIMIT)


def _rmsnorm_body(x_ref, g_ref, o_ref):
    x = x_ref[...]
    y = x * lax.rsqrt(jnp.mean(x * x, axis=-1, keepdims=True) + EPS)
    o_ref[...] = (y * g_ref[...]).astype(o_ref.dtype)


def _rmsnorm(x, g, out_dtype, tm=512):
    m, d = x.shape
    return pl.pallas_call(
        _rmsnorm_body,
        grid=(m // tm,),
        in_specs=[pl.BlockSpec((tm, d), lambda i: (i, 0)), pl.BlockSpec((1, d), lambda i: (0, 0))],
        out_specs=pl.BlockSpec((tm, d), lambda i: (i, 0)),
        out_shape=jax.ShapeDtypeStruct((m, d), out_dtype),
        compiler_params=_cparams(("parallel",)),
        name="rmsnorm",
    )(x, g.reshape(1, d))


def _mm_body(*refs, n_out, act, has_res):
    x_ref, w_ref = refs[0], refs[1]
    res_ref = refs[2] if has_res else None
    outs = refs[2 + int(has_res):2 + int(has_res) + n_out]
    acc = jnp.dot(x_ref[...], w_ref[...], preferred_element_type=F32)
    if act == "relu2":
        acc = jnp.square(jnp.maximum(acc, 0.0))
    if has_res:
        acc = res_ref[...] + acc
    for o in outs:
        o[...] = acc.astype(o.dtype)


def _matmul(x, w, out_dtypes, *, tm, tn, act=None, res=None, name="matmul"):
    m, k = x.shape
    n = w.shape[1]
    in_specs = [pl.BlockSpec((tm, k), lambda i, j: (i, 0)), pl.BlockSpec((k, tn), lambda i, j: (0, j))]
    args = [x, w]
    if res is not None:
        in_specs.append(pl.BlockSpec((tm, tn), lambda i, j: (i, j)))
        args.append(res)
    outs = pl.pallas_call(
        functools.partial(_mm_body, n_out=len(out_dtypes), act=act, has_res=res is not None),
        grid=(m // tm, n // tn),
        in_specs=in_specs,
        out_specs=[pl.BlockSpec((tm, tn), lambda i, j: (i, j)) for _ in out_dtypes],
        out_shape=[jax.ShapeDtypeStruct((m, n), dt) for dt in out_dtypes],
        compiler_params=_cparams(("parallel", "parallel")),
        name=name,
    )(*args)
    return outs


def _mmk_body(x_ref, w_ref, res_ref, o_ref, acc_ref, *, nk):
    k = pl.program_id(2)

    @pl.when(k == 0)
    def _():
        acc_ref[...] = jnp.zeros_like(acc_ref)

    acc_ref[...] += jnp.dot(x_ref[...], w_ref[...], preferred_element_type=F32)

    @pl.when(k == nk - 1)
    def _():
        o_ref[...] = res_ref[...] + acc_ref[...]


def _matmul_ktiled(x, w, res, *, tm, tn, tk, name):
    m, k = x.shape
    n = w.shape[1]
    nk = k // tk
    return pl.pallas_call(
        functools.partial(_mmk_body, nk=nk),
        grid=(m // tm, n // tn, nk),
        in_specs=[pl.BlockSpec((tm, tk), lambda i, j, kk: (i, kk)),
                  pl.BlockSpec((tk, tn), lambda i, j, kk: (kk, j)),
                  pl.BlockSpec((tm, tn), lambda i, j, kk: (i, j))],
        out_specs=pl.BlockSpec((tm, tn), lambda i, j, kk: (i, j)),
        out_shape=jax.ShapeDtypeStruct((m, n), F32),
        scratch_shapes=[pltpu.VMEM((tm, tn), F32)],
        compiler_params=_cparams(("parallel", "parallel", "arbitrary")),
        name=name,
    )(x, w, res)


def _merge_body(xn_ref, po_ref, ao_ref, wgp_ref, wga_ref, wbp_ref, wba_ref, o_ref):
    xn = xn_ref[...]
    gp = jnp.dot(xn, wgp_ref[...], preferred_element_type=F32)
    ga = jnp.dot(xn, wga_ref[...], preferred_element_type=F32)
    a = jnp.dot(po_ref[...], wbp_ref[...], preferred_element_type=F32)
    b = jnp.dot(ao_ref[...], wba_ref[...], preferred_element_type=F32)
    o_ref[...] = (jax.nn.sigmoid(gp) * a + jax.nn.sigmoid(ga) * b).astype(o_ref.dtype)


def _merge(xn, po, ao, wgp, wga, wbp, wba, *, tm=512, tn=512):
    m, d = xn.shape
    kb = po.shape[1]
    n = wgp.shape[1]
    row = lambda i, j: (i, 0)
    col = lambda i, j: (0, j)
    return pl.pallas_call(
        _merge_body,
        grid=(m // tm, n // tn),
        in_specs=[pl.BlockSpec((tm, d), row), pl.BlockSpec((tm, kb), row), pl.BlockSpec((tm, kb), row),
                  pl.BlockSpec((d, tn), col), pl.BlockSpec((d, tn), col),
                  pl.BlockSpec((kb, tn), col), pl.BlockSpec((kb, tn), col)],
        out_specs=pl.BlockSpec((tm, tn), lambda i, j: (i, j)),
        out_shape=jax.ShapeDtypeStruct((m, n), BF16),
        compiler_params=_cparams(("parallel", "parallel")),
        name="merge",
    )(xn, po, ao, wgp, wga, wbp, wba)


def _pool_body(*refs, has_halo):
    if has_halo:
        halo_ref, cur_ref, s_ref, inv_ref, pw_ref, sc_ref, o_ref = refs
        halo = jnp.where(pl.program_id(1) == 0, 0.0, halo_ref[0])
        cur = cur_ref[0]
        ext = jnp.concatenate([halo, cur], axis=0)
    else:
        ext_ref, cur_ref, s_ref, inv_ref, pw_ref, sc_ref, o_ref = refs
        ext = ext_ref[...]
        cur = cur_ref[...]
    hi = ext.astype(BF16)
    lo = (ext - hi.astype(F32)).astype(BF16)
    band = s_ref[0]
    win = (jnp.dot(band, hi, preferred_element_type=F32) + jnp.dot(band, lo, preferred_element_type=F32))
    pooled = win * inv_ref[0] - cur
    y = jnp.dot(pooled.astype(BF16), pw_ref[0], preferred_element_type=F32) * sc_ref[...]
    if has_halo:
        o_ref[0] = y.astype(o_ref.dtype)
    else:
        o_ref[...] = y.astype(o_ref.dtype)


def _pool_prompt(u, pool_w, pool_scale, *, tb=512):
    n, t, _ = u.shape
    halo = PROMPT_HALO
    band = np.zeros((N_POOL_GROUPS, tb, halo + tb), np.float32)
    inv = np.zeros((N_POOL_GROUPS, t, 1), np.float32)
    r = np.arange(tb)[:, None]
    c = np.arange(halo + tb)[None, :]
    for g, w in enumerate(POOL_WINDOWS):
        band[g] = ((c >= r + halo - w + 1) & (c <= r + halo)).astype(np.float32)
        inv[g, :, 0] = 1.0 / np.minimum(w, np.arange(t) + 1)
    hb = tb // halo
    return pl.pallas_call(
        functools.partial(_pool_body, has_halo=True),
        grid=(n, t // tb, N_POOL_GROUPS),
        in_specs=[pl.BlockSpec((1, halo, POOL_GROUP), lambda b, i, g: (b, jnp.maximum(i * hb - 1, 0), g)),
                  pl.BlockSpec((1, tb, POOL_GROUP), lambda b, i, g: (b, i, g)),
                  pl.BlockSpec((1, tb, halo + tb), lambda b, i, g: (g, 0, 0)),
                  pl.BlockSpec((1, tb, 1), lambda b, i, g: (g, i, 0)),
                  pl.BlockSpec((1, POOL_GROUP, POOL_GROUP), lambda b, i, g: (g, 0, 0)),
                  pl.BlockSpec((1, POOL_GROUP), lambda b, i, g: (0, g))],
        out_specs=pl.BlockSpec((1, tb, POOL_GROUP), lambda b, i, g: (b, i, g)),
        out_shape=jax.ShapeDtypeStruct((n, t, POOL_WIDTH), BF16),
        compiler_params=_cparams(("parallel", "arbitrary", "arbitrary")),
        name="pool_prompt",
    )(u, u, jnp.asarray(band, BF16), jnp.asarray(inv), pool_w, pool_scale.reshape(1, POOL_WIDTH))


def _pool_sample(u, prefix, pool_w, pool_scale, *, sb=16):
    n, t, _ = u.shape
    e = HALO + t
    ext = jnp.concatenate([jnp.zeros((n, HALO - POOL_PREFIX, POOL_WIDTH), F32), prefix.astype(F32), u], axis=1)
    ext = ext.reshape(n * e, POOL_WIDTH)
    band = np.zeros((N_POOL_GROUPS, sb * t, sb * e), np.float32)
    inv = np.zeros((N_POOL_GROUPS, sb * t, 1), np.float32)
    r = np.arange(sb * t)[:, None]
    c = np.arange(sb * e)[None, :]
    for g, w in enumerate(POOL_WINDOWS):
        pos = HALO + r % t
        band[g] = ((r // t == c // e) & (c % e >= pos - w + 1) & (c % e <= pos)).astype(np.float32)
        inv[g] = 1.0 / w
    out = pl.pallas_call(
        functools.partial(_pool_body, has_halo=False),
        grid=(n // sb, N_POOL_GROUPS),
        in_specs=[pl.BlockSpec((sb * e, POOL_GROUP), lambda i, g: (i, g)),
                  pl.BlockSpec((sb * t, POOL_GROUP), lambda i, g: (i, g)),
                  pl.BlockSpec((1, sb * t, sb * e), lambda i, g: (g, 0, 0)),
                  pl.BlockSpec((1, sb * t, 1), lambda i, g: (g, 0, 0)),
                  pl.BlockSpec((1, POOL_GROUP, POOL_GROUP), lambda i, g: (g, 0, 0)),
                  pl.BlockSpec((1, POOL_GROUP), lambda i, g: (0, g))],
        out_specs=pl.BlockSpec((sb * t, POOL_GROUP), lambda i, g: (i, g)),
        out_shape=jax.ShapeDtypeStruct((n * t, POOL_WIDTH), BF16),
        compiler_params=_cparams(("parallel", "arbitrary")),
        name="pool_sample",
    )(ext, u.reshape(n * t, POOL_WIDTH), jnp.asarray(band, BF16), jnp.asarray(inv), pool_w,
      pool_scale.reshape(1, POOL_WIDTH))
    return out


def _bucket_thresholds():
    d = np.arange(0, 4 * MAX_DISTANCE)
    max_exact = NUM_BUCKETS // 2
    df = np.maximum(d, 1).astype(np.float32)
    large = max_exact + (np.log(df / np.float32(max_exact)) / np.float32(math.log(MAX_DISTANCE / max_exact))
                         * np.float32(NUM_BUCKETS - max_exact)).astype(np.int32)
    bucket = np.where(d < max_exact, d, np.minimum(large, NUM_BUCKETS - 1))
    assert np.all(np.diff(bucket) >= 0) and np.all(bucket[MAX_DISTANCE + 1:] == NUM_BUCKETS - 1)
    return [int(np.argmax(bucket >= b)) for b in range(NUM_BUCKETS)]


def _bias_prompt_body(rb_ref, o_ref):
    h = pl.program_id(0)
    thr = _bucket_thresholds()
    base = lax.broadcasted_iota(I32, (LANES, LANES), 0) - lax.broadcasted_iota(I32, (LANES, LANES), 1)
    for j in range(2):
        dist = base + j * LANES
        tile = jnp.full((LANES, LANES), rb_ref[0, h], F32)
        for b in range(1, NUM_BUCKETS):
            tile = jnp.where(dist >= thr[b], rb_ref[b, h], tile)
        o_ref[0, j] = tile - rb_ref[NUM_BUCKETS - 1, h]


def _bias_sample_body(rbx_ref, page_ref, new_ref, *, t):
    thr = _bucket_thresholds()
    for o_ref, rows, off in ((page_ref, PAGE_SIZE, PAGE_SIZE), (new_ref, t, 0)):
        qq = lax.broadcasted_iota(I32, (rows, LANES), 1) % t
        dist = off + qq - lax.broadcasted_iota(I32, (rows, LANES), 0)
        tile = jnp.broadcast_to(rbx_ref[0:1, :], (rows, LANES))
        for b in range(1, NUM_BUCKETS):
            tile = jnp.where(dist >= thr[b], rbx_ref[b:b + 1, :], tile)
        o_ref[...] = tile - rbx_ref[NUM_BUCKETS - 1:NUM_BUCKETS, :]


def _bias_tables(rel_bias, t):
    rb = rel_bias.astype(F32)
    tp = pl.pallas_call(
        _bias_prompt_body,
        grid=(N_HEADS,),
        in_specs=[pl.BlockSpec(memory_space=pltpu.SMEM)],
        out_specs=pl.BlockSpec((1, 2, LANES, LANES), lambda h: (h, 0, 0, 0)),
        out_shape=jax.ShapeDtypeStruct((N_HEADS, 2, LANES, LANES), F32),
        name="bias_prompt",
    )(rb)
    rbx = jnp.repeat(rb, LANES // N_HEADS, axis=1)
    ts_page, ts_new = pl.pallas_call(
        functools.partial(_bias_sample_body, t=t),
        out_shape=[jax.ShapeDtypeStruct((PAGE_SIZE, LANES), F32), jax.ShapeDtypeStruct((t, LANES), F32)],
        name="bias_sample",
    )(rbx)
    return tp, ts_page, ts_new


def _sortable(score):
    bits = lax.bitcast_convert_type(score, I32)
    return bits ^ ((bits >> 31) & 0x7FFFFFFF)


def _select_topk(key_ref, jcut_ref, krow, ncol, tq):
    def counts(pred):
        def body(c, acc):
            blk = key_ref[:, pl.ds(pl.multiple_of(c * LANES, LANES), LANES)]
            col = c * LANES + lax.broadcasted_iota(I32, (tq, LANES), 1)
            return acc + pred(blk, col).astype(I32)
        acc = lax.fori_loop(0, ncol, body, jnp.zeros((tq, LANES), I32))
        return jnp.sum(acc, axis=1, keepdims=True)

    def count_ge(cand):
        cb = jnp.broadcast_to(cand, (tq, LANES))
        return counts(lambda blk, col: blk >= cb)

    zero = jnp.zeros((tq, 1), I32)
    v = jnp.where(count_ge(zero) >= krow, zero, jnp.full((tq, 1), INT_MIN, I32))

    def bit_body(bi, v):
        cand = v | jnp.left_shift(jnp.int32(1), 30 - bi)
        return jnp.where(count_ge(cand) >= krow, cand, v)

    v = lax.fori_loop(0, 31, bit_body, v)

    vb = jnp.broadcast_to(v, (tq, LANES))
    n_gt = counts(lambda blk, col: blk > vb)
    n_ge = counts(lambda blk, col: blk >= vb)
    need = krow - n_gt
    jcut_ref[...] = jnp.full((tq, LANES), ncol * LANES, I32)

    @pl.when(jnp.max(n_ge - krow) > 0)
    def _():
        def jbit(bi, x):
            cand = x + jnp.left_shift(jnp.int32(1), 15 - bi)
            cb = jnp.broadcast_to(cand, (tq, LANES))
            cnt = counts(lambda blk, col: (blk == vb) & (col <= cb))
            return jnp.where(cnt < need, cand, x)
        x = lax.fori_loop(0, 16, jbit, jnp.full((tq, 1), -1, I32))
        jcut_ref[...] = jnp.broadcast_to(x + 2, (tq, LANES))

    return v


def _write_mask(key_ref, jcut_ref, mask_store, v, ncol, ncol_total, tq):
    vb = jnp.broadcast_to(v, (tq, LANES))
    jc = jcut_ref[...]

    def body(c, _):
        off = pl.multiple_of(c * LANES, LANES)
        blk = key_ref[:, pl.ds(off, LANES)]
        col = c * LANES + lax.broadcasted_iota(I32, (tq, LANES), 1)
        sel = (blk > vb) | ((blk == vb) & (col < jc))
        mask_store(off, jnp.where(sel, 1.0, 0.0).astype(BF16))
        return 0

    lax.fori_loop(0, ncol, body, 0)

    def zbody(c, _):
        mask_store(pl.multiple_of(c * LANES, LANES), jnp.zeros((tq, LANES), BF16))
        return 0

    lax.fori_loop(ncol, ncol_total, zbody, 0)


P1_TQ = 256
P1_CK = 256


def _p1_body(qi_ref, kbd_ref, wi_ref, mask_ref, key_ref, jcut_ref, wb_ref, *, s):
    i = pl.program_id(1)
    t0 = i * P1_TQ
    n_ch = (i + 1) * (P1_TQ // P1_CK)

    for hh in range(IDX_HEADS):
        wb_ref[hh] = jnp.broadcast_to(wi_ref[0, :, hh:hh + 1] * IDX_W_SCALE, (P1_TQ, LANES))

    def chunk_body(c, _):
        col0 = pl.multiple_of(c * P1_CK, P1_CK)
        kb = kbd_ref[0, :, pl.ds(pl.multiple_of(c * 2 * P1_CK, 2 * P1_CK), 2 * P1_CK)]
        for r in range(P1_TQ // LANES):
            rows = slice(r * LANES, (r + 1) * LANES)
            acc = [jnp.zeros((LANES, LANES), F32) for _ in range(P1_CK // LANES)]
            for hp in range(IDX_HEADS // 2):
                d2 = jnp.dot(qi_ref[0, rows, hp * LANES:(hp + 1) * LANES], kb, preferred_element_type=F32)
                for half in range(2):
                    w = wb_ref[2 * hp + half, rows, :]
                    for cc in range(P1_CK // LANES):
                        lo = half * P1_CK + cc * LANES
                        acc[cc] = acc[cc] + w * jnp.maximum(d2[:, lo:lo + LANES], 0.0)
            for cc in range(P1_CK // LANES):
                trow = t0 + r * LANES + lax.broadcasted_iota(I32, (LANES, LANES), 0)
                col = col0 + cc * LANES + lax.broadcasted_iota(I32, (LANES, LANES), 1)
                key = jnp.where(col <= trow, _sortable(acc[cc]), INT_MIN)
                key_ref[rows, pl.ds(pl.multiple_of(col0 + cc * LANES, LANES), LANES)] = key
        return 0

    lax.fori_loop(0, n_ch, chunk_body, 0)

    krow = jnp.minimum(TOPK_MAX, t0 + lax.broadcasted_iota(I32, (P1_TQ, 1), 0) + 1)
    ncol = n_ch * (P1_CK // LANES)
    v = _select_topk(key_ref, jcut_ref, krow, ncol, P1_TQ)

    def store(off, val):
        mask_ref[0, :, pl.ds(off, LANES)] = val

    _write_mask(key_ref, jcut_ref, store, v, ncol, s // LANES, P1_TQ)


def _prompt_select(qi, kbd, wi):
    b, s, _ = qi.shape
    return pl.pallas_call(
        functools.partial(_p1_body, s=s),
        grid=(b, s // P1_TQ),
        in_specs=[pl.BlockSpec((1, P1_TQ, IDX_HEADS * IDX_DIM), lambda bb, i: (bb, i, 0)),
                  pl.BlockSpec((1, 2 * IDX_DIM, 2 * s), lambda bb, i: (bb, 0, 0)),
                  pl.BlockSpec((1, P1_TQ, IDX_HEADS), lambda bb, i: (bb, i, 0))],
        out_specs=pl.BlockSpec((1, P1_TQ, s), lambda bb, i: (bb, i, 0)),
        out_shape=jax.ShapeDtypeStruct((b, s, s), BF16),
        scratch_shapes=[pltpu.VMEM((P1_TQ, s), I32), pltpu.VMEM((P1_TQ, LANES), I32),
                        pltpu.VMEM((IDX_HEADS, P1_TQ, LANES), F32)],
        compiler_params=_cparams(("parallel", "arbitrary")),
        name="prompt_select",
    )(qi, kbd, wi)


P2_TQ = 256


def _p2_body(q_ref, k_ref, v_ref, mask_ref, tb_ref, o_ref, lg_ref):
    i = pl.program_id(1)
    n_ch = i + 1
    tq = P2_TQ
    q = q_ref[0]

    def logits_body(c, _):
        off = pl.multiple_of(c * tq, tq)
        kc = k_ref[0, pl.ds(off, tq), :]
        sc = lax.dot_general(q, kc, (((1,), (1,)), ((), ())), preferred_element_type=F32) * ATTN_SCALE
        lg_ref[:, pl.ds(off, tq)] = sc
        return 0

    lax.fori_loop(0, n_ch, logits_body, 0)

    for a in range(tq // LANES):
        rows = slice(a * LANES, (a + 1) * LANES)
        dcol = pl.multiple_of(i * tq + a * LANES, LANES)
        lg_ref[rows, pl.ds(dcol, LANES)] += tb_ref[0, 0]
        if a > 0:
            scol = pl.multiple_of(i * tq + (a - 1) * LANES, LANES)
            lg_ref[rows, pl.ds(scol, LANES)] += tb_ref[0, 1]

    @pl.when(i > 0)
    def _():
        scol = pl.multiple_of(i * tq - LANES, LANES)
        lg_ref[0:LANES, pl.ds(scol, LANES)] += tb_ref[0, 1]

    def max_body(c, m):
        off = pl.multiple_of(c * tq, tq)
        sc = jnp.where(mask_ref[0, :, pl.ds(off, tq)] > 0, lg_ref[:, pl.ds(off, tq)], NEG)
        lg_ref[:, pl.ds(off, tq)] = sc
        return jnp.maximum(m, jnp.max(sc, axis=1, keepdims=True))

    m = lax.fori_loop(0, n_ch, max_body, jnp.full((tq, 1), NEG, F32))

    def pv_body(c, carry):
        l, acc = carry
        off = pl.multiple_of(c * tq, tq)
        p = jnp.exp(lg_ref[:, pl.ds(off, tq)] - m)
        l = l + jnp.sum(p, axis=1, keepdims=True)
        acc = acc + jnp.dot(p.astype(BF16), v_ref[0, pl.ds(off, tq), :], preferred_element_type=F32)
        return l, acc

    l, acc = lax.fori_loop(0, n_ch, pv_body, (jnp.zeros((tq, 1), F32), jnp.zeros((tq, HEAD_DIM), F32)))
    o_ref[0] = (acc / l).astype(o_ref.dtype)


def _prompt_attention(q, k, v, mask, tb):
    b, s, _ = q.shape
    return pl.pallas_call(
        _p2_body,
        grid=(b, s // P2_TQ, N_HEADS),
        in_specs=[pl.BlockSpec((1, P2_TQ, HEAD_DIM), lambda bb, i, h: (bb, i, h)),
                  pl.BlockSpec((1, s, HEAD_DIM), lambda bb, i, h: (bb, 0, h)),
                  pl.BlockSpec((1, s, HEAD_DIM), lambda bb, i, h: (bb, 0, h)),
                  pl.BlockSpec((1, P2_TQ, s), lambda bb, i, h: (bb, i, 0)),
                  pl.BlockSpec((1, 2, LANES, LANES), lambda bb, i, h: (h, 0, 0, 0))],
        out_specs=pl.BlockSpec((1, P2_TQ, HEAD_DIM), lambda bb, i, h: (bb, i, h)),
        out_shape=jax.ShapeDtypeStruct((b, s, ATTN_WIDTH), BF16),
        scratch_shapes=[pltpu.VMEM((P2_TQ, s), F32)],
        compiler_params=_cparams(("parallel", "parallel", "arbitrary")),
        name="prompt_attention",
    )(q, k, v, mask, tb)


def _s1_body(pt_ref, qi_ref, w_ref, *refs, n_pages, t):
    page_refs = refs[:n_pages + 1]
    o_ref = refs[n_pages + 1]
    qi = qi_ref[0]
    w = jnp.broadcast_to(w_ref[0] * IDX_W_SCALE, (IDX_HEADS * t, LANES))
    for p, kp_ref in enumerate(page_refs):
        kp = kp_ref[0].astype(BF16)
        d = lax.dot_general(qi, kp, (((1,), (1,)), ((), ())), preferred_element_type=F32)
        wr = (w * jnp.maximum(d, 0.0)).reshape(IDX_HEADS, t, LANES)
        o_ref[0, :, p * LANES:(p + 1) * LANES] = jnp.sum(wr, axis=0)


def _sample_scores(qi_hq, wi_hq, cache_kidx, ki_new_pad, page_table):
    n, rows, _ = qi_hq.shape
    t = rows // IDX_HEADS
    n_pages = page_table.shape[1]
    lp = (n_pages + 1) * PAGE_SIZE

    def page_spec(p):
        return pl.BlockSpec((1, PAGE_SIZE, IDX_DIM), lambda b, pt: (pt[b, p], 0, 0))

    grid_spec = pltpu.PrefetchScalarGridSpec(
        num_scalar_prefetch=1,
        grid=(n,),
        in_specs=[pl.BlockSpec((1, rows, IDX_DIM), lambda b, pt: (b, 0, 0)),
                  pl.BlockSpec((1, rows, 1), lambda b, pt: (b, 0, 0))]
                 + [page_spec(p) for p in range(n_pages)]
                 + [pl.BlockSpec((1, PAGE_SIZE, IDX_DIM), lambda b, pt: (b, 0, 0))],
        out_specs=pl.BlockSpec((1, t, lp), lambda b, pt: (b, 0, 0)),
    )
    return pl.pallas_call(
        functools.partial(_s1_body, n_pages=n_pages, t=t),
        grid_spec=grid_spec,
        out_shape=jax.ShapeDtypeStruct((n, t, lp), F32),
        compiler_params=_cparams(("arbitrary",)),
        name="sample_scores",
    )(page_table, qi_hq, wi_hq, *([cache_kidx] * n_pages), ki_new_pad)


S2_TQ = 256


def _s2_body(sc_ref, mask_ref, key_ref, jcut_ref, *, t, lp):
    r0 = pl.program_id(0) * S2_TQ
    ncol = lp // LANES
    lim = PAST_LEN + (r0 + lax.broadcasted_iota(I32, (S2_TQ, LANES), 0)) % t + 1
    for c in range(ncol):
        col = c * LANES + lax.broadcasted_iota(I32, (S2_TQ, LANES), 1)
        key_ref[:, c * LANES:(c + 1) * LANES] = jnp.where(col < lim, _sortable(sc_ref[:, c * LANES:(c + 1) * LANES]),
                                                          INT_MIN)
    krow = jnp.full((S2_TQ, 1), min(TOPK_MAX, (PAST_LEN + t) // 4), I32)
    v = _select_topk(key_ref, jcut_ref, krow, ncol, S2_TQ)

    def store(off, val):
        mask_ref[:, pl.ds(off, LANES)] = val

    _write_mask(key_ref, jcut_ref, store, v, ncol, ncol, S2_TQ)


def _sample_select(scores, t):
    r, lp = scores.shape
    return pl.pallas_call(
        functools.partial(_s2_body, t=t, lp=lp),
        grid=(r // S2_TQ,),
        in_specs=[pl.BlockSpec((S2_TQ, lp), lambda i: (i, 0))],
        out_specs=pl.BlockSpec((S2_TQ, lp), lambda i: (i, 0)),
        out_shape=jax.ShapeDtypeStruct((r, lp), BF16),
        scratch_shapes=[pltpu.VMEM((S2_TQ, lp), I32), pltpu.VMEM((S2_TQ, LANES), I32)],
        compiler_params=_cparams(("parallel",)),
        name="sample_select",
    )(scores)


S3_GROUP = 4


def _s3_body(pt_ref, qbd_ref, mask_ref, *refs, n_pages, t):
    ng = n_pages // S3_GROUP
    k_refs = refs[:S3_GROUP]
    v_refs = refs[S3_GROUP:2 * S3_GROUP]
    knew_ref, vnew_ref, tsp_ref, tsn_ref, o_ref, lg_ref, acc_ref, vpad_ref = refs[2 * S3_GROUP:]
    g = pl.program_id(1)
    past = n_pages * PAGE_SIZE

    def masked_logits(kf32, msk, bias):
        lg = jnp.dot(kf32.astype(BF16), qbd_ref[0], preferred_element_type=F32) * ATTN_SCALE
        if bias is not None:
            lg = lg + bias
        return jnp.where(msk > 0, lg, NEG)

    @pl.when(g < ng)
    def _():
        for j in range(S3_GROUP):
            row0 = pl.multiple_of((g * S3_GROUP + j) * PAGE_SIZE, PAGE_SIZE)
            bias = jnp.where(g == ng - 1, tsp_ref[...], 0.0) if j == S3_GROUP - 1 else None
            lg_ref[pl.ds(row0, PAGE_SIZE), :] = masked_logits(k_refs[j][0], mask_ref[0, pl.ds(row0, PAGE_SIZE), :], bias)

    @pl.when(g == ng - 1)
    def _():
        msk_new = mask_ref[0, past:past + 2 * t, :].astype(F32)[:t]
        lg_ref[past:past + t, :] = masked_logits(knew_ref[0], msk_new, tsn_ref[...])
        lg_ref[past + t:, :] = jnp.full((PAGE_SIZE - t, LANES), NEG, F32)

    @pl.when(g == ng)
    def _():
        lg = lg_ref[...]
        p = jnp.exp(lg - jnp.max(lg, axis=0, keepdims=True))
        lg_ref[...] = p / jnp.sum(p, axis=0, keepdims=True)
        acc_ref[...] = jnp.zeros_like(acc_ref)

    def add_pv(row0, vf32):
        pt = jnp.transpose(lg_ref[pl.ds(row0, PAGE_SIZE), :]).astype(BF16)
        acc_ref[...] += jnp.dot(pt, vf32.astype(BF16), preferred_element_type=F32)

    @pl.when(g >= ng)
    def _():
        for j in range(S3_GROUP):
            add_pv(pl.multiple_of(((g - ng) * S3_GROUP + j) * PAGE_SIZE, PAGE_SIZE), v_refs[j][0])

    @pl.when(g == 2 * ng - 1)
    def _():
        vpad_ref[...] = jnp.zeros_like(vpad_ref)
        vpad_ref[0:t, :] = vnew_ref[0]
        add_pv(past, vpad_ref[...])
        for h in range(N_HEADS):
            o_ref[0, :, h * HEAD_DIM:(h + 1) * HEAD_DIM] = acc_ref[h * t:(h + 1) * t, h * HEAD_DIM:(h + 1) * HEAD_DIM]


def _sample_attention(qbd, mask_t, cache_k, cache_v, k_new, v_new, ts_page, ts_new, page_table):
    n, t, _ = k_new.shape
    n_pages = page_table.shape[1]
    ng = n_pages // S3_GROUP
    lp = mask_t.shape[1]

    def k_spec(j):
        return pl.BlockSpec((1, PAGE_SIZE, ATTN_WIDTH),
                            lambda b, g, pt: (pt[b, jnp.minimum(g, ng - 1) * S3_GROUP + j], 0, 0))

    def v_spec(j):
        return pl.BlockSpec((1, PAGE_SIZE, ATTN_WIDTH),
                            lambda b, g, pt: (pt[b, jnp.maximum(g - ng, 0) * S3_GROUP + j], 0, 0))

    per_seq = lambda b, g, pt: (b, 0, 0)
    const2 = lambda b, g, pt: (0, 0)
    grid_spec = pltpu.PrefetchScalarGridSpec(
        num_scalar_prefetch=1,
        grid=(n, 2 * ng),
        in_specs=[pl.BlockSpec((1, ATTN_WIDTH, LANES), per_seq), pl.BlockSpec((1, lp, LANES), per_seq)]
                 + [k_spec(j) for j in range(S3_GROUP)] + [v_spec(j) for j in range(S3_GROUP)]
                 + [pl.BlockSpec((1, t, ATTN_WIDTH), per_seq), pl.BlockSpec((1, t, ATTN_WIDTH), per_seq),
                    pl.BlockSpec((PAGE_SIZE, LANES), const2), pl.BlockSpec((t, LANES), const2)],
        out_specs=pl.BlockSpec((1, t, ATTN_WIDTH), per_seq),
        scratch_shapes=[pltpu.VMEM((lp, LANES), F32), pltpu.VMEM((LANES, ATTN_WIDTH), F32),
                        pltpu.VMEM((PAGE_SIZE, ATTN_WIDTH), F32)],
    )
    return pl.pallas_call(
        functools.partial(_s3_body, n_pages=n_pages, t=t),
        grid_spec=grid_spec,
        out_shape=jax.ShapeDtypeStruct((n, t, ATTN_WIDTH), F32),
        compiler_params=_cparams(("arbitrary", "arbitrary")),
        name="sample_attention",
    )(page_table, qbd, mask_t, *([cache_k] * S3_GROUP), *([cache_v] * S3_GROUP), k_new, v_new, ts_page, ts_new)


def _split_w_in(w):
    o = [0]
    for s in (POOL_WIDTH, ATTN_WIDTH, ATTN_WIDTH, ATTN_WIDTH, IDX_HEADS * IDX_DIM, IDX_DIM, IDX_HEADS, D_MODEL,
              D_MODEL):
        o.append(o[-1] + s)
    cut = lambda a, b: w[:, o[a]:o[b]].astype(BF16)
    kiwi = jnp.pad(cut(5, 7), ((0, 0), (0, LANES - IDX_DIM - IDX_HEADS)))
    return dict(u=cut(0, 1), q=cut(1, 2), k=cut(2, 3), v=cut(3, 4), qi=cut(4, 5), kiwi=kiwi, gp=cut(7, 8),
                ga=cut(8, 9))


def _project(x2d, ln1, wi):
    m = x2d.shape[0]
    tm = min(m, 1024)
    xn = _rmsnorm(x2d, ln1, BF16)
    (u,) = _matmul(xn, wi["u"], [F32], tm=tm, tn=1024, name="proj_u")
    (q,) = _matmul(xn, wi["q"], [BF16], tm=tm, tn=1024, name="proj_q")
    k, k16 = _matmul(xn, wi["k"], [F32, BF16], tm=tm, tn=1024, name="proj_k")
    v, v16 = _matmul(xn, wi["v"], [F32, BF16], tm=tm, tn=1024, name="proj_v")
    (qi,) = _matmul(xn, wi["qi"], [BF16], tm=tm, tn=1024, name="proj_qi")
    (kiwi,) = _matmul(xn, wi["kiwi"], [F32], tm=tm, tn=LANES, name="proj_kiwi")
    ki = kiwi[:, :IDX_DIM]
    wgt = kiwi[:, IDX_DIM:IDX_DIM + IDX_HEADS]
    return xn, u, q, k, k16, v, v16, qi, ki, wgt


def _finish(x2d, xn, pool_o, attn_o, wi, w_bp, w_ba, w_out, ln2, w1, w2, ln_f):
    m = x2d.shape[0]
    tm = min(m, 1024)
    mg = _merge(xn, pool_o, attn_o, wi["gp"], wi["ga"], w_bp, w_ba)
    (x1,) = _matmul(mg, w_out, [F32], tm=tm, tn=1024, res=x2d, name="out_proj")
    hn = _rmsnorm(x1, ln2, BF16)
    (h,) = _matmul(hn, w1, [BF16], tm=tm, tn=1024, act="relu2", name="mlp_in")
    x2 = _matmul_ktiled(h, w2, x1, tm=tm, tn=1024, tk=2048, name="mlp_out")
    return _rmsnorm(x2, ln_f, F32)


def kernel(x_prompt, x_sample, cache_k, cache_v, cache_kidx, state_pool, page_table, ln1, w_in, pool_w,
           pool_scale, rel_bias, w_branch_pool, w_branch_attn, w_out, ln2, w_mlp_in, w_mlp_out, ln_f):
    bn, s, _ = x_prompt.shape
    n, t, _ = x_sample.shape
    depth = w_in.shape[0]
    assert depth == 1, "one layer: the caches and pooling state of layer 0 are the only ones read"
    n_pages = page_table.shape[1]
    n_phys = cache_k.shape[1]

    wi = _split_w_in(w_in[0])
    pw = pool_w[0].astype(BF16)
    w_bp = w_branch_pool[0].astype(BF16)
    w_ba = w_branch_attn[0].astype(BF16)
    wo = w_out[0].astype(BF16)
    w1 = w_mlp_in[0].astype(BF16)
    w2 = w_mlp_out[0].astype(BF16)
    tb_prompt, ts_page, ts_new = _bias_tables(rel_bias, t)

    xp = x_prompt.reshape(bn * s, D_MODEL)
    xn, u, q, k, k16, v, v16, qi, ki, wgt = _project(xp, ln1[0], wi)
    pool_o = _pool_prompt(u.reshape(bn, s, POOL_WIDTH), pw, pool_scale[0]).reshape(bn * s, POOL_WIDTH)
    ki_t = jnp.swapaxes(ki.reshape(bn, s, IDX_DIM), 1, 2).astype(BF16).reshape(bn, IDX_DIM, s // P1_CK, P1_CK)
    zeros = jnp.zeros_like(ki_t)
    kbd = jnp.concatenate([jnp.concatenate([ki_t, zeros], axis=-1), jnp.concatenate([zeros, ki_t], axis=-1)],
                          axis=1).reshape(bn, 2 * IDX_DIM, 2 * s)
    mask = _prompt_select(qi.reshape(bn, s, IDX_HEADS * IDX_DIM), kbd, wgt.reshape(bn, s, IDX_HEADS))
    attn_o = _prompt_attention(q.reshape(bn, s, ATTN_WIDTH), k16.reshape(bn, s, ATTN_WIDTH),
                               v16.reshape(bn, s, ATTN_WIDTH), mask, tb_prompt).reshape(bn * s, ATTN_WIDTH)
    y_prompt = _finish(xp, xn, pool_o, attn_o, wi, w_bp, w_ba, wo, ln2[0], w1, w2, ln_f).reshape(bn, s, D_MODEL)
    k_prompt = k.reshape(1, bn, s, N_HEADS, HEAD_DIM)
    v_prompt = v.reshape(1, bn, s, N_HEADS, HEAD_DIM)
    kidx_prompt = ki.reshape(1, bn, s, IDX_DIM)
    pool_prompt = u.reshape(bn, s, POOL_WIDTH)[:, s - POOL_PREFIX:, :][None]

    xs = x_sample.reshape(n * t, D_MODEL)
    xn, u, q, k, k16, v, v16, qi, ki, wgt = _project(xs, ln1[0], wi)
    u3 = u.reshape(n, t, POOL_WIDTH)
    pool_o = _pool_sample(u3, state_pool[0], pw, pool_scale[0])
    qi_hq = qi.reshape(n, t, IDX_HEADS, IDX_DIM).transpose(0, 2, 1, 3).reshape(n, IDX_HEADS * t, IDX_DIM)
    wi_hq = wgt.reshape(n, t, IDX_HEADS).transpose(0, 2, 1).reshape(n, IDX_HEADS * t, 1)
    ki_new_pad = jnp.pad(ki.reshape(n, t, IDX_DIM), ((0, 0), (0, PAGE_SIZE - t), (0, 0)))
    scores = _sample_scores(qi_hq, wi_hq, cache_kidx[0], ki_new_pad, page_table)
    lp = scores.shape[-1]
    smask = _sample_select(scores.reshape(n * t, lp), t)
    mask_t = jnp.tile(jnp.swapaxes(smask.reshape(n, t, lp), 1, 2), (1, 1, N_HEADS))
    q_t = q.reshape(n, t, N_HEADS, HEAD_DIM).transpose(0, 2, 3, 1)
    eye = jnp.eye(N_HEADS, dtype=BF16)
    qbd = (q_t[:, :, :, None, :] * eye[None, :, None, :, None]).reshape(n, ATTN_WIDTH, N_HEADS * t)
    attn_o = _sample_attention(qbd, mask_t, cache_k[0].reshape(n_phys, PAGE_SIZE, ATTN_WIDTH),
                               cache_v[0].reshape(n_phys, PAGE_SIZE, ATTN_WIDTH), k.reshape(n, t, ATTN_WIDTH),
                               v.reshape(n, t, ATTN_WIDTH), ts_page, ts_new, page_table)
    attn_o = attn_o.reshape(n * t, ATTN_WIDTH).astype(BF16)
    y_sample = _finish(xs, xn, pool_o, attn_o, wi, w_bp, w_ba, wo, ln2[0], w1, w2, ln_f).reshape(n, t, D_MODEL)
    k_sample = k.reshape(1, n, t, N_HEADS, HEAD_DIM)
    v_sample = v.reshape(1, n, t, N_HEADS, HEAD_DIM)
    kidx_sample = ki.reshape(1, n, t, IDX_DIM)
    pool_sample = jnp.concatenate([state_pool[0][:, t:, :], u3], axis=1)[None]

    return (y_prompt, y_sample, k_prompt, v_prompt, kidx_prompt, pool_prompt,
            k_sample, v_sample, kidx_sample, pool_sample)
```

```python
import functools
import math

import numpy as np
import jax
import jax.numpy as jnp
from jax import lax
from jax.experimental import pallas as pl
from jax.experimental.pallas import tpu as pltpu

D_MODEL = 4096
PAST_LEN = 2048
PAGE_SIZE = 128
POOL_WIDTH = D_MODEL // 2
POOL_WINDOWS = (2, 4, 8, 16)
N_POOL_GROUPS = len(POOL_WINDOWS)
POOL_GROUP = POOL_WIDTH // N_POOL_GROUPS
POOL_PREFIX = max(POOL_WINDOWS) - 1
HEAD_DIM = 128
N_HEADS = (D_MODEL // 2) // HEAD_DIM
ATTN_WIDTH = N_HEADS * HEAD_DIM
IDX_HEADS = 32
IDX_DIM = 64
TOPK_MAX = 256
NUM_BUCKETS = 32
MAX_DISTANCE = 128
D_FF = 4 * D_MODEL
EPS = 1e-6

F32 = jnp.float32
BF16 = jnp.bfloat16
I32 = jnp.int32

LANES = 128
VMEM_LIMIT = 56 * 1024 * 1024
INT_MIN = -(2 ** 31)
NEG = -1e30
HALO = 16
PROMPT_HALO = 128
IDX_W_SCALE = (IDX_HEADS ** -0.5) * (IDX_DIM ** -0.5)
ATTN_SCALE = HEAD_DIM ** -0.5
LOG2E = math.log2(math.e)


def _cparams(sem):
    return pltpu.CompilerParams(dimension_semantics=sem, vmem_limit_bytes=VMEM_LIMIT)


def _rmsnorm_body(x_ref, g_ref, o_ref):
    x = x_ref[...]
    y = x * lax.rsqrt(jnp.mean(x * x, axis=-1, keepdims=True) + EPS)
    o_ref[...] = (y * g_ref[...]).astype(o_ref.dtype)


def _rmsnorm(x, g, out_dtype, tm=512):
    m, d = x.shape
    return pl.pallas_call(
        _rmsnorm_body,
        grid=(m // tm,),
        in_specs=[pl.BlockSpec((tm, d), lambda i: (i, 0)), pl.BlockSpec((1, d), lambda i: (0, 0))],
        out_specs=pl.BlockSpec((tm, d), lambda i: (i, 0)),
        out_shape=jax.ShapeDtypeStruct((m, d), out_dtype),
        compiler_params=_cparams(("parallel",)),
        name="rmsnorm",
    )(x, g.reshape(1, d))


def _mm_body(*refs, n_out, act, has_res):
    x_ref, w_ref = refs[0], refs[1]
    res_ref = refs[2] if has_res else None
    outs = refs[2 + int(has_res):2 + int(has_res) + n_out]
    acc = jnp.dot(x_ref[...], w_ref[...], preferred_element_type=F32)
    if act == "relu2":
        acc = jnp.square(jnp.maximum(acc, 0.0))
    if has_res:
        acc = res_ref[...] + acc
    for o in outs:
        o[...] = acc.astype(o.dtype)


def _matmul(x, w, out_dtypes, *, tm, tn, act=None, res=None, name="matmul"):
    m, k = x.shape
    n = w.shape[1]
    in_specs = [pl.BlockSpec((tm, k), lambda i, j: (i, 0)), pl.BlockSpec((k, tn), lambda i, j: (0, j))]
    args = [x, w]
    if res is not None:
        in_specs.append(pl.BlockSpec((tm, tn), lambda i, j: (i, j)))
        args.append(res)
    outs = pl.pallas_call(
        functools.partial(_mm_body, n_out=len(out_dtypes), act=act, has_res=res is not None),
        grid=(m // tm, n // tn),
        in_specs=in_specs,
        out_specs=[pl.BlockSpec((tm, tn), lambda i, j: (i, j)) for _ in out_dtypes],
        out_shape=[jax.ShapeDtypeStruct((m, n), dt) for dt in out_dtypes],
        compiler_params=_cparams(("parallel", "parallel")),
        name=name,
    )(*args)
    return outs


def _mmk_body(x_ref, w_ref, res_ref, o_ref, acc_ref, *, nk):
    k = pl.program_id(2)

    @pl.when(k == 0)
    def _():
        acc_ref[...] = jnp.zeros_like(acc_ref)

    acc_ref[...] += jnp.dot(x_ref[...], w_ref[...], preferred_element_type=F32)

    @pl.when(k == nk - 1)
    def _():
        o_ref[...] = res_ref[...] + acc_ref[...]


def _matmul_ktiled(x, w, res, *, tm, tn, tk, name):
    m, k = x.shape
    n = w.shape[1]
    nk = k // tk
    return pl.pallas_call(
        functools.partial(_mmk_body, nk=nk),
        grid=(m // tm, n // tn, nk),
        in_specs=[pl.BlockSpec((tm, tk), lambda i, j, kk: (i, kk)),
                  pl.BlockSpec((tk, tn), lambda i, j, kk: (kk, j)),
                  pl.BlockSpec((tm, tn), lambda i, j, kk: (i, j))],
        out_specs=pl.BlockSpec((tm, tn), lambda i, j, kk: (i, j)),
        out_shape=jax.ShapeDtypeStruct((m, n), F32),
        scratch_shapes=[pltpu.VMEM((tm, tn), F32)],
        compiler_params=_cparams(("parallel", "parallel", "arbitrary")),
        name=name,
    )(x, w, res)


def _merge_body(xn_ref, po_ref, ao_ref, wgp_ref, wga_ref, wbp_ref, wba_ref, o_ref):
    xn = xn_ref[...]
    gp = jnp.dot(xn, wgp_ref[...], preferred_element_type=F32)
    ga = jnp.dot(xn, wga_ref[...], preferred_element_type=F32)
    a = jnp.dot(po_ref[...], wbp_ref[...], preferred_element_type=F32)
    b = jnp.dot(ao_ref[...], wba_ref[...], preferred_element_type=F32)
    o_ref[...] = (jax.nn.sigmoid(gp) * a + jax.nn.sigmoid(ga) * b).astype(o_ref.dtype)


def _merge(xn, po, ao, wgp, wga, wbp, wba, *, tm=512, tn=512):
    m, d = xn.shape
    kb = po.shape[1]
    n = wgp.shape[1]
    row = lambda i, j: (i, 0)
    col = lambda i, j: (0, j)
    return pl.pallas_call(
        _merge_body,
        grid=(m // tm, n // tn),
        in_specs=[pl.BlockSpec((tm, d), row), pl.BlockSpec((tm, kb), row), pl.BlockSpec((tm, kb), row),
                  pl.BlockSpec((d, tn), col), pl.BlockSpec((d, tn), col),
                  pl.BlockSpec((kb, tn), col), pl.BlockSpec((kb, tn), col)],
        out_specs=pl.BlockSpec((tm, tn), lambda i, j: (i, j)),
        out_shape=jax.ShapeDtypeStruct((m, n), BF16),
        compiler_params=_cparams(("parallel", "parallel")),
        name="merge",
    )(xn, po, ao, wgp, wga, wbp, wba)


def _pool_body(*refs, has_halo):
    if has_halo:
        halo_ref, cur_ref, s_ref, inv_ref, pw_ref, sc_ref, o_ref = refs
        halo = jnp.where(pl.program_id(1) == 0, 0.0, halo_ref[0])
        cur = cur_ref[0]
        ext = jnp.concatenate([halo, cur], axis=0)
    else:
        ext_ref, cur_ref, s_ref, inv_ref, pw_ref, sc_ref, o_ref = refs
        ext = ext_ref[...]
        cur = cur_ref[...]
    hi = ext.astype(BF16)
    lo = (ext - hi.astype(F32)).astype(BF16)
    band = s_ref[0]
    win = (jnp.dot(band, hi, preferred_element_type=F32) + jnp.dot(band, lo, preferred_element_type=F32))
    pooled = win * inv_ref[0] - cur
    y = jnp.dot(pooled.astype(BF16), pw_ref[0], preferred_element_type=F32) * sc_ref[...]
    if has_halo:
        o_ref[0] = y.astype(o_ref.dtype)
    else:
        o_ref[...] = y.astype(o_ref.dtype)


def _pool_prompt(u, pool_w, pool_scale, *, tb=512):
    n, t, _ = u.shape
    halo = PROMPT_HALO
    band = np.zeros((N_POOL_GROUPS, tb, halo + tb), np.float32)
    inv = np.zeros((N_POOL_GROUPS, t, 1), np.float32)
    r = np.arange(tb)[:, None]
    c = np.arange(halo + tb)[None, :]
    for g, w in enumerate(POOL_WINDOWS):
        band[g] = ((c >= r + halo - w + 1) & (c <= r + halo)).astype(np.float32)
        inv[g, :, 0] = 1.0 / np.minimum(w, np.arange(t) + 1)
    hb = tb // halo
    return pl.pallas_call(
        functools.partial(_pool_body, has_halo=True),
        grid=(n, t // tb, N_POOL_GROUPS),
        in_specs=[pl.BlockSpec((1, halo, POOL_GROUP), lambda b, i, g: (b, jnp.maximum(i * hb - 1, 0), g)),
                  pl.BlockSpec((1, tb, POOL_GROUP), lambda b, i, g: (b, i, g)),
                  pl.BlockSpec((1, tb, halo + tb), lambda b, i, g: (g, 0, 0)),
                  pl.BlockSpec((1, tb, 1), lambda b, i, g: (g, i, 0)),
                  pl.BlockSpec((1, POOL_GROUP, POOL_GROUP), lambda b, i, g: (g, 0, 0)),
                  pl.BlockSpec((1, POOL_GROUP), lambda b, i, g: (0, g))],
        out_specs=pl.BlockSpec((1, tb, POOL_GROUP), lambda b, i, g: (b, i, g)),
        out_shape=jax.ShapeDtypeStruct((n, t, POOL_WIDTH), BF16),
        compiler_params=_cparams(("parallel", "arbitrary", "arbitrary")),
        name="pool_prompt",
    )(u, u, jnp.asarray(band, BF16), jnp.asarray(inv), pool_w, pool_scale.reshape(1, POOL_WIDTH))


def _pool_sample(u, prefix, pool_w, pool_scale, *, sb=16):
    n, t, _ = u.shape
    e = HALO + t
    ext = jnp.concatenate([jnp.zeros((n, HALO - POOL_PREFIX, POOL_WIDTH), F32), prefix.astype(F32), u], axis=1)
    ext = ext.reshape(n * e, POOL_WIDTH)
    band = np.zeros((N_POOL_GROUPS, sb * t, sb * e), np.float32)
    inv = np.zeros((N_POOL_GROUPS, sb * t, 1), np.float32)
    r = np.arange(sb * t)[:, None]
    c = np.arange(sb * e)[None, :]
    for g, w in enumerate(POOL_WINDOWS):
        pos = HALO + r % t
        band[g] = ((r // t == c // e) & (c % e >= pos - w + 1) & (c % e <= pos)).astype(np.float32)
        inv[g] = 1.0 / w
    out = pl.pallas_call(
        functools.partial(_pool_body, has_halo=False),
        grid=(n // sb, N_POOL_GROUPS),
        in_specs=[pl.BlockSpec((sb * e, POOL_GROUP), lambda i, g: (i, g)),
                  pl.BlockSpec((sb * t, POOL_GROUP), lambda i, g: (i, g)),
                  pl.BlockSpec((1, sb * t, sb * e), lambda i, g: (g, 0, 0)),
                  pl.BlockSpec((1, sb * t, 1), lambda i, g: (g, 0, 0)),
                  pl.BlockSpec((1, POOL_GROUP, POOL_GROUP), lambda i, g: (g, 0, 0)),
                  pl.BlockSpec((1, POOL_GROUP), lambda i, g: (0, g))],
        out_specs=pl.BlockSpec((sb * t, POOL_GROUP), lambda i, g: (i, g)),
        out_shape=jax.ShapeDtypeStruct((n * t, POOL_WIDTH), BF16),
        compiler_params=_cparams(("parallel", "arbitrary")),
        name="pool_sample",
    )(ext, u.reshape(n * t, POOL_WIDTH), jnp.asarray(band, BF16), jnp.asarray(inv), pool_w,
      pool_scale.reshape(1, POOL_WIDTH))
    return out


def _bucket_thresholds():
    d = np.arange(0, 4 * MAX_DISTANCE)
    max_exact = NUM_BUCKETS // 2
    df = np.maximum(d, 1).astype(np.float32)
    large = max_exact + (np.log(df / np.float32(max_exact)) / np.float32(math.log(MAX_DISTANCE / max_exact))
                         * np.float32(NUM_BUCKETS - max_exact)).astype(np.int32)
    bucket = np.where(d < max_exact, d, np.minimum(large, NUM_BUCKETS - 1))
    assert np.all(np.diff(bucket) >= 0) and np.all(bucket[MAX_DISTANCE + 1:] == NUM_BUCKETS - 1)
    return [int(np.argmax(bucket >= b)) for b in range(NUM_BUCKETS)]


def _bias_prompt_body(rb_ref, o_ref):
    h = pl.program_id(0)
    thr = _bucket_thresholds()
    base = lax.broadcasted_iota(I32, (LANES, LANES), 0) - lax.broadcasted_iota(I32, (LANES, LANES), 1)
    for j in range(2):
        dist = base + j * LANES
        tile = jnp.full((LANES, LANES), rb_ref[0, h], F32)
        for b in range(1, NUM_BUCKETS):
            tile = jnp.where(dist >= thr[b], rb_ref[b, h], tile)
        o_ref[0, j] = (tile - rb_ref[NUM_BUCKETS - 1, h]) * LOG2E


def _bias_sample_body(rbx_ref, page_ref, new_ref, *, t):
    thr = _bucket_thresholds()
    for o_ref, rows, off in ((page_ref, PAGE_SIZE, PAGE_SIZE), (new_ref, t, 0)):
        qq = lax.broadcasted_iota(I32, (rows, LANES), 1) % t
        dist = off + qq - lax.broadcasted_iota(I32, (rows, LANES), 0)
        tile = jnp.broadcast_to(rbx_ref[0:1, :], (rows, LANES))
        for b in range(1, NUM_BUCKETS):
            tile = jnp.where(dist >= thr[b], rbx_ref[b:b + 1, :], tile)
        o_ref[...] = tile - rbx_ref[NUM_BUCKETS - 1:NUM_BUCKETS, :]


def _bias_tables(rel_bias, t):
    rb = rel_bias.astype(F32)
    tp = pl.pallas_call(
        _bias_prompt_body,
        grid=(N_HEADS,),
        in_specs=[pl.BlockSpec(memory_space=pltpu.SMEM)],
        out_specs=pl.BlockSpec((1, 2, LANES, LANES), lambda h: (h, 0, 0, 0)),
        out_shape=jax.ShapeDtypeStruct((N_HEADS, 2, LANES, LANES), F32),
        name="bias_prompt",
    )(rb)
    rbx = jnp.repeat(rb, LANES // N_HEADS, axis=1)
    ts_page, ts_new = pl.pallas_call(
        functools.partial(_bias_sample_body, t=t),
        out_shape=[jax.ShapeDtypeStruct((PAGE_SIZE, LANES), F32), jax.ShapeDtypeStruct((t, LANES), F32)],
        name="bias_sample",
    )(rbx)
    return tp, ts_page, ts_new


SEL_RB = 128


def _sortable(score):
    bits = lax.bitcast_convert_type(score, I32)
    return bits ^ ((bits >> 31) & 0x7FFFFFFF)


def _select_topk(key_ref, jcut_ref, krow, nblk, tq):
    def counts(pred):
        def body(r, acc):
            row0 = pl.multiple_of(r * SEL_RB, SEL_RB)
            hit = pred(key_ref[pl.ds(row0, SEL_RB), :], row0).astype(I32)
            return acc + jnp.sum(hit.reshape(SEL_RB // 8, 8, tq), axis=0)
        acc = lax.fori_loop(0, nblk, body, jnp.zeros((8, tq), I32))
        return jnp.sum(acc, axis=0, keepdims=True)

    def count_ge(cand):
        return counts(lambda blk, row0: blk >= cand)

    zero = jnp.zeros((1, tq), I32)
    v = jnp.where(count_ge(zero) >= krow, zero, jnp.full((1, tq), INT_MIN, I32))

    def bit_body(bi, v):
        cand = v | jnp.left_shift(jnp.int32(1), 30 - bi)
        return jnp.where(count_ge(cand) >= krow, cand, v)

    v = lax.fori_loop(0, 31, bit_body, v)

    n_gt = counts(lambda blk, row0: blk > v)
    n_ge = counts(lambda blk, row0: blk >= v)
    need = krow - n_gt
    jcut_ref[...] = jnp.full((8, tq), nblk * SEL_RB, I32)

    @pl.when(jnp.max(n_ge - krow) > 0)
    def _():
        def jbit(bi, x):
            cand = x + jnp.left_shift(jnp.int32(1), 15 - bi)

            def pred(blk, row0):
                row = row0 + lax.broadcasted_iota(I32, (SEL_RB, tq), 0)
                return (blk == v) & (row <= cand)
            return jnp.where(counts(pred) < need, cand, x)
        x = lax.fori_loop(0, 16, jbit, jnp.full((1, tq), -1, I32))
        jcut_ref[...] = jnp.broadcast_to(x + 2, (8, tq))

    return v


def _selected(key_ref, jcut_ref, v, row0, tq):
    blk = key_ref[pl.ds(row0, SEL_RB), :]
    row = row0 + lax.broadcasted_iota(I32, (SEL_RB, tq), 0)
    return (blk > v) | ((blk == v) & (row < jcut_ref[0:1, :]))


P1_TQ = 256
P1_CK = SEL_RB


def _p1_body(qi_ref, kbd_ref, wt_ref, amask_ref, key_ref, jcut_ref, *, s):
    i = pl.program_id(1)
    t0 = i * P1_TQ
    n_ch = (i + 1) * (P1_TQ // P1_CK)
    tq_iota = t0 + lax.broadcasted_iota(I32, (P1_CK, P1_TQ), 1)

    def chunk_body(c, _):
        kb = kbd_ref[0, pl.ds(pl.multiple_of(c * 2 * P1_CK, 2 * P1_CK), 2 * P1_CK), :]
        acc = jnp.zeros((P1_CK, P1_TQ), F32)
        for hp in range(IDX_HEADS // 2):
            d2 = lax.dot_general(kb, qi_ref[0, :, hp * LANES:(hp + 1) * LANES], (((1,), (1,)), ((), ())),
                                 preferred_element_type=F32)
            w0 = wt_ref[0, 2 * hp:2 * hp + 1, :] * IDX_W_SCALE
            w1 = wt_ref[0, 2 * hp + 1:2 * hp + 2, :] * IDX_W_SCALE
            acc = acc + w0 * jnp.maximum(d2[:P1_CK], 0.0) + w1 * jnp.maximum(d2[P1_CK:], 0.0)
        row0 = pl.multiple_of(c * P1_CK, P1_CK)
        srow = row0 + lax.broadcasted_iota(I32, (P1_CK, P1_TQ), 0)
        key_ref[pl.ds(row0, P1_CK), :] = jnp.where(srow <= tq_iota, _sortable(acc), INT_MIN)
        return 0

    lax.fori_loop(0, n_ch, chunk_body, 0)

    krow = jnp.minimum(TOPK_MAX, t0 + lax.broadcasted_iota(I32, (1, P1_TQ), 1) + 1)
    v = _select_topk(key_ref, jcut_ref, krow, n_ch, P1_TQ)

    def out_body(r, _):
        row0 = pl.multiple_of(r * SEL_RB, SEL_RB)
        am = jnp.where(_selected(key_ref, jcut_ref, v, row0, P1_TQ), 0.0, NEG)
        for h in range(P1_TQ // LANES):
            amask_ref[0, h * LANES:(h + 1) * LANES, pl.ds(row0, SEL_RB)] = jnp.transpose(
                am[:, h * LANES:(h + 1) * LANES])
        return 0

    lax.fori_loop(0, n_ch, out_body, 0)

    def fill_body(r, _):
        amask_ref[0, :, pl.ds(pl.multiple_of(r * SEL_RB, SEL_RB), SEL_RB)] = jnp.full((P1_TQ, SEL_RB), NEG, F32)
        return 0

    lax.fori_loop(n_ch, s // SEL_RB, fill_body, 0)


def _prompt_select(qi, kbd, wt):
    b, s, _ = qi.shape
    return pl.pallas_call(
        functools.partial(_p1_body, s=s),
        grid=(b, s // P1_TQ),
        in_specs=[pl.BlockSpec((1, P1_TQ, IDX_HEADS * IDX_DIM), lambda bb, i: (bb, i, 0)),
                  pl.BlockSpec((1, 2 * s, 2 * IDX_DIM), lambda bb, i: (bb, 0, 0)),
                  pl.BlockSpec((1, IDX_HEADS, P1_TQ), lambda bb, i: (bb, 0, i))],
        out_specs=pl.BlockSpec((1, P1_TQ, s), lambda bb, i: (bb, i, 0)),
        out_shape=jax.ShapeDtypeStruct((b, s, s), F32),
        scratch_shapes=[pltpu.VMEM((s, P1_TQ), I32), pltpu.VMEM((8, P1_TQ), I32)],
        compiler_params=_cparams(("parallel", "arbitrary")),
        name="prompt_select",
    )(qi, kbd, wt)


P2_TQ = 512
P2_CK = 1024


def _p2_body(q_ref, k_ref, v_ref, amask_ref, tb_ref, o_ref, lg_ref, m_ref, l_ref, acc_ref, *, s):
    i = pl.program_id(1)
    tq = P2_TQ
    lim = (i + 1) * tq
    q = q_ref[0]
    chunks = [(c * P2_CK, (c + 1) * P2_CK) for c in range(s // P2_CK)]

    def lane_fold(x, op):
        out = x[:, :LANES]
        for j in range(1, x.shape[1] // LANES):
            out = op(out, x[:, j * LANES:(j + 1) * LANES])
        return out

    for lo, hi in chunks:
        @pl.when(lo < lim)
        def _():
            sc = lax.dot_general(q, k_ref[0, lo:hi, :], (((1,), (1,)), ((), ())), preferred_element_type=F32)
            lg_ref[:, lo:hi] = sc * (ATTN_SCALE * LOG2E) + amask_ref[0, :, lo:hi]

    for a in range(tq // LANES):
        rows = slice(a * LANES, (a + 1) * LANES)
        dcol = pl.multiple_of(i * tq + a * LANES, LANES)
        lg_ref[rows, pl.ds(dcol, LANES)] += tb_ref[0, 0]
        if a > 0:
            scol = pl.multiple_of(i * tq + (a - 1) * LANES, LANES)
            lg_ref[rows, pl.ds(scol, LANES)] += tb_ref[0, 1]

    @pl.when(i > 0)
    def _():
        scol = pl.multiple_of(i * tq - LANES, LANES)
        lg_ref[0:LANES, pl.ds(scol, LANES)] += tb_ref[0, 1]

    m_ref[...] = jnp.full((tq, LANES), NEG, F32)
    for lo, hi in chunks:
        @pl.when(lo < lim)
        def _():
            m_ref[...] = jnp.maximum(m_ref[...], lane_fold(lg_ref[:, lo:hi], jnp.maximum))

    m = jnp.max(m_ref[...], axis=1, keepdims=True)
    l_ref[...] = jnp.zeros((tq, LANES), F32)
    acc_ref[...] = jnp.zeros((tq, HEAD_DIM), F32)
    for lo, hi in chunks:
        @pl.when(lo < lim)
        def _():
            p = jnp.exp2(lg_ref[:, lo:hi] - m)
            l_ref[...] += lane_fold(p, jnp.add)
            acc_ref[...] += jnp.dot(p.astype(BF16), v_ref[0, lo:hi, :], preferred_element_type=F32)

    o_ref[0] = (acc_ref[...] / jnp.sum(l_ref[...], axis=1, keepdims=True)).astype(o_ref.dtype)


def _prompt_attention(q, k, v, amask, tb):
    b, s, _ = q.shape
    return pl.pallas_call(
        functools.partial(_p2_body, s=s),
        grid=(b, s // P2_TQ, N_HEADS),
        in_specs=[pl.BlockSpec((1, P2_TQ, HEAD_DIM), lambda bb, i, h: (bb, i, h)),
                  pl.BlockSpec((1, s, HEAD_DIM), lambda bb, i, h: (bb, 0, h)),
                  pl.BlockSpec((1, s, HEAD_DIM), lambda bb, i, h: (bb, 0, h)),
                  pl.BlockSpec((1, P2_TQ, s), lambda bb, i, h: (bb, i, 0)),
                  pl.BlockSpec((1, 2, LANES, LANES), lambda bb, i, h: (h, 0, 0, 0))],
        out_specs=pl.BlockSpec((1, P2_TQ, HEAD_DIM), lambda bb, i, h: (bb, i, h)),
        out_shape=jax.ShapeDtypeStruct((b, s, ATTN_WIDTH), BF16),
        scratch_shapes=[pltpu.VMEM((P2_TQ, s), F32), pltpu.VMEM((P2_TQ, LANES), F32),
                        pltpu.VMEM((P2_TQ, LANES), F32), pltpu.VMEM((P2_TQ, HEAD_DIM), F32)],
        compiler_params=_cparams(("parallel", "parallel", "arbitrary")),
        name="prompt_attention",
    )(q, k, v, amask, tb)


def _s1_body(pt_ref, qi_ref, w_ref, *refs, n_pages, t):
    page_refs = refs[:n_pages + 1]
    o_ref = refs[n_pages + 1]
    qi = qi_ref[0]
    w = jnp.broadcast_to(w_ref[0] * IDX_W_SCALE, (IDX_HEADS * t, LANES))
    for p, kp_ref in enumerate(page_refs):
        kp = kp_ref[0].astype(BF16)
        d = lax.dot_general(qi, kp, (((1,), (1,)), ((), ())), preferred_element_type=F32)
        wr = (w * jnp.maximum(d, 0.0)).reshape(IDX_HEADS, t, LANES)
        o_ref[0, :, p * LANES:(p + 1) * LANES] = jnp.sum(wr, axis=0)


def _sample_scores(qi_hq, wi_hq, cache_kidx, ki_new_pad, page_table):
    n, rows, _ = qi_hq.shape
    t = rows // IDX_HEADS
    n_pages = page_table.shape[1]
    lp = (n_pages + 1) * PAGE_SIZE

    def page_spec(p):
        return pl.BlockSpec((1, PAGE_SIZE, IDX_DIM), lambda b, pt: (pt[b, p], 0, 0))

    grid_spec = pltpu.PrefetchScalarGridSpec(
        num_scalar_prefetch=1,
        grid=(n,),
        in_specs=[pl.BlockSpec((1, rows, IDX_DIM), lambda b, pt: (b, 0, 0)),
                  pl.BlockSpec((1, rows, 1), lambda b, pt: (b, 0, 0))]
                 + [page_spec(p) for p in range(n_pages)]
                 + [pl.BlockSpec((1, PAGE_SIZE, IDX_DIM), lambda b, pt: (b, 0, 0))],
        out_specs=pl.BlockSpec((1, t, lp), lambda b, pt: (b, 0, 0)),
    )
    return pl.pallas_call(
        functools.partial(_s1_body, n_pages=n_pages, t=t),
        grid_spec=grid_spec,
        out_shape=jax.ShapeDtypeStruct((n, t, lp), F32),
        compiler_params=_cparams(("arbitrary",)),
        name="sample_scores",
    )(page_table, qi_hq, wi_hq, *([cache_kidx] * n_pages), ki_new_pad)


S2_TQ = 256


def _s2_body(sc_ref, mask_ref, key_ref, jcut_ref, *, t, lp):
    c0 = pl.program_id(0) * S2_TQ
    nblk = lp // SEL_RB
    lim = PAST_LEN + (c0 + lax.broadcasted_iota(I32, (SEL_RB, S2_TQ), 1)) % t + 1
    for r in range(nblk):
        rows = slice(r * SEL_RB, (r + 1) * SEL_RB)
        srow = r * SEL_RB + lax.broadcasted_iota(I32, (SEL_RB, S2_TQ), 0)
        key_ref[rows, :] = jnp.where(srow < lim, _sortable(sc_ref[rows, :]), INT_MIN)
    krow = jnp.full((1, S2_TQ), min(TOPK_MAX, (PAST_LEN + t) // 4), I32)
    v = _select_topk(key_ref, jcut_ref, krow, nblk, S2_TQ)
    for r in range(nblk):
        sel = _selected(key_ref, jcut_ref, v, r * SEL_RB, S2_TQ)
        mask_ref[r * SEL_RB:(r + 1) * SEL_RB, :] = jnp.where(sel, 1.0, 0.0).astype(BF16)


def _sample_select(scores_t, t):
    lp, r = scores_t.shape
    return pl.pallas_call(
        functools.partial(_s2_body, t=t, lp=lp),
        grid=(r // S2_TQ,),
        in_specs=[pl.BlockSpec((lp, S2_TQ), lambda i: (0, i))],
        out_specs=pl.BlockSpec((lp, S2_TQ), lambda i: (0, i)),
        out_shape=jax.ShapeDtypeStruct((lp, r), BF16),
        scratch_shapes=[pltpu.VMEM((lp, S2_TQ), I32), pltpu.VMEM((8, S2_TQ), I32)],
        compiler_params=_cparams(("parallel",)),
        name="sample_select",
    )(scores_t)


S3_GROUP = 4


def _page_rows_by_head(ref):
    return jnp.concatenate(
        [ref[0, pl.ds(h, PAGE_SIZE, stride=N_HEADS), :].astype(BF16) for h in range(N_HEADS)], axis=1)


def _s3_body(pt_ref, qbd_ref, mask_ref, *refs, n_pages, t):
    ng = n_pages // S3_GROUP
    k_refs = refs[:S3_GROUP]
    v_refs = refs[S3_GROUP:2 * S3_GROUP]
    knew_ref, vnew_ref, tsp_ref, tsn_ref, o_ref, lg_ref, acc_ref, vpad_ref = refs[2 * S3_GROUP:]
    g = pl.program_id(1)
    past = n_pages * PAGE_SIZE

    def masked_logits(k16, msk, bias):
        lg = jnp.dot(k16, qbd_ref[0], preferred_element_type=F32) * ATTN_SCALE
        if bias is not None:
            lg = lg + bias
        return jnp.where(msk > 0, lg, NEG)

    @pl.when(g < ng)
    def _():
        for j in range(S3_GROUP):
            row0 = pl.multiple_of((g * S3_GROUP + j) * PAGE_SIZE, PAGE_SIZE)
            bias = jnp.where(g == ng - 1, tsp_ref[...], 0.0) if j == S3_GROUP - 1 else None
            lg_ref[pl.ds(row0, PAGE_SIZE), :] = masked_logits(_page_rows_by_head(k_refs[j]),
                                                              mask_ref[0, pl.ds(row0, PAGE_SIZE), :], bias)

    @pl.when(g == ng - 1)
    def _():
        msk_new = mask_ref[0, past:past + 2 * t, :].astype(F32)[:t]
        lg_ref[past:past + t, :] = masked_logits(knew_ref[0].astype(BF16), msk_new, tsn_ref[...])
        lg_ref[past + t:, :] = jnp.full((PAGE_SIZE - t, LANES), NEG, F32)

    @pl.when(g == ng)
    def _():
        lg = lg_ref[...]
        p = jnp.exp(lg - jnp.max(lg, axis=0, keepdims=True))
        lg_ref[...] = p / jnp.sum(p, axis=0, keepdims=True)
        acc_ref[...] = jnp.zeros_like(acc_ref)

    def add_pv(row0, v16):
        pt = jnp.transpose(lg_ref[pl.ds(row0, PAGE_SIZE), :]).astype(BF16)
        acc_ref[...] += jnp.dot(pt, v16, preferred_element_type=F32)

    @pl.when(g >= ng)
    def _():
        for j in range(S3_GROUP):
            add_pv(pl.multiple_of(((g - ng) * S3_GROUP + j) * PAGE_SIZE, PAGE_SIZE), _page_rows_by_head(v_refs[j]))

    @pl.when(g == 2 * ng - 1)
    def _():
        vpad_ref[...] = jnp.zeros_like(vpad_ref)
        vpad_ref[0:t, :] = vnew_ref[0]
        add_pv(past, vpad_ref[...].astype(BF16))
        for h in range(N_HEADS):
            o_ref[0, :, h * HEAD_DIM:(h + 1) * HEAD_DIM] = acc_ref[h * t:(h + 1) * t, h * HEAD_DIM:(h + 1) * HEAD_DIM]


def _sample_attention(qbd, mask_t, cache_k, cache_v, k_new, v_new, ts_page, ts_new, page_table):
    n, t, _ = k_new.shape
    n_pages = page_table.shape[1]
    ng = n_pages // S3_GROUP
    lp = mask_t.shape[1]
    page_block = (1, PAGE_SIZE * N_HEADS, HEAD_DIM)

    def k_spec(j):
        return pl.BlockSpec(page_block, lambda b, g, pt: (pt[b, jnp.minimum(g, ng - 1) * S3_GROUP + j], 0, 0))

    def v_spec(j):
        return pl.BlockSpec(page_block, lambda b, g, pt: (pt[b, jnp.maximum(g - ng, 0) * S3_GROUP + j], 0, 0))

    per_seq = lambda b, g, pt: (b, 0, 0)
    const2 = lambda b, g, pt: (0, 0)
    grid_spec = pltpu.PrefetchScalarGridSpec(
        num_scalar_prefetch=1,
        grid=(n, 2 * ng),
        in_specs=[pl.BlockSpec((1, ATTN_WIDTH, LANES), per_seq), pl.BlockSpec((1, lp, LANES), per_seq)]
                 + [k_spec(j) for j in range(S3_GROUP)] + [v_spec(j) for j in range(S3_GROUP)]
                 + [pl.BlockSpec((1, t, ATTN_WIDTH), per_seq), pl.BlockSpec((1, t, ATTN_WIDTH), per_seq),
                    pl.BlockSpec((PAGE_SIZE, LANES), const2), pl.BlockSpec((t, LANES), const2)],
        out_specs=pl.BlockSpec((1, t, ATTN_WIDTH), per_seq),
        scratch_shapes=[pltpu.VMEM((lp, LANES), F32), pltpu.VMEM((LANES, ATTN_WIDTH), F32),
                        pltpu.VMEM((PAGE_SIZE, ATTN_WIDTH), F32)],
    )
    return pl.pallas_call(
        functools.partial(_s3_body, n_pages=n_pages, t=t),
        grid_spec=grid_spec,
        out_shape=jax.ShapeDtypeStruct((n, t, ATTN_WIDTH), F32),
        compiler_params=_cparams(("arbitrary", "arbitrary")),
        name="sample_attention",
    )(page_table, qbd, mask_t, *([cache_k] * S3_GROUP), *([cache_v] * S3_GROUP), k_new, v_new, ts_page, ts_new)


def _split_w_in(w):
    o = [0]
    for s in (POOL_WIDTH, ATTN_WIDTH, ATTN_WIDTH, ATTN_WIDTH, IDX_HEADS * IDX_DIM, IDX_DIM, IDX_HEADS, D_MODEL,
              D_MODEL):
        o.append(o[-1] + s)
    cut = lambda a, b: w[:, o[a]:o[b]].astype(BF16)
    kiwi = jnp.pad(cut(5, 7), ((0, 0), (0, LANES - IDX_DIM - IDX_HEADS)))
    return dict(u=cut(0, 1), q=cut(1, 2), k=cut(2, 3), v=cut(3, 4), qi=cut(4, 5), kiwi=kiwi, gp=cut(7, 8),
                ga=cut(8, 9))


def _project(x2d, ln1, wi):
    m = x2d.shape[0]
    tm = min(m, 1024)
    xn = _rmsnorm(x2d, ln1, BF16)
    (u,) = _matmul(xn, wi["u"], [F32], tm=tm, tn=1024, name="proj_u")
    (q,) = _matmul(xn, wi["q"], [BF16], tm=tm, tn=1024, name="proj_q")
    k, k16 = _matmul(xn, wi["k"], [F32, BF16], tm=tm, tn=1024, name="proj_k")
    v, v16 = _matmul(xn, wi["v"], [F32, BF16], tm=tm, tn=1024, name="proj_v")
    (qi,) = _matmul(xn, wi["qi"], [BF16], tm=tm, tn=1024, name="proj_qi")
    (kiwi,) = _matmul(xn, wi["kiwi"], [F32], tm=tm, tn=LANES, name="proj_kiwi")
    ki = kiwi[:, :IDX_DIM]
    wgt = kiwi[:, IDX_DIM:IDX_DIM + IDX_HEADS]
    return xn, u, q, k, k16, v, v16, qi, ki, wgt


def _finish(x2d, xn, pool_o, attn_o, wi, w_bp, w_ba, w_out, ln2, w1, w2, ln_f):
    m = x2d.shape[0]
    tm = min(m, 1024)
    mg = _merge(xn, pool_o, attn_o, wi["gp"], wi["ga"], w_bp, w_ba)
    (x1,) = _matmul(mg, w_out, [F32], tm=tm, tn=1024, res=x2d, name="out_proj")
    hn = _rmsnorm(x1, ln2, BF16)
    (h,) = _matmul(hn, w1, [BF16], tm=tm, tn=1024, act="relu2", name="mlp_in")
    x2 = _matmul_ktiled(h, w2, x1, tm=tm, tn=1024, tk=2048, name="mlp_out")
    return _rmsnorm(x2, ln_f, F32)


def kernel(x_prompt, x_sample, cache_k, cache_v, cache_kidx, state_pool, page_table, ln1, w_in, pool_w,
           pool_scale, rel_bias, w_branch_pool, w_branch_attn, w_out, ln2, w_mlp_in, w_mlp_out, ln_f):
    bn, s, _ = x_prompt.shape
    n, t, _ = x_sample.shape
    depth = w_in.shape[0]
    assert depth == 1, "one layer: the caches and pooling state of layer 0 are the only ones read"
    n_pages = page_table.shape[1]
    n_phys = cache_k.shape[1]

    wi = _split_w_in(w_in[0])
    pw = pool_w[0].astype(BF16)
    w_bp = w_branch_pool[0].astype(BF16)
    w_ba = w_branch_attn[0].astype(BF16)
    wo = w_out[0].astype(BF16)
    w1 = w_mlp_in[0].astype(BF16)
    w2 = w_mlp_out[0].astype(BF16)
    tb_prompt, ts_page, ts_new = _bias_tables(rel_bias, t)

    xp = x_prompt.reshape(bn * s, D_MODEL)
    xn, u, q, k, k16, v, v16, qi, ki, wgt = _project(xp, ln1[0], wi)
    pool_o = _pool_prompt(u.reshape(bn, s, POOL_WIDTH), pw, pool_scale[0]).reshape(bn * s, POOL_WIDTH)
    ki_c = ki.astype(BF16).reshape(bn, s // P1_CK, P1_CK, IDX_DIM)
    zeros = jnp.zeros_like(ki_c)
    kbd = jnp.concatenate([jnp.concatenate([ki_c, zeros], axis=-1), jnp.concatenate([zeros, ki_c], axis=-1)],
                          axis=2).reshape(bn, 2 * s, 2 * IDX_DIM)
    amask = _prompt_select(qi.reshape(bn, s, IDX_HEADS * IDX_DIM), kbd,
                           jnp.swapaxes(wgt.reshape(bn, s, IDX_HEADS), 1, 2))
    attn_o = _prompt_attention(q.reshape(bn, s, ATTN_WIDTH), k16.reshape(bn, s, ATTN_WIDTH),
                               v16.reshape(bn, s, ATTN_WIDTH), amask, tb_prompt).reshape(bn * s, ATTN_WIDTH)
    y_prompt = _finish(xp, xn, pool_o, attn_o, wi, w_bp, w_ba, wo, ln2[0], w1, w2, ln_f).reshape(bn, s, D_MODEL)
    k_prompt = k.reshape(1, bn, s, N_HEADS, HEAD_DIM)
    v_prompt = v.reshape(1, bn, s, N_HEADS, HEAD_DIM)
    kidx_prompt = ki.reshape(1, bn, s, IDX_DIM)
    pool_prompt = u.reshape(bn, s, POOL_WIDTH)[:, s - POOL_PREFIX:, :][None]

    xs = x_sample.reshape(n * t, D_MODEL)
    xn, u, q, k, k16, v, v16, qi, ki, wgt = _project(xs, ln1[0], wi)
    u3 = u.reshape(n, t, POOL_WIDTH)
    pool_o = _pool_sample(u3, state_pool[0], pw, pool_scale[0])
    qi_hq = qi.reshape(n, t, IDX_HEADS, IDX_DIM).transpose(0, 2, 1, 3).reshape(n, IDX_HEADS * t, IDX_DIM)
    wi_hq = wgt.reshape(n, t, IDX_HEADS).transpose(0, 2, 1).reshape(n, IDX_HEADS * t, 1)
    ki_new_pad = jnp.pad(ki.reshape(n, t, IDX_DIM), ((0, 0), (0, PAGE_SIZE - t), (0, 0)))
    scores = _sample_scores(qi_hq, wi_hq, cache_kidx[0], ki_new_pad, page_table)
    lp = scores.shape[-1]
    smask_t = _sample_select(scores.reshape(n * t, lp).T, t)
    mask_t = jnp.tile(jnp.swapaxes(smask_t.reshape(lp, n, t), 0, 1), (1, 1, N_HEADS))
    q_t = q.reshape(n, t, N_HEADS, HEAD_DIM).transpose(0, 2, 3, 1)
    eye = jnp.eye(N_HEADS, dtype=BF16)
    qbd = (q_t[:, :, :, None, :] * eye[None, :, None, :, None]).reshape(n, ATTN_WIDTH, N_HEADS * t)
    attn_o = _sample_attention(qbd, mask_t, cache_k[0].reshape(n_phys, PAGE_SIZE * N_HEADS, HEAD_DIM),
                               cache_v[0].reshape(n_phys, PAGE_SIZE * N_HEADS, HEAD_DIM), k.reshape(n, t, ATTN_WIDTH),
                               v.reshape(n, t, ATTN_WIDTH), ts_page, ts_new, page_table)
    attn_o = attn_o.reshape(n * t, ATTN_WIDTH).astype(BF16)
    y_sample = _finish(xs, xn, pool_o, attn_o, wi, w_bp, w_ba, wo, ln2[0], w1, w2, ln_f).reshape(n, t, D_MODEL)
    k_sample = k.reshape(1, n, t, N_HEADS, HEAD_DIM)
    v_sample = v.reshape(1, n, t, N_HEADS, HEAD_DIM)
    kidx_sample = ki.reshape(1, n, t, IDX_DIM)
    pool_sample = jnp.concatenate([state_pool[0][:, t:, :], u3], axis=1)[None]

    return (y_prompt, y_sample, k_prompt, v_prompt, kidx_prompt, pool_prompt,
            k_sample, v_sample, kidx_sample, pool_sample)
```

```python
import functools
import math

import numpy as np
import jax
import jax.numpy as jnp
from jax import lax
from jax.experimental import pallas as pl
from jax.experimental.pallas import tpu as pltpu

D_MODEL = 4096
PAST_LEN = 2048
PAGE_SIZE = 128
POOL_WIDTH = D_MODEL // 2
POOL_WINDOWS = (2, 4, 8, 16)
N_POOL_GROUPS = len(POOL_WINDOWS)
POOL_GROUP = POOL_WIDTH // N_POOL_GROUPS
POOL_PREFIX = max(POOL_WINDOWS) - 1
HEAD_DIM = 128
N_HEADS = (D_MODEL // 2) // HEAD_DIM
ATTN_WIDTH = N_HEADS * HEAD_DIM
IDX_HEADS = 32
IDX_DIM = 64
TOPK_MAX = 256
NUM_BUCKETS = 32
MAX_DISTANCE = 128
D_FF = 4 * D_MODEL
EPS = 1e-6

F32 = jnp.float32
BF16 = jnp.bfloat16
I32 = jnp.int32

LANES = 128
VMEM_LIMIT = 56 * 1024 * 1024
INT_MIN = -(2 ** 31)
NEG = -1e30
HALO = 16
PROMPT_HALO = 128
IDX_W_SCALE = (IDX_HEADS ** -0.5) * (IDX_DIM ** -0.5)
ATTN_SCALE = HEAD_DIM ** -0.5
LOG2E = math.log2(math.e)


def _cparams(sem):
    return pltpu.CompilerParams(dimension_semantics=sem, vmem_limit_bytes=VMEM_LIMIT)


def _rmsnorm_body(x_ref, g_ref, o_ref):
    x = x_ref[...]
    y = x * lax.rsqrt(jnp.mean(x * x, axis=-1, keepdims=True) + EPS)
    o_ref[...] = (y * g_ref[...]).astype(o_ref.dtype)


def _rmsnorm(x, g, out_dtype, tm=512):
    m, d = x.shape
    return pl.pallas_call(
        _rmsnorm_body,
        grid=(m // tm,),
        in_specs=[pl.BlockSpec((tm, d), lambda i: (i, 0)), pl.BlockSpec((1, d), lambda i: (0, 0))],
        out_specs=pl.BlockSpec((tm, d), lambda i: (i, 0)),
        out_shape=jax.ShapeDtypeStruct((m, d), out_dtype),
        compiler_params=_cparams(("parallel",)),
        name="rmsnorm",
    )(x, g.reshape(1, d))


def _mm_body(*refs, n_out, act, has_res):
    x_ref, w_ref = refs[0], refs[1]
    res_ref = refs[2] if has_res else None
    outs = refs[2 + int(has_res):2 + int(has_res) + n_out]
    acc = jnp.dot(x_ref[...], w_ref[...], preferred_element_type=F32)
    if act == "relu2":
        acc = jnp.square(jnp.maximum(acc, 0.0))
    if has_res:
        acc = res_ref[...] + acc
    for o in outs:
        o[...] = acc.astype(o.dtype)


def _matmul(x, w, out_dtypes, *, tm, tn, act=None, res=None, name="matmul"):
    m, k = x.shape
    n = w.shape[1]
    in_specs = [pl.BlockSpec((tm, k), lambda i, j: (i, 0)), pl.BlockSpec((k, tn), lambda i, j: (0, j))]
    args = [x, w]
    if res is not None:
        in_specs.append(pl.BlockSpec((tm, tn), lambda i, j: (i, j)))
        args.append(res)
    outs = pl.pallas_call(
        functools.partial(_mm_body, n_out=len(out_dtypes), act=act, has_res=res is not None),
        grid=(m // tm, n // tn),
        in_specs=in_specs,
        out_specs=[pl.BlockSpec((tm, tn), lambda i, j: (i, j)) for _ in out_dtypes],
        out_shape=[jax.ShapeDtypeStruct((m, n), dt) for dt in out_dtypes],
        compiler_params=_cparams(("parallel", "parallel")),
        name=name,
    )(*args)
    return outs


def _mmh_body(x_ref, w_ref, o32_ref, o16_ref, *, heads):
    j = pl.program_id(1)
    acc = jnp.dot(x_ref[...], w_ref[...], preferred_element_type=F32)
    o16_ref[...] = acc.astype(BF16)
    tm = acc.shape[0]
    for hh in range(heads):
        o32_ref[pl.ds(j * heads + hh, tm, stride=N_HEADS), :] = acc[:, hh * HEAD_DIM:(hh + 1) * HEAD_DIM]


def _matmul_heads(x, w, *, tm, tn, name):
    m, k = x.shape
    n = w.shape[1]
    assert n == N_HEADS * HEAD_DIM
    return pl.pallas_call(
        functools.partial(_mmh_body, heads=tn // HEAD_DIM),
        grid=(m // tm, n // tn),
        in_specs=[pl.BlockSpec((tm, k), lambda i, j: (i, 0)), pl.BlockSpec((k, tn), lambda i, j: (0, j))],
        out_specs=[pl.BlockSpec((tm * N_HEADS, HEAD_DIM), lambda i, j: (i, 0)),
                   pl.BlockSpec((tm, tn), lambda i, j: (i, j))],
        out_shape=[jax.ShapeDtypeStruct((m * N_HEADS, HEAD_DIM), F32), jax.ShapeDtypeStruct((m, n), BF16)],
        compiler_params=_cparams(("parallel", "arbitrary")),
        name=name,
    )(x, w)


def _mmk_body(x_ref, w_ref, res_ref, o_ref, acc_ref, *, nk):
    k = pl.program_id(2)

    @pl.when(k == 0)
    def _():
        acc_ref[...] = jnp.zeros_like(acc_ref)

    acc_ref[...] += jnp.dot(x_ref[...], w_ref[...], preferred_element_type=F32)

    @pl.when(k == nk - 1)
    def _():
        o_ref[...] = res_ref[...] + acc_ref[...]


def _matmul_ktiled(x, w, res, *, tm, tn, tk, name):
    m, k = x.shape
    n = w.shape[1]
    nk = k // tk
    return pl.pallas_call(
        functools.partial(_mmk_body, nk=nk),
        grid=(m // tm, n // tn, nk),
        in_specs=[pl.BlockSpec((tm, tk), lambda i, j, kk: (i, kk)),
                  pl.BlockSpec((tk, tn), lambda i, j, kk: (kk, j)),
                  pl.BlockSpec((tm, tn), lambda i, j, kk: (i, j))],
        out_specs=pl.BlockSpec((tm, tn), lambda i, j, kk: (i, j)),
        out_shape=jax.ShapeDtypeStruct((m, n), F32),
        scratch_shapes=[pltpu.VMEM((tm, tn), F32)],
        compiler_params=_cparams(("parallel", "parallel", "arbitrary")),
        name=name,
    )(x, w, res)


def _merge_body(xn_ref, po_ref, ao_ref, wgp_ref, wga_ref, wbp_ref, wba_ref, o_ref):
    xn = xn_ref[...]
    gp = jnp.dot(xn, wgp_ref[...], preferred_element_type=F32)
    ga = jnp.dot(xn, wga_ref[...], preferred_element_type=F32)
    a = jnp.dot(po_ref[...], wbp_ref[...], preferred_element_type=F32)
    b = jnp.dot(ao_ref[...], wba_ref[...], preferred_element_type=F32)
    o_ref[...] = (jax.nn.sigmoid(gp) * a + jax.nn.sigmoid(ga) * b).astype(o_ref.dtype)


def _merge(xn, po, ao, wgp, wga, wbp, wba, *, tm=512, tn=512):
    m, d = xn.shape
    kb = po.shape[1]
    n = wgp.shape[1]
    row = lambda i, j: (i, 0)
    col = lambda i, j: (0, j)
    return pl.pallas_call(
        _merge_body,
        grid=(m // tm, n // tn),
        in_specs=[pl.BlockSpec((tm, d), row), pl.BlockSpec((tm, kb), row), pl.BlockSpec((tm, kb), row),
                  pl.BlockSpec((d, tn), col), pl.BlockSpec((d, tn), col),
                  pl.BlockSpec((kb, tn), col), pl.BlockSpec((kb, tn), col)],
        out_specs=pl.BlockSpec((tm, tn), lambda i, j: (i, j)),
        out_shape=jax.ShapeDtypeStruct((m, n), BF16),
        compiler_params=_cparams(("parallel", "parallel")),
        name="merge",
    )(xn, po, ao, wgp, wga, wbp, wba)


def _pool_body(*refs, has_halo):
    if has_halo:
        halo_ref, cur_ref, s_ref, inv_ref, pw_ref, sc_ref, o_ref = refs
        halo = jnp.where(pl.program_id(1) == 0, 0.0, halo_ref[0])
        cur = cur_ref[0]
        ext = jnp.concatenate([halo, cur], axis=0)
    else:
        ext_ref, cur_ref, s_ref, inv_ref, pw_ref, sc_ref, o_ref = refs
        ext = ext_ref[...]
        cur = cur_ref[...]
    hi = ext.astype(BF16)
    lo = (ext - hi.astype(F32)).astype(BF16)
    band = s_ref[0]
    win = (jnp.dot(band, hi, preferred_element_type=F32) + jnp.dot(band, lo, preferred_element_type=F32))
    pooled = win * inv_ref[0] - cur
    y = jnp.dot(pooled.astype(BF16), pw_ref[0], preferred_element_type=F32) * sc_ref[...]
    if has_halo:
        o_ref[0] = y.astype(o_ref.dtype)
    else:
        o_ref[...] = y.astype(o_ref.dtype)


def _pool_prompt(u, pool_w, pool_scale, *, tb=512):
    n, t, _ = u.shape
    halo = PROMPT_HALO
    band = np.zeros((N_POOL_GROUPS, tb, halo + tb), np.float32)
    inv = np.zeros((N_POOL_GROUPS, t, 1), np.float32)
    r = np.arange(tb)[:, None]
    c = np.arange(halo + tb)[None, :]
    for g, w in enumerate(POOL_WINDOWS):
        band[g] = ((c >= r + halo - w + 1) & (c <= r + halo)).astype(np.float32)
        inv[g, :, 0] = 1.0 / np.minimum(w, np.arange(t) + 1)
    hb = tb // halo
    return pl.pallas_call(
        functools.partial(_pool_body, has_halo=True),
        grid=(n, t // tb, N_POOL_GROUPS),
        in_specs=[pl.BlockSpec((1, halo, POOL_GROUP), lambda b, i, g: (b, jnp.maximum(i * hb - 1, 0), g)),
                  pl.BlockSpec((1, tb, POOL_GROUP), lambda b, i, g: (b, i, g)),
                  pl.BlockSpec((1, tb, halo + tb), lambda b, i, g: (g, 0, 0)),
                  pl.BlockSpec((1, tb, 1), lambda b, i, g: (g, i, 0)),
                  pl.BlockSpec((1, POOL_GROUP, POOL_GROUP), lambda b, i, g: (g, 0, 0)),
                  pl.BlockSpec((1, POOL_GROUP), lambda b, i, g: (0, g))],
        out_specs=pl.BlockSpec((1, tb, POOL_GROUP), lambda b, i, g: (b, i, g)),
        out_shape=jax.ShapeDtypeStruct((n, t, POOL_WIDTH), BF16),
        compiler_params=_cparams(("parallel", "arbitrary", "arbitrary")),
        name="pool_prompt",
    )(u, u, jnp.asarray(band, BF16), jnp.asarray(inv), pool_w, pool_scale.reshape(1, POOL_WIDTH))


def _pool_sample(u, prefix, pool_w, pool_scale, *, sb=16):
    n, t, _ = u.shape
    e = HALO + t
    ext = jnp.concatenate([jnp.zeros((n, HALO - POOL_PREFIX, POOL_WIDTH), F32), prefix.astype(F32), u], axis=1)
    ext = ext.reshape(n * e, POOL_WIDTH)
    band = np.zeros((N_POOL_GROUPS, sb * t, sb * e), np.float32)
    inv = np.zeros((N_POOL_GROUPS, sb * t, 1), np.float32)
    r = np.arange(sb * t)[:, None]
    c = np.arange(sb * e)[None, :]
    for g, w in enumerate(POOL_WINDOWS):
        pos = HALO + r % t
        band[g] = ((r // t == c // e) & (c % e >= pos - w + 1) & (c % e <= pos)).astype(np.float32)
        inv[g] = 1.0 / w
    out = pl.pallas_call(
        functools.partial(_pool_body, has_halo=False),
        grid=(n // sb, N_POOL_GROUPS),
        in_specs=[pl.BlockSpec((sb * e, POOL_GROUP), lambda i, g: (i, g)),
                  pl.BlockSpec((sb * t, POOL_GROUP), lambda i, g: (i, g)),
                  pl.BlockSpec((1, sb * t, sb * e), lambda i, g: (g, 0, 0)),
                  pl.BlockSpec((1, sb * t, 1), lambda i, g: (g, 0, 0)),
                  pl.BlockSpec((1, POOL_GROUP, POOL_GROUP), lambda i, g: (g, 0, 0)),
                  pl.BlockSpec((1, POOL_GROUP), lambda i, g: (0, g))],
        out_specs=pl.BlockSpec((sb * t, POOL_GROUP), lambda i, g: (i, g)),
        out_shape=jax.ShapeDtypeStruct((n * t, POOL_WIDTH), BF16),
        compiler_params=_cparams(("parallel", "arbitrary")),
        name="pool_sample",
    )(ext, u.reshape(n * t, POOL_WIDTH), jnp.asarray(band, BF16), jnp.asarray(inv), pool_w,
      pool_scale.reshape(1, POOL_WIDTH))
    return out


def _bucket_thresholds():
    d = np.arange(0, 4 * MAX_DISTANCE)
    max_exact = NUM_BUCKETS // 2
    df = np.maximum(d, 1).astype(np.float32)
    large = max_exact + (np.log(df / np.float32(max_exact)) / np.float32(math.log(MAX_DISTANCE / max_exact))
                         * np.float32(NUM_BUCKETS - max_exact)).astype(np.int32)
    bucket = np.where(d < max_exact, d, np.minimum(large, NUM_BUCKETS - 1))
    assert np.all(np.diff(bucket) >= 0) and np.all(bucket[MAX_DISTANCE + 1:] == NUM_BUCKETS - 1)
    return [int(np.argmax(bucket >= b)) for b in range(NUM_BUCKETS)]


def _bias_prompt_body(rb_ref, o_ref):
    h = pl.program_id(0)
    thr = _bucket_thresholds()
    base = lax.broadcasted_iota(I32, (LANES, LANES), 0) - lax.broadcasted_iota(I32, (LANES, LANES), 1)
    for j in range(2):
        dist = base + j * LANES
        tile = jnp.full((LANES, LANES), rb_ref[0, h], F32)
        for b in range(1, NUM_BUCKETS):
            tile = jnp.where(dist >= thr[b], rb_ref[b, h], tile)
        o_ref[0, j] = (tile - rb_ref[NUM_BUCKETS - 1, h]) * LOG2E


def _bias_sample_body(rbx_ref, page_ref, new_ref, *, t):
    thr = _bucket_thresholds()
    for o_ref, rows, off in ((page_ref, PAGE_SIZE, PAGE_SIZE), (new_ref, t, 0)):
        qq = lax.broadcasted_iota(I32, (rows, LANES), 1) % t
        dist = off + qq - lax.broadcasted_iota(I32, (rows, LANES), 0)
        tile = jnp.broadcast_to(rbx_ref[0:1, :], (rows, LANES))
        for b in range(1, NUM_BUCKETS):
            tile = jnp.where(dist >= thr[b], rbx_ref[b:b + 1, :], tile)
        o_ref[...] = tile - rbx_ref[NUM_BUCKETS - 1:NUM_BUCKETS, :]


def _bias_tables(rel_bias, t):
    rb = rel_bias.astype(F32)
    tp = pl.pallas_call(
        _bias_prompt_body,
        grid=(N_HEADS,),
        in_specs=[pl.BlockSpec(memory_space=pltpu.SMEM)],
        out_specs=pl.BlockSpec((1, 2, LANES, LANES), lambda h: (h, 0, 0, 0)),
        out_shape=jax.ShapeDtypeStruct((N_HEADS, 2, LANES, LANES), F32),
        name="bias_prompt",
    )(rb)
    rbx = jnp.repeat(rb, LANES // N_HEADS, axis=1)
    ts_page, ts_new = pl.pallas_call(
        functools.partial(_bias_sample_body, t=t),
        out_shape=[jax.ShapeDtypeStruct((PAGE_SIZE, LANES), F32), jax.ShapeDtypeStruct((t, LANES), F32)],
        name="bias_sample",
    )(rbx)
    return tp, ts_page, ts_new


SEL_RB = 128


def _sortable(score):
    bits = lax.bitcast_convert_type(score, I32)
    return bits ^ ((bits >> 31) & 0x7FFFFFFF)


def _select_topk(key_ref, jcut_ref, krow, nblk, tq):
    def counts(pred):
        def body(r, acc):
            row0 = pl.multiple_of(r * SEL_RB, SEL_RB)
            hit = pred(key_ref[pl.ds(row0, SEL_RB), :], row0).astype(I32)
            return acc + jnp.sum(hit.reshape(SEL_RB // 8, 8, tq), axis=0)
        acc = lax.fori_loop(0, nblk, body, jnp.zeros((8, tq), I32))
        return jnp.sum(acc, axis=0, keepdims=True)

    def count_ge(cand):
        return counts(lambda blk, row0: blk >= cand)

    zero = jnp.zeros((1, tq), I32)
    v = jnp.where(count_ge(zero) >= krow, zero, jnp.full((1, tq), INT_MIN, I32))

    def bit_body(bi, v):
        cand = v | jnp.left_shift(jnp.int32(1), 30 - bi)
        return jnp.where(count_ge(cand) >= krow, cand, v)

    v = lax.fori_loop(0, 31, bit_body, v)

    n_gt = counts(lambda blk, row0: blk > v)
    n_ge = counts(lambda blk, row0: blk >= v)
    need = krow - n_gt
    jcut_ref[...] = jnp.full((8, tq), nblk * SEL_RB, I32)

    @pl.when(jnp.max(n_ge - krow) > 0)
    def _():
        def jbit(bi, x):
            cand = x + jnp.left_shift(jnp.int32(1), 15 - bi)

            def pred(blk, row0):
                row = row0 + lax.broadcasted_iota(I32, (SEL_RB, tq), 0)
                return (blk == v) & (row <= cand)
            return jnp.where(counts(pred) < need, cand, x)
        x = lax.fori_loop(0, 16, jbit, jnp.full((1, tq), -1, I32))
        jcut_ref[...] = jnp.broadcast_to(x + 2, (8, tq))

    return v


def _selected(key_ref, jcut_ref, v, row0, tq):
    blk = key_ref[pl.ds(row0, SEL_RB), :]
    row = row0 + lax.broadcasted_iota(I32, (SEL_RB, tq), 0)
    return (blk > v) | ((blk == v) & (row < jcut_ref[0:1, :]))


P1_TQ = 256
P1_CK = SEL_RB


def _p1_body(qi_ref, kbd_ref, wt_ref, amask_ref, key_ref, jcut_ref, *, s):
    i = pl.program_id(1)
    t0 = i * P1_TQ
    n_ch = (i + 1) * (P1_TQ // P1_CK)
    tq_iota = t0 + lax.broadcasted_iota(I32, (P1_CK, P1_TQ), 1)

    def chunk_body(c, _):
        kb = kbd_ref[0, pl.ds(pl.multiple_of(c * 2 * P1_CK, 2 * P1_CK), 2 * P1_CK), :]
        acc = jnp.zeros((P1_CK, P1_TQ), F32)
        for hp in range(IDX_HEADS // 2):
            d2 = lax.dot_general(kb, qi_ref[0, :, hp * LANES:(hp + 1) * LANES], (((1,), (1,)), ((), ())),
                                 preferred_element_type=F32)
            w0 = wt_ref[0, 2 * hp:2 * hp + 1, :] * IDX_W_SCALE
            w1 = wt_ref[0, 2 * hp + 1:2 * hp + 2, :] * IDX_W_SCALE
            acc = acc + w0 * jnp.maximum(d2[:P1_CK], 0.0) + w1 * jnp.maximum(d2[P1_CK:], 0.0)
        row0 = pl.multiple_of(c * P1_CK, P1_CK)
        srow = row0 + lax.broadcasted_iota(I32, (P1_CK, P1_TQ), 0)
        key_ref[pl.ds(row0, P1_CK), :] = jnp.where(srow <= tq_iota, _sortable(acc), INT_MIN)
        return 0

    lax.fori_loop(0, n_ch, chunk_body, 0)

    krow = jnp.minimum(TOPK_MAX, t0 + lax.broadcasted_iota(I32, (1, P1_TQ), 1) + 1)
    v = _select_topk(key_ref, jcut_ref, krow, n_ch, P1_TQ)

    def out_body(r, _):
        row0 = pl.multiple_of(r * SEL_RB, SEL_RB)
        am = jnp.where(_selected(key_ref, jcut_ref, v, row0, P1_TQ), 0.0, NEG)
        for h in range(P1_TQ // LANES):
            amask_ref[0, h * LANES:(h + 1) * LANES, pl.ds(row0, SEL_RB)] = jnp.transpose(
                am[:, h * LANES:(h + 1) * LANES])
        return 0

    lax.fori_loop(0, n_ch, out_body, 0)

    def fill_body(r, _):
        amask_ref[0, :, pl.ds(pl.multiple_of(r * SEL_RB, SEL_RB), SEL_RB)] = jnp.full((P1_TQ, SEL_RB), NEG, F32)
        return 0

    lax.fori_loop(n_ch, s // SEL_RB, fill_body, 0)


def _prompt_select(qi, kbd, wt):
    b, s, _ = qi.shape
    return pl.pallas_call(
        functools.partial(_p1_body, s=s),
        grid=(b, s // P1_TQ),
        in_specs=[pl.BlockSpec((1, P1_TQ, IDX_HEADS * IDX_DIM), lambda bb, i: (bb, i, 0)),
                  pl.BlockSpec((1, 2 * s, 2 * IDX_DIM), lambda bb, i: (bb, 0, 0)),
                  pl.BlockSpec((1, IDX_HEADS, P1_TQ), lambda bb, i: (bb, 0, i))],
        out_specs=pl.BlockSpec((1, P1_TQ, s), lambda bb, i: (bb, i, 0)),
        out_shape=jax.ShapeDtypeStruct((b, s, s), F32),
        scratch_shapes=[pltpu.VMEM((s, P1_TQ), I32), pltpu.VMEM((8, P1_TQ), I32)],
        compiler_params=_cparams(("parallel", "arbitrary")),
        name="prompt_select",
    )(qi, kbd, wt)


P2_TQ = 512
P2_CK = 1024
P2_HEADS = 2


def _p2_body(q_ref, k_ref, v_ref, amask_ref, tb_ref, o_ref, *lg_refs, nc, i0):
    i = i0 + pl.program_id(1)
    tq = P2_TQ
    chunks = [(c * P2_CK, (c + 1) * P2_CK) for c in range(nc)]

    def lane_fold(x, op):
        out = x[:, :LANES]
        for j in range(1, x.shape[1] // LANES):
            out = op(out, x[:, j * LANES:(j + 1) * LANES])
        return out

    for hh, lg_ref in enumerate(lg_refs):
        hd = slice(hh * HEAD_DIM, (hh + 1) * HEAD_DIM)
        q = q_ref[0, :, hd]
        for lo, hi in chunks:
            sc = lax.dot_general(q, k_ref[0, lo:hi, hd], (((1,), (1,)), ((), ())), preferred_element_type=F32)
            lg_ref[:, lo:hi] = sc * (ATTN_SCALE * LOG2E) + amask_ref[0, :, lo:hi]

        for a in range(tq // LANES):
            rows = slice(a * LANES, (a + 1) * LANES)
            dcol = pl.multiple_of(i * tq + a * LANES, LANES)
            lg_ref[rows, pl.ds(dcol, LANES)] += tb_ref[hh, 0]
            if a > 0:
                scol = pl.multiple_of(i * tq + (a - 1) * LANES, LANES)
                lg_ref[rows, pl.ds(scol, LANES)] += tb_ref[hh, 1]

        @pl.when(i > 0)
        def _():
            scol = pl.multiple_of(i * tq - LANES, LANES)
            lg_ref[0:LANES, pl.ds(scol, LANES)] += tb_ref[hh, 1]

    for hh, lg_ref in enumerate(lg_refs):
        hd = slice(hh * HEAD_DIM, (hh + 1) * HEAD_DIM)
        m = jnp.full((tq, LANES), NEG, F32)
        for lo, hi in chunks:
            m = jnp.maximum(m, lane_fold(lg_ref[:, lo:hi], jnp.maximum))
        m = jnp.max(m, axis=1, keepdims=True)
        l = jnp.zeros((tq, LANES), F32)
        acc = jnp.zeros((tq, HEAD_DIM), F32)
        for lo, hi in chunks:
            p = jnp.exp2(lg_ref[:, lo:hi] - m)
            l = l + lane_fold(p, jnp.add)
            acc = acc + jnp.dot(p.astype(BF16), v_ref[0, lo:hi, hd], preferred_element_type=F32)
        o_ref[0, :, hd] = (acc / jnp.sum(l, axis=1, keepdims=True)).astype(o_ref.dtype)


def _prompt_attention(q, k, v, amask, tb):
    b, s, _ = q.shape
    wd = P2_HEADS * HEAD_DIM
    per_call = P2_CK // P2_TQ
    outs = []
    for nc in range(1, s // P2_CK + 1):
        i0 = (nc - 1) * per_call
        outs.append(pl.pallas_call(
            functools.partial(_p2_body, nc=nc, i0=i0),
            grid=(b, per_call, N_HEADS // P2_HEADS),
            in_specs=[pl.BlockSpec((1, P2_TQ, wd), lambda bb, i, h, i0=i0: (bb, i0 + i, h)),
                      pl.BlockSpec((1, nc * P2_CK, wd), lambda bb, i, h: (bb, 0, h)),
                      pl.BlockSpec((1, nc * P2_CK, wd), lambda bb, i, h: (bb, 0, h)),
                      pl.BlockSpec((1, P2_TQ, nc * P2_CK), lambda bb, i, h, i0=i0: (bb, i0 + i, 0)),
                      pl.BlockSpec((P2_HEADS, 2, LANES, LANES), lambda bb, i, h: (h, 0, 0, 0))],
            out_specs=pl.BlockSpec((1, P2_TQ, wd), lambda bb, i, h: (bb, i, h)),
            out_shape=jax.ShapeDtypeStruct((b, per_call * P2_TQ, ATTN_WIDTH), BF16),
            scratch_shapes=[pltpu.VMEM((P2_TQ, nc * P2_CK), F32) for _ in range(P2_HEADS)],
            compiler_params=_cparams(("parallel", "parallel", "arbitrary")),
            name=f"prompt_attention_{nc}",
        )(q, k, v, amask, tb))
    return jnp.concatenate(outs, axis=1)


def _s1_body(pt_ref, qi_ref, w_ref, *refs, n_pages, t):
    page_refs = refs[:n_pages + 1]
    o_ref = refs[n_pages + 1]
    qi = qi_ref[0]
    w = jnp.broadcast_to(w_ref[0] * IDX_W_SCALE, (IDX_HEADS * t, LANES))
    for p, kp_ref in enumerate(page_refs):
        d = jnp.dot(qi, kp_ref[0].astype(BF16), preferred_element_type=F32)
        wr = (w * jnp.maximum(d, 0.0)).reshape(IDX_HEADS, t, LANES)
        o_ref[0, :, p * LANES:(p + 1) * LANES] = jnp.sum(wr, axis=0)


def _sample_scores(qi_hq, wi_hq, cache_kidx, ki_new_pad, page_table):
    n, rows, _ = qi_hq.shape
    t = rows // IDX_HEADS
    n_pages = page_table.shape[1]
    lp = (n_pages + 1) * PAGE_SIZE

    def page_spec(p):
        return pl.BlockSpec((1, IDX_DIM, PAGE_SIZE), lambda b, pt: (pt[b, p], 0, 0))

    grid_spec = pltpu.PrefetchScalarGridSpec(
        num_scalar_prefetch=1,
        grid=(n,),
        in_specs=[pl.BlockSpec((1, rows, IDX_DIM), lambda b, pt: (b, 0, 0)),
                  pl.BlockSpec((1, rows, 1), lambda b, pt: (b, 0, 0))]
                 + [page_spec(p) for p in range(n_pages)]
                 + [pl.BlockSpec((1, IDX_DIM, PAGE_SIZE), lambda b, pt: (b, 0, 0))],
        out_specs=pl.BlockSpec((1, t, lp), lambda b, pt: (b, 0, 0)),
    )
    return pl.pallas_call(
        functools.partial(_s1_body, n_pages=n_pages, t=t),
        grid_spec=grid_spec,
        out_shape=jax.ShapeDtypeStruct((n, t, lp), F32),
        compiler_params=_cparams(("arbitrary",)),
        name="sample_scores",
    )(page_table, qi_hq, wi_hq, *([cache_kidx] * n_pages), ki_new_pad)


S2_TQ = 256


def _s2_body(sc_ref, mask_ref, key_ref, jcut_ref, *, t, lp):
    c0 = pl.program_id(0) * S2_TQ
    nblk = lp // SEL_RB
    lim = PAST_LEN + (c0 + lax.broadcasted_iota(I32, (SEL_RB, S2_TQ), 1)) % t + 1
    for r in range(nblk):
        rows = slice(r * SEL_RB, (r + 1) * SEL_RB)
        srow = r * SEL_RB + lax.broadcasted_iota(I32, (SEL_RB, S2_TQ), 0)
        key_ref[rows, :] = jnp.where(srow < lim, _sortable(sc_ref[rows, :]), INT_MIN)
    krow = jnp.full((1, S2_TQ), min(TOPK_MAX, (PAST_LEN + t) // 4), I32)
    v = _select_topk(key_ref, jcut_ref, krow, nblk, S2_TQ)
    for r in range(nblk):
        sel = _selected(key_ref, jcut_ref, v, r * SEL_RB, S2_TQ)
        mask_ref[r * SEL_RB:(r + 1) * SEL_RB, :] = jnp.where(sel, 1.0, 0.0).astype(BF16)


def _sample_select(scores_t, t):
    lp, r = scores_t.shape
    return pl.pallas_call(
        functools.partial(_s2_body, t=t, lp=lp),
        grid=(r // S2_TQ,),
        in_specs=[pl.BlockSpec((lp, S2_TQ), lambda i: (0, i))],
        out_specs=pl.BlockSpec((lp, S2_TQ), lambda i: (0, i)),
        out_shape=jax.ShapeDtypeStruct((lp, r), BF16),
        scratch_shapes=[pltpu.VMEM((lp, S2_TQ), I32), pltpu.VMEM((8, S2_TQ), I32)],
        compiler_params=_cparams(("parallel",)),
        name="sample_select",
    )(scores_t)


S3_GROUP = 8
S3_SEQS = LANES // 8


def _s3_body(pt_ref, qbd_ref, mask_ref, e_ref, *refs, n_pages, t):
    ng = n_pages // S3_GROUP
    k_refs = refs[:S3_GROUP]
    v_refs = refs[S3_GROUP:2 * S3_GROUP]
    knew_ref, vnew_ref, tsp_ref, tsn_ref, o_ref, lg_ref, acc_ref, stage_ref, vpad_ref = refs[2 * S3_GROUP:]
    g = pl.program_id(1)
    past = n_pages * PAGE_SIZE
    rows_g = S3_GROUP * PAGE_SIZE

    def stage(page_refs):
        for j in range(S3_GROUP):
            for h in range(N_HEADS):
                stage_ref[j * PAGE_SIZE:(j + 1) * PAGE_SIZE, h * HEAD_DIM:(h + 1) * HEAD_DIM] = (
                    page_refs[j][0, pl.ds(h, PAGE_SIZE, stride=N_HEADS), :].astype(BF16))

    def masked(lg, msk_rows):
        sel = jnp.dot(msk_rows, e_ref[0], preferred_element_type=F32)
        return jnp.where(sel > 0.5, lg, NEG)

    @pl.when(g < ng)
    def _():
        stage(k_refs)
        row0 = pl.multiple_of(g * rows_g, rows_g)
        lg = jnp.dot(stage_ref[...], qbd_ref[0], preferred_element_type=F32) * ATTN_SCALE
        lg_ref[pl.ds(row0, rows_g), :] = masked(lg, mask_ref[pl.ds(row0, rows_g), :])

    @pl.when(g == ng - 1)
    def _():
        lg_ref[past - PAGE_SIZE:past, :] += tsp_ref[...]
        lgn = jnp.dot(knew_ref[0].astype(BF16), qbd_ref[0], preferred_element_type=F32) * ATTN_SCALE + tsn_ref[...]
        lgn = jnp.concatenate([lgn, jnp.zeros((t, LANES), F32)], axis=0)
        lg_ref[past:past + t, :] = masked(lgn, mask_ref[past:past + 2 * t, :])[:t]
        lg_ref[past + t:, :] = jnp.full((PAGE_SIZE - t, LANES), NEG, F32)

    @pl.when(g == ng)
    def _():
        lg = lg_ref[...]
        p = jnp.exp(lg - jnp.max(lg, axis=0, keepdims=True))
        lg_ref[...] = p / jnp.sum(p, axis=0, keepdims=True)
        acc_ref[...] = jnp.zeros_like(acc_ref)

    def probs_t(row0, npages):
        return jnp.concatenate([jnp.transpose(lg_ref[pl.ds(row0 + j * PAGE_SIZE, PAGE_SIZE), :])
                                for j in range(npages)], axis=1).astype(BF16)

    @pl.when(g >= ng)
    def _():
        stage(v_refs)
        row0 = pl.multiple_of((g - ng) * rows_g, rows_g)
        acc_ref[...] += jnp.dot(probs_t(row0, S3_GROUP), stage_ref[...], preferred_element_type=F32)

    @pl.when(g == 2 * ng - 1)
    def _():
        vpad_ref[...] = jnp.zeros_like(vpad_ref)
        vpad_ref[0:t, :] = vnew_ref[0]
        acc = acc_ref[...] + jnp.dot(probs_t(past, 1), vpad_ref[...].astype(BF16), preferred_element_type=F32)
        for h in range(N_HEADS):
            o_ref[0, :, h * HEAD_DIM:(h + 1) * HEAD_DIM] = acc[h * t:(h + 1) * t, h * HEAD_DIM:(h + 1) * HEAD_DIM]


def _sample_attention(qbd, mask_t, cache_k, cache_v, k_new, v_new, ts_page, ts_new, page_table):
    n, t, _ = k_new.shape
    n_pages = page_table.shape[1]
    ng = n_pages // S3_GROUP
    lp = mask_t.shape[0]
    assert t * S3_SEQS == LANES and t * N_HEADS == LANES
    page_block = (1, PAGE_SIZE * N_HEADS, HEAD_DIM)
    sq = np.arange(LANES)
    expand = np.stack([(sq[:, None] // t == s) & (sq[:, None] % t == sq[None, :] % t) for s in range(S3_SEQS)])

    def k_spec(j):
        return pl.BlockSpec(page_block, lambda b, g, pt: (pt[b, jnp.minimum(g, ng - 1) * S3_GROUP + j], 0, 0))

    def v_spec(j):
        return pl.BlockSpec(page_block, lambda b, g, pt: (pt[b, jnp.maximum(g - ng, 0) * S3_GROUP + j], 0, 0))

    per_seq = lambda b, g, pt: (b, 0, 0)
    const2 = lambda b, g, pt: (0, 0)
    grid_spec = pltpu.PrefetchScalarGridSpec(
        num_scalar_prefetch=1,
        grid=(n, 2 * ng),
        in_specs=[pl.BlockSpec((1, ATTN_WIDTH, LANES), per_seq),
                  pl.BlockSpec((lp, LANES), lambda b, g, pt: (0, b // S3_SEQS)),
                  pl.BlockSpec((1, LANES, LANES), lambda b, g, pt: (b % S3_SEQS, 0, 0))]
                 + [k_spec(j) for j in range(S3_GROUP)] + [v_spec(j) for j in range(S3_GROUP)]
                 + [pl.BlockSpec((1, t, ATTN_WIDTH), per_seq), pl.BlockSpec((1, t, ATTN_WIDTH), per_seq),
                    pl.BlockSpec((PAGE_SIZE, LANES), const2), pl.BlockSpec((t, LANES), const2)],
        out_specs=pl.BlockSpec((1, t, ATTN_WIDTH), per_seq),
        scratch_shapes=[pltpu.VMEM((lp, LANES), F32), pltpu.VMEM((LANES, ATTN_WIDTH), F32),
                        pltpu.VMEM((S3_GROUP * PAGE_SIZE, ATTN_WIDTH), BF16),
                        pltpu.VMEM((PAGE_SIZE, ATTN_WIDTH), F32)],
    )
    return pl.pallas_call(
        functools.partial(_s3_body, n_pages=n_pages, t=t),
        grid_spec=grid_spec,
        out_shape=jax.ShapeDtypeStruct((n, t, ATTN_WIDTH), F32),
        compiler_params=_cparams(("arbitrary", "arbitrary")),
        name="sample_attention",
    )(page_table, qbd, mask_t, jnp.asarray(expand, BF16), *([cache_k] * S3_GROUP), *([cache_v] * S3_GROUP),
      k_new, v_new, ts_page, ts_new)


def _split_w_in(w):
    o = [0]
    for s in (POOL_WIDTH, ATTN_WIDTH, ATTN_WIDTH, ATTN_WIDTH, IDX_HEADS * IDX_DIM, IDX_DIM, IDX_HEADS, D_MODEL,
              D_MODEL):
        o.append(o[-1] + s)
    cut = lambda a, b: w[:, o[a]:o[b]].astype(BF16)
    kiwi = jnp.pad(cut(5, 7), ((0, 0), (0, LANES - IDX_DIM - IDX_HEADS)))
    return dict(u=cut(0, 1), q=cut(1, 2), k=cut(2, 3), v=cut(3, 4), qi=cut(4, 5), kiwi=kiwi, gp=cut(7, 8),
                ga=cut(8, 9))


def _project(x2d, ln1, wi):
    m = x2d.shape[0]
    tm = min(m, 1024)
    xn = _rmsnorm(x2d, ln1, BF16)
    (u,) = _matmul(xn, wi["u"], [F32], tm=tm, tn=1024, name="proj_u")
    (q,) = _matmul(xn, wi["q"], [BF16], tm=tm, tn=1024, name="proj_q")
    k, k16 = _matmul_heads(xn, wi["k"], tm=min(m, 512), tn=1024, name="proj_k")
    v, v16 = _matmul_heads(xn, wi["v"], tm=min(m, 512), tn=1024, name="proj_v")
    (qi,) = _matmul(xn, wi["qi"], [BF16], tm=tm, tn=1024, name="proj_qi")
    (kiwi,) = _matmul(xn, wi["kiwi"], [F32], tm=tm, tn=LANES, name="proj_kiwi")
    ki = kiwi[:, :IDX_DIM]
    wgt = kiwi[:, IDX_DIM:IDX_DIM + IDX_HEADS]
    return xn, u, q, k, k16, v, v16, qi, ki, wgt


def _finish(x2d, xn, pool_o, attn_o, wi, w_bp, w_ba, w_out, ln2, w1, w2, ln_f):
    m = x2d.shape[0]
    tm = min(m, 1024)
    mg = _merge(xn, pool_o, attn_o, wi["gp"], wi["ga"], w_bp, w_ba)
    (x1,) = _matmul(mg, w_out, [F32], tm=tm, tn=1024, res=x2d, name="out_proj")
    hn = _rmsnorm(x1, ln2, BF16)
    (h,) = _matmul(hn, w1, [BF16], tm=tm, tn=1024, act="relu2", name="mlp_in")
    x2 = _matmul_ktiled(h, w2, x1, tm=tm, tn=1024, tk=2048, name="mlp_out")
    return _rmsnorm(x2, ln_f, F32)


def kernel(x_prompt, x_sample, cache_k, cache_v, cache_kidx, state_pool, page_table, ln1, w_in, pool_w,
           pool_scale, rel_bias, w_branch_pool, w_branch_attn, w_out, ln2, w_mlp_in, w_mlp_out, ln_f):
    bn, s, _ = x_prompt.shape
    n, t, _ = x_sample.shape
    depth = w_in.shape[0]
    assert depth == 1, "one layer: the caches and pooling state of layer 0 are the only ones read"
    n_pages = page_table.shape[1]
    n_phys = cache_k.shape[1]

    wi = _split_w_in(w_in[0])
    pw = pool_w[0].astype(BF16)
    w_bp = w_branch_pool[0].astype(BF16)
    w_ba = w_branch_attn[0].astype(BF16)
    wo = w_out[0].astype(BF16)
    w1 = w_mlp_in[0].astype(BF16)
    w2 = w_mlp_out[0].astype(BF16)
    tb_prompt, ts_page, ts_new = _bias_tables(rel_bias, t)

    xp = x_prompt.reshape(bn * s, D_MODEL)
    xn, u, q, k, k16, v, v16, qi, ki, wgt = _project(xp, ln1[0], wi)
    pool_o = _pool_prompt(u.reshape(bn, s, POOL_WIDTH), pw, pool_scale[0]).reshape(bn * s, POOL_WIDTH)
    ki_c = ki.astype(BF16).reshape(bn, s // P1_CK, P1_CK, IDX_DIM)
    zeros = jnp.zeros_like(ki_c)
    kbd = jnp.concatenate([jnp.concatenate([ki_c, zeros], axis=-1), jnp.concatenate([zeros, ki_c], axis=-1)],
                          axis=2).reshape(bn, 2 * s, 2 * IDX_DIM)
    amask = _prompt_select(qi.reshape(bn, s, IDX_HEADS * IDX_DIM), kbd,
                           jnp.swapaxes(wgt.reshape(bn, s, IDX_HEADS), 1, 2))
    attn_o = _prompt_attention(q.reshape(bn, s, ATTN_WIDTH), k16.reshape(bn, s, ATTN_WIDTH),
                               v16.reshape(bn, s, ATTN_WIDTH), amask, tb_prompt).reshape(bn * s, ATTN_WIDTH)
    y_prompt = _finish(xp, xn, pool_o, attn_o, wi, w_bp, w_ba, wo, ln2[0], w1, w2, ln_f).reshape(bn, s, D_MODEL)
    k_prompt = k.reshape(1, bn, s, N_HEADS, HEAD_DIM)
    v_prompt = v.reshape(1, bn, s, N_HEADS, HEAD_DIM)
    kidx_prompt = ki.reshape(1, bn, s, IDX_DIM)
    pool_prompt = u.reshape(bn, s, POOL_WIDTH)[:, s - POOL_PREFIX:, :][None]

    xs = x_sample.reshape(n * t, D_MODEL)
    xn, u, q, k, k16, v, v16, qi, ki, wgt = _project(xs, ln1[0], wi)
    u3 = u.reshape(n, t, POOL_WIDTH)
    pool_o = _pool_sample(u3, state_pool[0], pw, pool_scale[0])
    qi_hq = qi.reshape(n, t, IDX_HEADS, IDX_DIM).transpose(0, 2, 1, 3).reshape(n, IDX_HEADS * t, IDX_DIM)
    wi_hq = wgt.reshape(n, t, IDX_HEADS).transpose(0, 2, 1).reshape(n, IDX_HEADS * t, 1)
    ki_new_pad = jnp.pad(jnp.swapaxes(ki.reshape(n, t, IDX_DIM), 1, 2), ((0, 0), (0, 0), (0, PAGE_SIZE - t)))
    scores = _sample_scores(qi_hq, wi_hq, jnp.swapaxes(cache_kidx[0], 1, 2), ki_new_pad, page_table)
    lp = scores.shape[-1]
    mask_t = _sample_select(scores.reshape(n * t, lp).T, t)
    q_t = q.reshape(n, t, N_HEADS, HEAD_DIM).transpose(0, 2, 3, 1)
    eye = jnp.eye(N_HEADS, dtype=BF16)
    qbd = (q_t[:, :, :, None, :] * eye[None, :, None, :, None]).reshape(n, ATTN_WIDTH, N_HEADS * t)
    attn_o = _sample_attention(qbd, mask_t, cache_k[0].reshape(n_phys, PAGE_SIZE * N_HEADS, HEAD_DIM),
                               cache_v[0].reshape(n_phys, PAGE_SIZE * N_HEADS, HEAD_DIM), k.reshape(n, t, ATTN_WIDTH),
                               v.reshape(n, t, ATTN_WIDTH), ts_page, ts_new, page_table)
    attn_o = attn_o.reshape(n * t, ATTN_WIDTH).astype(BF16)
    y_sample = _finish(xs, xn, pool_o, attn_o, wi, w_bp, w_ba, wo, ln2[0], w1, w2, ln_f).reshape(n, t, D_MODEL)
    k_sample = k.reshape(1, n, t, N_HEADS, HEAD_DIM)
    v_sample = v.reshape(1, n, t, N_HEADS, HEAD_DIM)
    kidx_sample = ki.reshape(1, n, t, IDX_DIM)
    pool_sample = jnp.concatenate([state_pool[0][:, t:, :], u3], axis=1)[None]

    return (y_prompt, y_sample, k_prompt, v_prompt, kidx_prompt, pool_prompt,
            k_sample, v_sample, kidx_sample, pool_sample)
```

```python
import functools
import math

import numpy as np
import jax
import jax.numpy as jnp
from jax import lax
from jax.experimental import pallas as pl
from jax.experimental.pallas import tpu as pltpu

D_MODEL = 4096
PAST_LEN = 2048
PAGE_SIZE = 128
POOL_WIDTH = D_MODEL // 2
POOL_WINDOWS = (2, 4, 8, 16)
N_POOL_GROUPS = len(POOL_WINDOWS)
POOL_GROUP = POOL_WIDTH // N_POOL_GROUPS
POOL_PREFIX = max(POOL_WINDOWS) - 1
HEAD_DIM = 128
N_HEADS = (D_MODEL // 2) // HEAD_DIM
ATTN_WIDTH = N_HEADS * HEAD_DIM
IDX_HEADS = 32
IDX_DIM = 64
TOPK_MAX = 256
NUM_BUCKETS = 32
MAX_DISTANCE = 128
D_FF = 4 * D_MODEL
EPS = 1e-6

F32 = jnp.float32
BF16 = jnp.bfloat16
I32 = jnp.int32

LANES = 128
VMEM_LIMIT = 56 * 1024 * 1024
INT_MIN = -(2 ** 31)
NEG = -1e30
HALO = 16
PROMPT_HALO = 128
IDX_W_SCALE = (IDX_HEADS ** -0.5) * (IDX_DIM ** -0.5)
ATTN_SCALE = HEAD_DIM ** -0.5
LOG2E = math.log2(math.e)


def _cparams(sem):
    return pltpu.CompilerParams(dimension_semantics=sem, vmem_limit_bytes=VMEM_LIMIT)


def _rmsnorm_body(x_ref, g_ref, o_ref):
    x = x_ref[...]
    y = x * lax.rsqrt(jnp.mean(x * x, axis=-1, keepdims=True) + EPS)
    o_ref[...] = (y * g_ref[...]).astype(o_ref.dtype)


def _rmsnorm(x, g, out_dtype, tm=512):
    m, d = x.shape
    return pl.pallas_call(
        _rmsnorm_body,
        grid=(m // tm,),
        in_specs=[pl.BlockSpec((tm, d), lambda i: (i, 0)), pl.BlockSpec((1, d), lambda i: (0, 0))],
        out_specs=pl.BlockSpec((tm, d), lambda i: (i, 0)),
        out_shape=jax.ShapeDtypeStruct((m, d), out_dtype),
        compiler_params=_cparams(("parallel",)),
        name="rmsnorm",
    )(x, g.reshape(1, d))


def _mm_body(*refs, n_out, act, has_res):
    x_ref, w_ref = refs[0], refs[1]
    res_ref = refs[2] if has_res else None
    outs = refs[2 + int(has_res):2 + int(has_res) + n_out]
    acc = jnp.dot(x_ref[...], w_ref[...], preferred_element_type=F32)
    if act == "relu2":
        acc = jnp.square(jnp.maximum(acc, 0.0))
    if has_res:
        acc = res_ref[...] + acc
    for o in outs:
        o[...] = acc.astype(o.dtype)


def _matmul(x, w, out_dtypes, *, tm, tn, act=None, res=None, name="matmul", col0=0, n=None):
    m, k = x.shape
    n = w.shape[1] if n is None else n
    assert col0 % tn == 0
    cb = col0 // tn
    in_specs = [pl.BlockSpec((tm, k), lambda i, j: (i, 0)), pl.BlockSpec((k, tn), lambda i, j: (0, cb + j))]
    args = [x, w]
    if res is not None:
        in_specs.append(pl.BlockSpec((tm, tn), lambda i, j: (i, j)))
        args.append(res)
    outs = pl.pallas_call(
        functools.partial(_mm_body, n_out=len(out_dtypes), act=act, has_res=res is not None),
        grid=(m // tm, n // tn),
        in_specs=in_specs,
        out_specs=[pl.BlockSpec((tm, tn), lambda i, j: (i, j)) for _ in out_dtypes],
        out_shape=[jax.ShapeDtypeStruct((m, n), dt) for dt in out_dtypes],
        compiler_params=_cparams(("parallel", "parallel")),
        name=name,
    )(*args)
    return outs


def _mmh_body(x_ref, w_ref, o32_ref, o16_ref, *, heads):
    j = pl.program_id(1)
    acc = jnp.dot(x_ref[...], w_ref[...], preferred_element_type=F32)
    o16_ref[...] = acc.astype(BF16)
    tm = acc.shape[0]
    for hh in range(heads):
        o32_ref[pl.ds(j * heads + hh, tm, stride=N_HEADS), :] = acc[:, hh * HEAD_DIM:(hh + 1) * HEAD_DIM]


def _matmul_heads(x, w, *, tm, tn, name, col0=0):
    m, k = x.shape
    n = N_HEADS * HEAD_DIM
    assert col0 % tn == 0
    cb = col0 // tn
    return pl.pallas_call(
        functools.partial(_mmh_body, heads=tn // HEAD_DIM),
        grid=(m // tm, n // tn),
        in_specs=[pl.BlockSpec((tm, k), lambda i, j: (i, 0)), pl.BlockSpec((k, tn), lambda i, j: (0, cb + j))],
        out_specs=[pl.BlockSpec((tm * N_HEADS, HEAD_DIM), lambda i, j: (i, 0)),
                   pl.BlockSpec((tm, tn), lambda i, j: (i, j))],
        out_shape=[jax.ShapeDtypeStruct((m * N_HEADS, HEAD_DIM), F32), jax.ShapeDtypeStruct((m, n), BF16)],
        compiler_params=_cparams(("parallel", "arbitrary")),
        name=name,
    )(x, w)


def _mmk_body(x_ref, w_ref, res_ref, o_ref, acc_ref, *, nk):
    k = pl.program_id(2)

    @pl.when(k == 0)
    def _():
        acc_ref[...] = jnp.zeros_like(acc_ref)

    acc_ref[...] += jnp.dot(x_ref[...], w_ref[...], preferred_element_type=F32)

    @pl.when(k == nk - 1)
    def _():
        o_ref[...] = res_ref[...] + acc_ref[...]


def _matmul_ktiled(x, w, res, *, tm, tn, tk, name):
    m, k = x.shape
    n = w.shape[1]
    nk = k // tk
    return pl.pallas_call(
        functools.partial(_mmk_body, nk=nk),
        grid=(m // tm, n // tn, nk),
        in_specs=[pl.BlockSpec((tm, tk), lambda i, j, kk: (i, kk)),
                  pl.BlockSpec((tk, tn), lambda i, j, kk: (kk, j)),
                  pl.BlockSpec((tm, tn), lambda i, j, kk: (i, j))],
        out_specs=pl.BlockSpec((tm, tn), lambda i, j, kk: (i, j)),
        out_shape=jax.ShapeDtypeStruct((m, n), F32),
        scratch_shapes=[pltpu.VMEM((tm, tn), F32)],
        compiler_params=_cparams(("parallel", "parallel", "arbitrary")),
        name=name,
    )(x, w, res)


def _merge_body(xn_ref, po_ref, ao_ref, wgp_ref, wga_ref, wbp_ref, wba_ref, o_ref):
    xn = xn_ref[...]
    gp = jnp.dot(xn, wgp_ref[...], preferred_element_type=F32)
    ga = jnp.dot(xn, wga_ref[...], preferred_element_type=F32)
    a = jnp.dot(po_ref[...], wbp_ref[...], preferred_element_type=F32)
    b = jnp.dot(ao_ref[...], wba_ref[...], preferred_element_type=F32)
    o_ref[...] = (jax.nn.sigmoid(gp) * a + jax.nn.sigmoid(ga) * b).astype(o_ref.dtype)


def _merge(xn, po, ao, wgp, wga, wbp, wba, *, tm=512, tn=512):
    m, d = xn.shape
    kb = po.shape[1]
    n = wgp.shape[1]
    row = lambda i, j: (i, 0)
    col = lambda i, j: (0, j)
    return pl.pallas_call(
        _merge_body,
        grid=(m // tm, n // tn),
        in_specs=[pl.BlockSpec((tm, d), row), pl.BlockSpec((tm, kb), row), pl.BlockSpec((tm, kb), row),
                  pl.BlockSpec((d, tn), col), pl.BlockSpec((d, tn), col),
                  pl.BlockSpec((kb, tn), col), pl.BlockSpec((kb, tn), col)],
        out_specs=pl.BlockSpec((tm, tn), lambda i, j: (i, j)),
        out_shape=jax.ShapeDtypeStruct((m, n), BF16),
        compiler_params=_cparams(("parallel", "parallel")),
        name="merge",
    )(xn, po, ao, wgp, wga, wbp, wba)


def _pool_body(*refs, has_halo):
    if has_halo:
        halo_ref, cur_ref, s_ref, inv_ref, pw_ref, sc_ref, o_ref = refs
        halo = jnp.where(pl.program_id(1) == 0, 0.0, halo_ref[0])
        cur = cur_ref[0]
        ext = jnp.concatenate([halo, cur], axis=0)
    else:
        ext_ref, cur_ref, s_ref, inv_ref, pw_ref, sc_ref, o_ref = refs
        ext = ext_ref[...]
        cur = cur_ref[...]
    hi = ext.astype(BF16)
    lo = (ext - hi.astype(F32)).astype(BF16)
    band = s_ref[0]
    win = (jnp.dot(band, hi, preferred_element_type=F32) + jnp.dot(band, lo, preferred_element_type=F32))
    pooled = win * inv_ref[0] - cur
    y = jnp.dot(pooled.astype(BF16), pw_ref[0], preferred_element_type=F32) * sc_ref[...]
    if has_halo:
        o_ref[0] = y.astype(o_ref.dtype)
    else:
        o_ref[...] = y.astype(o_ref.dtype)


def _pool_prompt(u, pool_w, pool_scale, *, tb=512):
    n, t, _ = u.shape
    halo = PROMPT_HALO
    band = np.zeros((N_POOL_GROUPS, tb, halo + tb), np.float32)
    inv = np.zeros((N_POOL_GROUPS, t, 1), np.float32)
    r = np.arange(tb)[:, None]
    c = np.arange(halo + tb)[None, :]
    for g, w in enumerate(POOL_WINDOWS):
        band[g] = ((c >= r + halo - w + 1) & (c <= r + halo)).astype(np.float32)
        inv[g, :, 0] = 1.0 / np.minimum(w, np.arange(t) + 1)
    hb = tb // halo
    return pl.pallas_call(
        functools.partial(_pool_body, has_halo=True),
        grid=(n, t // tb, N_POOL_GROUPS),
        in_specs=[pl.BlockSpec((1, halo, POOL_GROUP), lambda b, i, g: (b, jnp.maximum(i * hb - 1, 0), g)),
                  pl.BlockSpec((1, tb, POOL_GROUP), lambda b, i, g: (b, i, g)),
                  pl.BlockSpec((1, tb, halo + tb), lambda b, i, g: (g, 0, 0)),
                  pl.BlockSpec((1, tb, 1), lambda b, i, g: (g, i, 0)),
                  pl.BlockSpec((1, POOL_GROUP, POOL_GROUP), lambda b, i, g: (g, 0, 0)),
                  pl.BlockSpec((1, POOL_GROUP), lambda b, i, g: (0, g))],
        out_specs=pl.BlockSpec((1, tb, POOL_GROUP), lambda b, i, g: (b, i, g)),
        out_shape=jax.ShapeDtypeStruct((n, t, POOL_WIDTH), BF16),
        compiler_params=_cparams(("parallel", "arbitrary", "arbitrary")),
        name="pool_prompt",
    )(u, u, jnp.asarray(band, BF16), jnp.asarray(inv), pool_w, pool_scale.reshape(1, POOL_WIDTH))


def _pool_sample(u, prefix, pool_w, pool_scale, *, sb=16):
    n, t, _ = u.shape
    e = HALO + t
    ext = jnp.concatenate([jnp.zeros((n, HALO - POOL_PREFIX, POOL_WIDTH), F32), prefix.astype(F32), u], axis=1)
    ext = ext.reshape(n * e, POOL_WIDTH)
    band = np.zeros((N_POOL_GROUPS, sb * t, sb * e), np.float32)
    inv = np.zeros((N_POOL_GROUPS, sb * t, 1), np.float32)
    r = np.arange(sb * t)[:, None]
    c = np.arange(sb * e)[None, :]
    for g, w in enumerate(POOL_WINDOWS):
        pos = HALO + r % t
        band[g] = ((r // t == c // e) & (c % e >= pos - w + 1) & (c % e <= pos)).astype(np.float32)
        inv[g] = 1.0 / w
    out = pl.pallas_call(
        functools.partial(_pool_body, has_halo=False),
        grid=(n // sb, N_POOL_GROUPS),
        in_specs=[pl.BlockSpec((sb * e, POOL_GROUP), lambda i, g: (i, g)),
                  pl.BlockSpec((sb * t, POOL_GROUP), lambda i, g: (i, g)),
                  pl.BlockSpec((1, sb * t, sb * e), lambda i, g: (g, 0, 0)),
                  pl.BlockSpec((1, sb * t, 1), lambda i, g: (g, 0, 0)),
                  pl.BlockSpec((1, POOL_GROUP, POOL_GROUP), lambda i, g: (g, 0, 0)),
                  pl.BlockSpec((1, POOL_GROUP), lambda i, g: (0, g))],
        out_specs=pl.BlockSpec((sb * t, POOL_GROUP), lambda i, g: (i, g)),
        out_shape=jax.ShapeDtypeStruct((n * t, POOL_WIDTH), BF16),
        compiler_params=_cparams(("parallel", "arbitrary")),
        name="pool_sample",
    )(ext, u.reshape(n * t, POOL_WIDTH), jnp.asarray(band, BF16), jnp.asarray(inv), pool_w,
      pool_scale.reshape(1, POOL_WIDTH))
    return out


def _bucket_thresholds():
    d = np.arange(0, 4 * MAX_DISTANCE)
    max_exact = NUM_BUCKETS // 2
    df = np.maximum(d, 1).astype(np.float32)
    large = max_exact + (np.log(df / np.float32(max_exact)) / np.float32(math.log(MAX_DISTANCE / max_exact))
                         * np.float32(NUM_BUCKETS - max_exact)).astype(np.int32)
    bucket = np.where(d < max_exact, d, np.minimum(large, NUM_BUCKETS - 1))
    assert np.all(np.diff(bucket) >= 0) and np.all(bucket[MAX_DISTANCE + 1:] == NUM_BUCKETS - 1)
    return [int(np.argmax(bucket >= b)) for b in range(NUM_BUCKETS)]


def _bias_prompt_body(rb_ref, o_ref):
    h = pl.program_id(0)
    thr = _bucket_thresholds()
    base = lax.broadcasted_iota(I32, (LANES, LANES), 0) - lax.broadcasted_iota(I32, (LANES, LANES), 1)
    for j in range(2):
        dist = base + j * LANES
        tile = jnp.full((LANES, LANES), rb_ref[0, h], F32)
        for b in range(1, NUM_BUCKETS):
            tile = jnp.where(dist >= thr[b], rb_ref[b, h], tile)
        o_ref[0, j] = (tile - rb_ref[NUM_BUCKETS - 1, h]) * LOG2E


def _bias_sample_body(rbx_ref, page_ref, new_ref, *, t):
    thr = _bucket_thresholds()
    for o_ref, rows, off in ((page_ref, PAGE_SIZE, PAGE_SIZE), (new_ref, t, 0)):
        qq = lax.broadcasted_iota(I32, (rows, LANES), 1) % t
        dist = off + qq - lax.broadcasted_iota(I32, (rows, LANES), 0)
        tile = jnp.broadcast_to(rbx_ref[0:1, :], (rows, LANES))
        for b in range(1, NUM_BUCKETS):
            tile = jnp.where(dist >= thr[b], rbx_ref[b:b + 1, :], tile)
        o_ref[...] = tile - rbx_ref[NUM_BUCKETS - 1:NUM_BUCKETS, :]


def _bias_tables(rel_bias, t):
    rb = rel_bias.astype(F32)
    tp = pl.pallas_call(
        _bias_prompt_body,
        grid=(N_HEADS,),
        in_specs=[pl.BlockSpec(memory_space=pltpu.SMEM)],
        out_specs=pl.BlockSpec((1, 2, LANES, LANES), lambda h: (h, 0, 0, 0)),
        out_shape=jax.ShapeDtypeStruct((N_HEADS, 2, LANES, LANES), F32),
        name="bias_prompt",
    )(rb)
    rbx = jnp.repeat(rb, LANES // N_HEADS, axis=1)
    ts_page, ts_new = pl.pallas_call(
        functools.partial(_bias_sample_body, t=t),
        out_shape=[jax.ShapeDtypeStruct((PAGE_SIZE, LANES), F32), jax.ShapeDtypeStruct((t, LANES), F32)],
        name="bias_sample",
    )(rbx)
    return tp, ts_page, ts_new


SEL_RB = 128


def _sortable(score):
    bits = lax.bitcast_convert_type(score, I32)
    return bits ^ ((bits >> 31) & 0x7FFFFFFF)


def _select_topk(key_ref, jcut_ref, krow, nblk, tq):
    def counts(pred):
        def body(r, acc):
            row0 = pl.multiple_of(r * SEL_RB, SEL_RB)
            hit = pred(key_ref[pl.ds(row0, SEL_RB), :], row0).astype(I32)
            return acc + jnp.sum(hit.reshape(SEL_RB // 8, 8, tq), axis=0)
        acc = lax.fori_loop(0, nblk, body, jnp.zeros((8, tq), I32))
        return jnp.sum(acc, axis=0, keepdims=True)

    def count_ge(cand):
        return counts(lambda blk, row0: blk >= cand)

    zero = jnp.zeros((1, tq), I32)
    v = jnp.where(count_ge(zero) >= krow, zero, jnp.full((1, tq), INT_MIN, I32))

    def bit_body(bi, v):
        cand = v | jnp.left_shift(jnp.int32(1), 30 - bi)
        return jnp.where(count_ge(cand) >= krow, cand, v)

    v = lax.fori_loop(0, 31, bit_body, v)

    n_gt = counts(lambda blk, row0: blk > v)
    n_ge = counts(lambda blk, row0: blk >= v)
    need = krow - n_gt
    jcut_ref[...] = jnp.full((8, tq), nblk * SEL_RB, I32)

    @pl.when(jnp.max(n_ge - krow) > 0)
    def _():
        def jbit(bi, x):
            cand = x + jnp.left_shift(jnp.int32(1), 15 - bi)

            def pred(blk, row0):
                row = row0 + lax.broadcasted_iota(I32, (SEL_RB, tq), 0)
                return (blk == v) & (row <= cand)
            return jnp.where(counts(pred) < need, cand, x)
        x = lax.fori_loop(0, 16, jbit, jnp.full((1, tq), -1, I32))
        jcut_ref[...] = jnp.broadcast_to(x + 2, (8, tq))

    return v


def _selected(key_ref, jcut_ref, v, row0, tq):
    blk = key_ref[pl.ds(row0, SEL_RB), :]
    row = row0 + lax.broadcasted_iota(I32, (SEL_RB, tq), 0)
    return (blk > v) | ((blk == v) & (row < jcut_ref[0:1, :]))


P1_TQ = 256
P1_CK = SEL_RB


def _p1_body(qi_ref, kbd_ref, wt_ref, amask_ref, key_ref, jcut_ref, *, s):
    i = pl.program_id(1)
    t0 = i * P1_TQ
    n_ch = (i + 1) * (P1_TQ // P1_CK)
    tq_iota = t0 + lax.broadcasted_iota(I32, (P1_CK, P1_TQ), 1)

    def chunk_body(c, _):
        kb = kbd_ref[0, pl.ds(pl.multiple_of(c * 2 * P1_CK, 2 * P1_CK), 2 * P1_CK), :]
        acc = jnp.zeros((P1_CK, P1_TQ), F32)
        for hp in range(IDX_HEADS // 2):
            d2 = lax.dot_general(kb, qi_ref[0, :, hp * LANES:(hp + 1) * LANES], (((1,), (1,)), ((), ())),
                                 preferred_element_type=F32)
            w0 = wt_ref[0, 2 * hp:2 * hp + 1, :] * IDX_W_SCALE
            w1 = wt_ref[0, 2 * hp + 1:2 * hp + 2, :] * IDX_W_SCALE
            acc = acc + w0 * jnp.maximum(d2[:P1_CK], 0.0) + w1 * jnp.maximum(d2[P1_CK:], 0.0)
        row0 = pl.multiple_of(c * P1_CK, P1_CK)
        srow = row0 + lax.broadcasted_iota(I32, (P1_CK, P1_TQ), 0)
        key_ref[pl.ds(row0, P1_CK), :] = jnp.where(srow <= tq_iota, _sortable(acc), INT_MIN)
        return 0

    lax.fori_loop(0, n_ch, chunk_body, 0)

    krow = jnp.minimum(TOPK_MAX, t0 + lax.broadcasted_iota(I32, (1, P1_TQ), 1) + 1)
    v = _select_topk(key_ref, jcut_ref, krow, n_ch, P1_TQ)

    def out_body(r, _):
        row0 = pl.multiple_of(r * SEL_RB, SEL_RB)
        am = jnp.where(_selected(key_ref, jcut_ref, v, row0, P1_TQ), 0.0, NEG)
        for h in range(P1_TQ // LANES):
            amask_ref[0, h * LANES:(h + 1) * LANES, pl.ds(row0, SEL_RB)] = jnp.transpose(
                am[:, h * LANES:(h + 1) * LANES])
        return 0

    lax.fori_loop(0, n_ch, out_body, 0)

    def fill_body(r, _):
        amask_ref[0, :, pl.ds(pl.multiple_of(r * SEL_RB, SEL_RB), SEL_RB)] = jnp.full((P1_TQ, SEL_RB), NEG, F32)
        return 0

    lax.fori_loop(n_ch, s // SEL_RB, fill_body, 0)


def _prompt_select(qi, kbd, wt):
    b, s, _ = qi.shape
    return pl.pallas_call(
        functools.partial(_p1_body, s=s),
        grid=(b, s // P1_TQ),
        in_specs=[pl.BlockSpec((1, P1_TQ, IDX_HEADS * IDX_DIM), lambda bb, i: (bb, i, 0)),
                  pl.BlockSpec((1, 2 * s, 2 * IDX_DIM), lambda bb, i: (bb, 0, 0)),
                  pl.BlockSpec((1, IDX_HEADS, P1_TQ), lambda bb, i: (bb, 0, i))],
        out_specs=pl.BlockSpec((1, P1_TQ, s), lambda bb, i: (bb, i, 0)),
        out_shape=jax.ShapeDtypeStruct((b, s, s), F32),
        scratch_shapes=[pltpu.VMEM((s, P1_TQ), I32), pltpu.VMEM((8, P1_TQ), I32)],
        compiler_params=_cparams(("parallel", "arbitrary")),
        name="prompt_select",
    )(qi, kbd, wt)


P2_TQ = 512
P2_CK = 1024
P2_HEADS = 2


def _p2_body(q_ref, k_ref, v_ref, amask_ref, tb_ref, o_ref, *lg_refs, nc, i0):
    i = i0 + pl.program_id(1)
    tq = P2_TQ
    chunks = [(c * P2_CK, (c + 1) * P2_CK) for c in range(nc)]

    def lane_fold(x, op):
        out = x[:, :LANES]
        for j in range(1, x.shape[1] // LANES):
            out = op(out, x[:, j * LANES:(j + 1) * LANES])
        return out

    for hh, lg_ref in enumerate(lg_refs):
        hd = slice(hh * HEAD_DIM, (hh + 1) * HEAD_DIM)
        q = q_ref[0, :, hd]
        for lo, hi in chunks:
            sc = lax.dot_general(q, k_ref[0, lo:hi, hd], (((1,), (1,)), ((), ())), preferred_element_type=F32)
            lg_ref[:, lo:hi] = sc * (ATTN_SCALE * LOG2E) + amask_ref[0, :, lo:hi]

        for a in range(tq // LANES):
            rows = slice(a * LANES, (a + 1) * LANES)
            dcol = pl.multiple_of(i * tq + a * LANES, LANES)
            lg_ref[rows, pl.ds(dcol, LANES)] += tb_ref[hh, 0]
            if a > 0:
                scol = pl.multiple_of(i * tq + (a - 1) * LANES, LANES)
                lg_ref[rows, pl.ds(scol, LANES)] += tb_ref[hh, 1]

        @pl.when(i > 0)
        def _():
            scol = pl.multiple_of(i * tq - LANES, LANES)
            lg_ref[0:LANES, pl.ds(scol, LANES)] += tb_ref[hh, 1]

    for hh, lg_ref in enumerate(lg_refs):
        hd = slice(hh * HEAD_DIM, (hh + 1) * HEAD_DIM)
        m = jnp.full((tq, LANES), NEG, F32)
        for lo, hi in chunks:
            m = jnp.maximum(m, lane_fold(lg_ref[:, lo:hi], jnp.maximum))
        m = jnp.max(m, axis=1, keepdims=True)
        l = jnp.zeros((tq, LANES), F32)
        acc = jnp.zeros((tq, HEAD_DIM), F32)
        for lo, hi in chunks:
            p = jnp.exp2(lg_ref[:, lo:hi] - m)
            l = l + lane_fold(p, jnp.add)
            acc = acc + jnp.dot(p.astype(BF16), v_ref[0, lo:hi, hd], preferred_element_type=F32)
        o_ref[0, :, hd] = (acc / jnp.sum(l, axis=1, keepdims=True)).astype(o_ref.dtype)


def _prompt_attention(q, k, v, amask, tb):
    b, s, _ = q.shape
    wd = P2_HEADS * HEAD_DIM
    per_call = P2_CK // P2_TQ
    outs = []
    for nc in range(1, s // P2_CK + 1):
        i0 = (nc - 1) * per_call
        outs.append(pl.pallas_call(
            functools.partial(_p2_body, nc=nc, i0=i0),
            grid=(b, per_call, N_HEADS // P2_HEADS),
            in_specs=[pl.BlockSpec((1, P2_TQ, wd), lambda bb, i, h, i0=i0: (bb, i0 + i, h)),
                      pl.BlockSpec((1, nc * P2_CK, wd), lambda bb, i, h: (bb, 0, h)),
                      pl.BlockSpec((1, nc * P2_CK, wd), lambda bb, i, h: (bb, 0, h)),
                      pl.BlockSpec((1, P2_TQ, nc * P2_CK), lambda bb, i, h, i0=i0: (bb, i0 + i, 0)),
                      pl.BlockSpec((P2_HEADS, 2, LANES, LANES), lambda bb, i, h: (h, 0, 0, 0))],
            out_specs=pl.BlockSpec((1, P2_TQ, wd), lambda bb, i, h: (bb, i, h)),
            out_shape=jax.ShapeDtypeStruct((b, per_call * P2_TQ, ATTN_WIDTH), BF16),
            scratch_shapes=[pltpu.VMEM((P2_TQ, nc * P2_CK), F32) for _ in range(P2_HEADS)],
            compiler_params=_cparams(("parallel", "parallel", "arbitrary")),
            name=f"prompt_attention_{nc}",
        )(q, k, v, amask, tb))
    return jnp.concatenate(outs, axis=1)


def _s1_body(pt_ref, qi_ref, w_ref, *refs, n_pages, t):
    page_refs = refs[:n_pages + 1]
    o_ref = refs[n_pages + 1]
    qi = qi_ref[0]
    w = jnp.broadcast_to(w_ref[0] * IDX_W_SCALE, (IDX_HEADS * t, LANES))
    for p, kp_ref in enumerate(page_refs):
        d = jnp.dot(qi, kp_ref[0].astype(BF16), preferred_element_type=F32)
        wr = (w * jnp.maximum(d, 0.0)).reshape(IDX_HEADS, t, LANES)
        o_ref[0, :, p * LANES:(p + 1) * LANES] = jnp.sum(wr, axis=0)


def _sample_scores(qi_hq, wi_hq, cache_kidx, ki_new_pad, page_table):
    n, rows, _ = qi_hq.shape
    t = rows // IDX_HEADS
    n_pages = page_table.shape[1]
    lp = (n_pages + 1) * PAGE_SIZE

    def page_spec(p):
        return pl.BlockSpec((1, IDX_DIM, PAGE_SIZE), lambda b, pt: (pt[b, p], 0, 0))

    grid_spec = pltpu.PrefetchScalarGridSpec(
        num_scalar_prefetch=1,
        grid=(n,),
        in_specs=[pl.BlockSpec((1, rows, IDX_DIM), lambda b, pt: (b, 0, 0)),
                  pl.BlockSpec((1, rows, 1), lambda b, pt: (b, 0, 0))]
                 + [page_spec(p) for p in range(n_pages)]
                 + [pl.BlockSpec((1, IDX_DIM, PAGE_SIZE), lambda b, pt: (b, 0, 0))],
        out_specs=pl.BlockSpec((1, t, lp), lambda b, pt: (b, 0, 0)),
    )
    return pl.pallas_call(
        functools.partial(_s1_body, n_pages=n_pages, t=t),
        grid_spec=grid_spec,
        out_shape=jax.ShapeDtypeStruct((n, t, lp), F32),
        compiler_params=_cparams(("arbitrary",)),
        name="sample_scores",
    )(page_table, qi_hq, wi_hq, *([cache_kidx] * n_pages), ki_new_pad)


S2_TQ = 256


def _s2_body(sc_ref, mask_ref, key_ref, jcut_ref, *, t, lp):
    c0 = pl.program_id(0) * S2_TQ
    nblk = lp // SEL_RB
    lim = PAST_LEN + (c0 + lax.broadcasted_iota(I32, (SEL_RB, S2_TQ), 1)) % t + 1
    for r in range(nblk):
        rows = slice(r * SEL_RB, (r + 1) * SEL_RB)
        srow = r * SEL_RB + lax.broadcasted_iota(I32, (SEL_RB, S2_TQ), 0)
        key_ref[rows, :] = jnp.where(srow < lim, _sortable(sc_ref[rows, :]), INT_MIN)
    krow = jnp.full((1, S2_TQ), min(TOPK_MAX, (PAST_LEN + t) // 4), I32)
    v = _select_topk(key_ref, jcut_ref, krow, nblk, S2_TQ)
    for r in range(nblk):
        sel = _selected(key_ref, jcut_ref, v, r * SEL_RB, S2_TQ)
        mask_ref[r * SEL_RB:(r + 1) * SEL_RB, :] = jnp.where(sel, 1.0, 0.0).astype(BF16)


def _sample_select(scores_t, t):
    lp, r = scores_t.shape
    return pl.pallas_call(
        functools.partial(_s2_body, t=t, lp=lp),
        grid=(r // S2_TQ,),
        in_specs=[pl.BlockSpec((lp, S2_TQ), lambda i: (0, i))],
        out_specs=pl.BlockSpec((lp, S2_TQ), lambda i: (0, i)),
        out_shape=jax.ShapeDtypeStruct((lp, r), BF16),
        scratch_shapes=[pltpu.VMEM((lp, S2_TQ), I32), pltpu.VMEM((8, S2_TQ), I32)],
        compiler_params=_cparams(("parallel",)),
        name="sample_select",
    )(scores_t)


S3_GROUP = 4
S3_SLOTS = 3
S3_SEQS = LANES // 8
S3_PITCH = 24
S3_PAGE_ROWS = PAGE_SIZE * S3_PITCH


def _s3_body(pt_ref, qbd_ref, mask_ref, e_ref, knew_ref, vnew_ref, tsp_ref, tsn_ref, ck_hbm, cv_hbm, o_ref,
             buf_ref, sem, lg_ref, acc_ref, stage_ref, vpad_ref, *, n_seq, n_pages, t):
    ng = n_pages // S3_GROUP
    steps = 2 * ng
    b = pl.program_id(0)
    g = pl.program_id(1)
    st = b * steps + g
    past = n_pages * PAGE_SIZE
    rows_g = S3_GROUP * PAGE_SIZE
    buf3 = buf_ref.reshape(S3_SLOTS * S3_GROUP * PAGE_SIZE, S3_PITCH, HEAD_DIM)

    def page_copy(cache_hbm, pid, slot, j):
        dst = buf3.at[pl.ds((slot * S3_GROUP + j) * PAGE_SIZE, PAGE_SIZE), pl.ds(0, N_HEADS), :]
        return pltpu.make_async_copy(cache_hbm.at[pid], dst, sem.at[slot, j])

    def fetch(ahead):
        g2 = g + ahead
        b2 = b + g2 // steps
        g2 = g2 % steps
        slot2 = (st + ahead) % S3_SLOTS

        @pl.when((b2 < n_seq) & (g2 < ng))
        def _():
            for j in range(S3_GROUP):
                page_copy(ck_hbm, pt_ref[b2, g2 * S3_GROUP + j], slot2, j).start()

        @pl.when((b2 < n_seq) & (g2 >= ng))
        def _():
            for j in range(S3_GROUP):
                page_copy(cv_hbm, pt_ref[b2, (g2 - ng) * S3_GROUP + j], slot2, j).start()

    @pl.when(st == 0)
    def _():
        for ahead in range(S3_SLOTS - 1):
            fetch(ahead)

    fetch(S3_SLOTS - 1)
    slot = st % S3_SLOTS
    for j in range(S3_GROUP):
        page_copy(ck_hbm, 0, slot, j).wait()

    def stage():
        for j in range(S3_GROUP):
            base = (slot * S3_GROUP + j) * S3_PAGE_ROWS
            for h in range(N_HEADS):
                stage_ref[j * PAGE_SIZE:(j + 1) * PAGE_SIZE, h * HEAD_DIM:(h + 1) * HEAD_DIM] = (
                    buf_ref[pl.ds(base + h, PAGE_SIZE, stride=S3_PITCH), :].astype(BF16))

    def masked(lg, msk_rows):
        sel = jnp.dot(msk_rows, e_ref[0], preferred_element_type=F32)
        return jnp.where(sel > 0.5, lg, NEG)

    @pl.when(g < ng)
    def _():
        stage()
        row0 = pl.multiple_of(g * rows_g, rows_g)
        lg = jnp.dot(stage_ref[...], qbd_ref[0], preferred_element_type=F32) * ATTN_SCALE
        lg_ref[pl.ds(row0, rows_g), :] = masked(lg, mask_ref[pl.ds(row0, rows_g), :])

    @pl.when(g == ng - 1)
    def _():
        lg_ref[past - PAGE_SIZE:past, :] += tsp_ref[...]
        lgn = jnp.dot(knew_ref[0].astype(BF16), qbd_ref[0], preferred_element_type=F32) * ATTN_SCALE + tsn_ref[...]
        lgn = jnp.concatenate([lgn, jnp.zeros((t, LANES), F32)], axis=0)
        lg_ref[past:past + t, :] = masked(lgn, mask_ref[past:past + 2 * t, :])[:t]
        lg_ref[past + t:, :] = jnp.full((PAGE_SIZE - t, LANES), NEG, F32)

    @pl.when(g == ng)
    def _():
        lg = lg_ref[...]
        p = jnp.exp(lg - jnp.max(lg, axis=0, keepdims=True))
        lg_ref[...] = p / jnp.sum(p, axis=0, keepdims=True)
        acc_ref[...] = jnp.zeros_like(acc_ref)

    def probs_t(row0, npages):
        return jnp.concatenate([jnp.transpose(lg_ref[pl.ds(row0 + j * PAGE_SIZE, PAGE_SIZE), :])
                                for j in range(npages)], axis=1).astype(BF16)

    @pl.when(g >= ng)
    def _():
        stage()
        row0 = pl.multiple_of((g - ng) * rows_g, rows_g)
        acc_ref[...] += jnp.dot(probs_t(row0, S3_GROUP), stage_ref[...], preferred_element_type=F32)

    @pl.when(g == steps - 1)
    def _():
        vpad_ref[...] = jnp.zeros_like(vpad_ref)
        vpad_ref[0:t, :] = vnew_ref[0]
        acc = acc_ref[...] + jnp.dot(probs_t(past, 1), vpad_ref[...].astype(BF16), preferred_element_type=F32)
        for h in range(N_HEADS):
            o_ref[0, :, h * HEAD_DIM:(h + 1) * HEAD_DIM] = acc[h * t:(h + 1) * t, h * HEAD_DIM:(h + 1) * HEAD_DIM]


def _sample_attention(qbd, mask_t, cache_k, cache_v, k_new, v_new, ts_page, ts_new, page_table):
    n, t, _ = k_new.shape
    n_pages = page_table.shape[1]
    ng = n_pages // S3_GROUP
    lp = mask_t.shape[0]
    assert t * S3_SEQS == LANES and t * N_HEADS == LANES
    sq = np.arange(LANES)
    expand = np.stack([(sq[:, None] // t == s) & (sq[:, None] % t == sq[None, :] % t) for s in range(S3_SEQS)])

    per_seq = lambda b, g, pt: (b, 0, 0)
    const2 = lambda b, g, pt: (0, 0)
    grid_spec = pltpu.PrefetchScalarGridSpec(
        num_scalar_prefetch=1,
        grid=(n, 2 * ng),
        in_specs=[pl.BlockSpec((1, ATTN_WIDTH, LANES), per_seq),
                  pl.BlockSpec((lp, LANES), lambda b, g, pt: (0, b // S3_SEQS)),
                  pl.BlockSpec((1, LANES, LANES), lambda b, g, pt: (b % S3_SEQS, 0, 0)),
                  pl.BlockSpec((1, t, ATTN_WIDTH), per_seq), pl.BlockSpec((1, t, ATTN_WIDTH), per_seq),
                  pl.BlockSpec((PAGE_SIZE, LANES), const2), pl.BlockSpec((t, LANES), const2),
                  pl.BlockSpec(memory_space=pl.ANY), pl.BlockSpec(memory_space=pl.ANY)],
        out_specs=pl.BlockSpec((1, t, ATTN_WIDTH), per_seq),
        scratch_shapes=[pltpu.VMEM((S3_SLOTS * S3_GROUP * S3_PAGE_ROWS, HEAD_DIM), F32),
                        pltpu.SemaphoreType.DMA((S3_SLOTS, S3_GROUP)),
                        pltpu.VMEM((lp, LANES), F32), pltpu.VMEM((LANES, ATTN_WIDTH), F32),
                        pltpu.VMEM((S3_GROUP * PAGE_SIZE, ATTN_WIDTH), BF16),
                        pltpu.VMEM((PAGE_SIZE, ATTN_WIDTH), F32)],
    )
    return pl.pallas_call(
        functools.partial(_s3_body, n_seq=n, n_pages=n_pages, t=t),
        grid_spec=grid_spec,
        out_shape=jax.ShapeDtypeStruct((n, t, ATTN_WIDTH), F32),
        compiler_params=_cparams(("arbitrary", "arbitrary")),
        name="sample_attention",
    )(page_table, qbd, mask_t, jnp.asarray(expand, BF16), k_new, v_new, ts_page, ts_new, cache_k, cache_v)


W_IN_SPLITS = (POOL_WIDTH, ATTN_WIDTH, ATTN_WIDTH, ATTN_WIDTH, IDX_HEADS * IDX_DIM, IDX_DIM, IDX_HEADS, D_MODEL, D_MODEL)
W_IN_OFFS = tuple(int(v) for v in np.cumsum((0,) + W_IN_SPLITS))


def _split_w_in(w):
    w16 = w.astype(BF16)
    return dict(all=w16, gp=w16[:, W_IN_OFFS[7]:W_IN_OFFS[8]], ga=w16[:, W_IN_OFFS[8]:W_IN_OFFS[9]])


def _project(x2d, ln1, wi):
    m = x2d.shape[0]
    tm = min(m, 1024)
    w, o = wi["all"], W_IN_OFFS
    xn = _rmsnorm(x2d, ln1, BF16)
    (u,) = _matmul(xn, w, [F32], tm=tm, tn=1024, name="proj_u", col0=o[0], n=POOL_WIDTH)
    (q,) = _matmul(xn, w, [BF16], tm=tm, tn=1024, name="proj_q", col0=o[1], n=ATTN_WIDTH)
    k, k16 = _matmul_heads(xn, w, tm=min(m, 512), tn=1024, name="proj_k", col0=o[2])
    v, v16 = _matmul_heads(xn, w, tm=min(m, 512), tn=1024, name="proj_v", col0=o[3])
    (qi,) = _matmul(xn, w, [BF16], tm=tm, tn=1024, name="proj_qi", col0=o[4], n=IDX_HEADS * IDX_DIM)
    (kiwi,) = _matmul(xn, w, [F32], tm=tm, tn=LANES, name="proj_kiwi", col0=o[5], n=LANES)
    ki = kiwi[:, :IDX_DIM]
    wgt = kiwi[:, IDX_DIM:IDX_DIM + IDX_HEADS]
    return xn, u, q, k, k16, v, v16, qi, ki, wgt


def _finish(x2d, xn, pool_o, attn_o, wi, w_bp, w_ba, w_out, ln2, w1, w2, ln_f):
    m = x2d.shape[0]
    tm = min(m, 1024)
    mg = _merge(xn, pool_o, attn_o, wi["gp"], wi["ga"], w_bp, w_ba)
    (x1,) = _matmul(mg, w_out, [F32], tm=tm, tn=1024, res=x2d, name="out_proj")
    hn = _rmsnorm(x1, ln2, BF16)
    (h,) = _matmul(hn, w1, [BF16], tm=tm, tn=1024, act="relu2", name="mlp_in")
    x2 = _matmul_ktiled(h, w2, x1, tm=tm, tn=1024, tk=2048, name="mlp_out")
    return _rmsnorm(x2, ln_f, F32)


def kernel(x_prompt, x_sample, cache_k, cache_v, cache_kidx, state_pool, page_table, ln1, w_in, pool_w,
           pool_scale, rel_bias, w_branch_pool, w_branch_attn, w_out, ln2, w_mlp_in, w_mlp_out, ln_f):
    bn, s, _ = x_prompt.shape
    n, t, _ = x_sample.shape
    depth = w_in.shape[0]
    assert depth == 1, "one layer: the caches and pooling state of layer 0 are the only ones read"
    n_pages = page_table.shape[1]
    n_phys = cache_k.shape[1]

    wi = _split_w_in(w_in[0])
    pw = pool_w[0].astype(BF16)
    w_bp = w_branch_pool[0].astype(BF16)
    w_ba = w_branch_attn[0].astype(BF16)
    wo = w_out[0].astype(BF16)
    w1 = w_mlp_in[0].astype(BF16)
    w2 = w_mlp_out[0].astype(BF16)
    tb_prompt, ts_page, ts_new = _bias_tables(rel_bias, t)

    xp = x_prompt.reshape(bn * s, D_MODEL)
    xn, u, q, k, k16, v, v16, qi, ki, wgt = _project(xp, ln1[0], wi)
    pool_o = _pool_prompt(u.reshape(bn, s, POOL_WIDTH), pw, pool_scale[0]).reshape(bn * s, POOL_WIDTH)
    ki_c = ki.astype(BF16).reshape(bn, s // P1_CK, P1_CK, IDX_DIM)
    zeros = jnp.zeros_like(ki_c)
    kbd = jnp.concatenate([jnp.concatenate([ki_c, zeros], axis=-1), jnp.concatenate([zeros, ki_c], axis=-1)],
                          axis=2).reshape(bn, 2 * s, 2 * IDX_DIM)
    amask = _prompt_select(qi.reshape(bn, s, IDX_HEADS * IDX_DIM), kbd,
                           jnp.swapaxes(wgt.reshape(bn, s, IDX_HEADS), 1, 2))
    attn_o = _prompt_attention(q.reshape(bn, s, ATTN_WIDTH), k16.reshape(bn, s, ATTN_WIDTH),
                               v16.reshape(bn, s, ATTN_WIDTH), amask, tb_prompt).reshape(bn * s, ATTN_WIDTH)
    y_prompt = _finish(xp, xn, pool_o, attn_o, wi, w_bp, w_ba, wo, ln2[0], w1, w2, ln_f).reshape(bn, s, D_MODEL)
    k_prompt = k.reshape(1, bn, s, N_HEADS, HEAD_DIM)
    v_prompt = v.reshape(1, bn, s, N_HEADS, HEAD_DIM)
    kidx_prompt = ki.reshape(1, bn, s, IDX_DIM)
    pool_prompt = u.reshape(bn, s, POOL_WIDTH)[:, s - POOL_PREFIX:, :][None]

    xs = x_sample.reshape(n * t, D_MODEL)
    xn, u, q, k, k16, v, v16, qi, ki, wgt = _project(xs, ln1[0], wi)
    u3 = u.reshape(n, t, POOL_WIDTH)
    pool_o = _pool_sample(u3, state_pool[0], pw, pool_scale[0])
    qi_hq = qi.reshape(n, t, IDX_HEADS, IDX_DIM).transpose(0, 2, 1, 3).reshape(n, IDX_HEADS * t, IDX_DIM)
    wi_hq = wgt.reshape(n, t, IDX_HEADS).transpose(0, 2, 1).reshape(n, IDX_HEADS * t, 1)
    ki_new_pad = jnp.pad(jnp.swapaxes(ki.reshape(n, t, IDX_DIM), 1, 2), ((0, 0), (0, 0), (0, PAGE_SIZE - t)))
    scores = _sample_scores(qi_hq, wi_hq, jnp.swapaxes(cache_kidx[0], 1, 2), ki_new_pad, page_table)
    lp = scores.shape[-1]
    mask_t = _sample_select(scores.reshape(n * t, lp).T, t)
    q_t = q.reshape(n, t, N_HEADS, HEAD_DIM).transpose(0, 2, 3, 1)
    eye = jnp.eye(N_HEADS, dtype=BF16)
    qbd = (q_t[:, :, :, None, :] * eye[None, :, None, :, None]).reshape(n, ATTN_WIDTH, N_HEADS * t)
    attn_o = _sample_attention(qbd, mask_t, cache_k[0], cache_v[0], k.reshape(n, t, ATTN_WIDTH),
                               v.reshape(n, t, ATTN_WIDTH), ts_page, ts_new, page_table)
    attn_o = attn_o.reshape(n * t, ATTN_WIDTH).astype(BF16)
    y_sample = _finish(xs, xn, pool_o, attn_o, wi, w_bp, w_ba, wo, ln2[0], w1, w2, ln_f).reshape(n, t, D_MODEL)
    k_sample = k.reshape(1, n, t, N_HEADS, HEAD_DIM)
    v_sample = v.reshape(1, n, t, N_HEADS, HEAD_DIM)
    kidx_sample = ki.reshape(1, n, t, IDX_DIM)
    pool_sample = jnp.concatenate([state_pool[0][:, t:, :], u3], axis=1)[None]

    return (y_prompt, y_sample, k_prompt, v_prompt, kidx_prompt, pool_prompt,
            k_sample, v_sample, kidx_sample, pool_sample)
```

```python
import functools
import math

import numpy as np
import jax
import jax.numpy as jnp
from jax import lax
from jax.experimental import pallas as pl
from jax.experimental.pallas import tpu as pltpu

D_MODEL = 4096
PAST_LEN = 2048
PAGE_SIZE = 128
POOL_WIDTH = D_MODEL // 2
POOL_WINDOWS = (2, 4, 8, 16)
N_POOL_GROUPS = len(POOL_WINDOWS)
POOL_GROUP = POOL_WIDTH // N_POOL_GROUPS
POOL_PREFIX = max(POOL_WINDOWS) - 1
HEAD_DIM = 128
N_HEADS = (D_MODEL // 2) // HEAD_DIM
ATTN_WIDTH = N_HEADS * HEAD_DIM
IDX_HEADS = 32
IDX_DIM = 64
TOPK_MAX = 256
NUM_BUCKETS = 32
MAX_DISTANCE = 128
D_FF = 4 * D_MODEL
EPS = 1e-6

F32 = jnp.float32
BF16 = jnp.bfloat16
I32 = jnp.int32

LANES = 128
VMEM_LIMIT = 56 * 1024 * 1024
INT_MIN = -(2 ** 31)
NEG = -1e30
HALO = 16
PROMPT_HALO = 128
IDX_W_SCALE = (IDX_HEADS ** -0.5) * (IDX_DIM ** -0.5)
ATTN_SCALE = HEAD_DIM ** -0.5
LOG2E = math.log2(math.e)


def _cparams(sem):
    return pltpu.CompilerParams(dimension_semantics=sem, vmem_limit_bytes=VMEM_LIMIT)


def _rmsnorm_body(x_ref, g_ref, o_ref):
    x = x_ref[...]
    y = x * lax.rsqrt(jnp.mean(x * x, axis=-1, keepdims=True) + EPS)
    o_ref[...] = (y * g_ref[...]).astype(o_ref.dtype)


def _rmsnorm(x, g, out_dtype, tm=512):
    m, d = x.shape
    return pl.pallas_call(
        _rmsnorm_body,
        grid=(m // tm,),
        in_specs=[pl.BlockSpec((tm, d), lambda i: (i, 0)), pl.BlockSpec((1, d), lambda i: (0, 0))],
        out_specs=pl.BlockSpec((tm, d), lambda i: (i, 0)),
        out_shape=jax.ShapeDtypeStruct((m, d), out_dtype),
        compiler_params=_cparams(("parallel",)),
        name="rmsnorm",
    )(x, g.reshape(1, d))


def _mm_body(*refs, n_out, act, has_res):
    x_ref, w_ref = refs[0], refs[1]
    res_ref = refs[2] if has_res else None
    outs = refs[2 + int(has_res):2 + int(has_res) + n_out]
    acc = jnp.dot(x_ref[...], w_ref[...], preferred_element_type=F32)
    if act == "relu2":
        acc = jnp.square(jnp.maximum(acc, 0.0))
    if has_res:
        acc = res_ref[...] + acc
    for o in outs:
        o[...] = acc.astype(o.dtype)


def _matmul(x, w, out_dtypes, *, tm, tn, act=None, res=None, name="matmul", col0=0, n=None):
    m, k = x.shape
    n = w.shape[1] if n is None else n
    assert col0 % tn == 0
    cb = col0 // tn
    in_specs = [pl.BlockSpec((tm, k), lambda i, j: (i, 0)), pl.BlockSpec((k, tn), lambda i, j: (0, cb + j))]
    args = [x, w]
    if res is not None:
        in_specs.append(pl.BlockSpec((tm, tn), lambda i, j: (i, j)))
        args.append(res)
    outs = pl.pallas_call(
        functools.partial(_mm_body, n_out=len(out_dtypes), act=act, has_res=res is not None),
        grid=(m // tm, n // tn),
        in_specs=in_specs,
        out_specs=[pl.BlockSpec((tm, tn), lambda i, j: (i, j)) for _ in out_dtypes],
        out_shape=[jax.ShapeDtypeStruct((m, n), dt) for dt in out_dtypes],
        compiler_params=_cparams(("parallel", "parallel")),
        name=name,
    )(*args)
    return outs


def _mmh_body(x_ref, w_ref, o32_ref, o16_ref, *, heads):
    j = pl.program_id(1)
    acc = jnp.dot(x_ref[...], w_ref[...], preferred_element_type=F32)
    o16_ref[...] = acc.astype(BF16)
    tm = acc.shape[0]
    for hh in range(heads):
        o32_ref[pl.ds(j * heads + hh, tm, stride=N_HEADS), :] = acc[:, hh * HEAD_DIM:(hh + 1) * HEAD_DIM]


def _matmul_heads(x, w, *, tm, tn, name, col0=0):
    m, k = x.shape
    n = N_HEADS * HEAD_DIM
    assert col0 % tn == 0
    cb = col0 // tn
    return pl.pallas_call(
        functools.partial(_mmh_body, heads=tn // HEAD_DIM),
        grid=(m // tm, n // tn),
        in_specs=[pl.BlockSpec((tm, k), lambda i, j: (i, 0)), pl.BlockSpec((k, tn), lambda i, j: (0, cb + j))],
        out_specs=[pl.BlockSpec((tm * N_HEADS, HEAD_DIM), lambda i, j: (i, 0)),
                   pl.BlockSpec((tm, tn), lambda i, j: (i, j))],
        out_shape=[jax.ShapeDtypeStruct((m * N_HEADS, HEAD_DIM), F32), jax.ShapeDtypeStruct((m, n), BF16)],
        compiler_params=_cparams(("parallel", "arbitrary")),
        name=name,
    )(x, w)


def _mmk_body(x_ref, w_ref, res_ref, o_ref, acc_ref, *, nk):
    k = pl.program_id(2)

    @pl.when(k == 0)
    def _():
        acc_ref[...] = jnp.zeros_like(acc_ref)

    acc_ref[...] += jnp.dot(x_ref[...], w_ref[...], preferred_element_type=F32)

    @pl.when(k == nk - 1)
    def _():
        o_ref[...] = res_ref[...] + acc_ref[...]


def _matmul_ktiled(x, w, res, *, tm, tn, tk, name):
    m, k = x.shape
    n = w.shape[1]
    nk = k // tk
    return pl.pallas_call(
        functools.partial(_mmk_body, nk=nk),
        grid=(m // tm, n // tn, nk),
        in_specs=[pl.BlockSpec((tm, tk), lambda i, j, kk: (i, kk)),
                  pl.BlockSpec((tk, tn), lambda i, j, kk: (kk, j)),
                  pl.BlockSpec((tm, tn), lambda i, j, kk: (i, j))],
        out_specs=pl.BlockSpec((tm, tn), lambda i, j, kk: (i, j)),
        out_shape=jax.ShapeDtypeStruct((m, n), F32),
        scratch_shapes=[pltpu.VMEM((tm, tn), F32)],
        compiler_params=_cparams(("parallel", "parallel", "arbitrary")),
        name=name,
    )(x, w, res)


def _merge_body(xn_ref, po_ref, ao_ref, wgp_ref, wga_ref, wbp_ref, wba_ref, o_ref):
    xn = xn_ref[...]
    gp = jnp.dot(xn, wgp_ref[...], preferred_element_type=F32)
    ga = jnp.dot(xn, wga_ref[...], preferred_element_type=F32)
    a = jnp.dot(po_ref[...], wbp_ref[...], preferred_element_type=F32)
    b = jnp.dot(ao_ref[...], wba_ref[...], preferred_element_type=F32)
    o_ref[...] = (jax.nn.sigmoid(gp) * a + jax.nn.sigmoid(ga) * b).astype(o_ref.dtype)


def _merge(xn, po, ao, wgp, wga, wbp, wba, *, tm=512, tn=512):
    m, d = xn.shape
    kb = po.shape[1]
    n = wgp.shape[1]
    row = lambda i, j: (i, 0)
    col = lambda i, j: (0, j)
    return pl.pallas_call(
        _merge_body,
        grid=(m // tm, n // tn),
        in_specs=[pl.BlockSpec((tm, d), row), pl.BlockSpec((tm, kb), row), pl.BlockSpec((tm, kb), row),
                  pl.BlockSpec((d, tn), col), pl.BlockSpec((d, tn), col),
                  pl.BlockSpec((kb, tn), col), pl.BlockSpec((kb, tn), col)],
        out_specs=pl.BlockSpec((tm, tn), lambda i, j: (i, j)),
        out_shape=jax.ShapeDtypeStruct((m, n), BF16),
        compiler_params=_cparams(("parallel", "parallel")),
        name="merge",
    )(xn, po, ao, wgp, wga, wbp, wba)


def _pool_body(*refs, has_halo):
    if has_halo:
        halo_ref, cur_ref, s_ref, inv_ref, pw_ref, sc_ref, o_ref = refs
        halo = jnp.where(pl.program_id(1) == 0, 0.0, halo_ref[0])
        cur = cur_ref[0]
        ext = jnp.concatenate([halo, cur], axis=0)
    else:
        ext_ref, cur_ref, s_ref, inv_ref, pw_ref, sc_ref, o_ref = refs
        ext = ext_ref[...]
        cur = cur_ref[...]
    hi = ext.astype(BF16)
    lo = (ext - hi.astype(F32)).astype(BF16)
    band = s_ref[0]
    win = (jnp.dot(band, hi, preferred_element_type=F32) + jnp.dot(band, lo, preferred_element_type=F32))
    pooled = win * inv_ref[0] - cur
    y = jnp.dot(pooled.astype(BF16), pw_ref[0], preferred_element_type=F32) * sc_ref[...]
    if has_halo:
        o_ref[0] = y.astype(o_ref.dtype)
    else:
        o_ref[...] = y.astype(o_ref.dtype)


def _pool_prompt(u, pool_w, pool_scale, *, tb=512):
    n, t, _ = u.shape
    halo = PROMPT_HALO
    band = np.zeros((N_POOL_GROUPS, tb, halo + tb), np.float32)
    inv = np.zeros((N_POOL_GROUPS, t, 1), np.float32)
    r = np.arange(tb)[:, None]
    c = np.arange(halo + tb)[None, :]
    for g, w in enumerate(POOL_WINDOWS):
        band[g] = ((c >= r + halo - w + 1) & (c <= r + halo)).astype(np.float32)
        inv[g, :, 0] = 1.0 / np.minimum(w, np.arange(t) + 1)
    hb = tb // halo
    return pl.pallas_call(
        functools.partial(_pool_body, has_halo=True),
        grid=(n, t // tb, N_POOL_GROUPS),
        in_specs=[pl.BlockSpec((1, halo, POOL_GROUP), lambda b, i, g: (b, jnp.maximum(i * hb - 1, 0), g)),
                  pl.BlockSpec((1, tb, POOL_GROUP), lambda b, i, g: (b, i, g)),
                  pl.BlockSpec((1, tb, halo + tb), lambda b, i, g: (g, 0, 0)),
                  pl.BlockSpec((1, tb, 1), lambda b, i, g: (g, i, 0)),
                  pl.BlockSpec((1, POOL_GROUP, POOL_GROUP), lambda b, i, g: (g, 0, 0)),
                  pl.BlockSpec((1, POOL_GROUP), lambda b, i, g: (0, g))],
        out_specs=pl.BlockSpec((1, tb, POOL_GROUP), lambda b, i, g: (b, i, g)),
        out_shape=jax.ShapeDtypeStruct((n, t, POOL_WIDTH), BF16),
        compiler_params=_cparams(("parallel", "arbitrary", "arbitrary")),
        name="pool_prompt",
    )(u, u, jnp.asarray(band, BF16), jnp.asarray(inv), pool_w, pool_scale.reshape(1, POOL_WIDTH))


def _pool_sample(u, prefix, pool_w, pool_scale, *, sb=16):
    n, t, _ = u.shape
    e = HALO + t
    ext = jnp.concatenate([jnp.zeros((n, HALO - POOL_PREFIX, POOL_WIDTH), F32), prefix.astype(F32), u], axis=1)
    ext = ext.reshape(n * e, POOL_WIDTH)
    band = np.zeros((N_POOL_GROUPS, sb * t, sb * e), np.float32)
    inv = np.zeros((N_POOL_GROUPS, sb * t, 1), np.float32)
    r = np.arange(sb * t)[:, None]
    c = np.arange(sb * e)[None, :]
    for g, w in enumerate(POOL_WINDOWS):
        pos = HALO + r % t
        band[g] = ((r // t == c // e) & (c % e >= pos - w + 1) & (c % e <= pos)).astype(np.float32)
        inv[g] = 1.0 / w
    out = pl.pallas_call(
        functools.partial(_pool_body, has_halo=False),
        grid=(n // sb, N_POOL_GROUPS),
        in_specs=[pl.BlockSpec((sb * e, POOL_GROUP), lambda i, g: (i, g)),
                  pl.BlockSpec((sb * t, POOL_GROUP), lambda i, g: (i, g)),
                  pl.BlockSpec((1, sb * t, sb * e), lambda i, g: (g, 0, 0)),
                  pl.BlockSpec((1, sb * t, 1), lambda i, g: (g, 0, 0)),
                  pl.BlockSpec((1, POOL_GROUP, POOL_GROUP), lambda i, g: (g, 0, 0)),
                  pl.BlockSpec((1, POOL_GROUP), lambda i, g: (0, g))],
        out_specs=pl.BlockSpec((sb * t, POOL_GROUP), lambda i, g: (i, g)),
        out_shape=jax.ShapeDtypeStruct((n * t, POOL_WIDTH), BF16),
        compiler_params=_cparams(("parallel", "arbitrary")),
        name="pool_sample",
    )(ext, u.reshape(n * t, POOL_WIDTH), jnp.asarray(band, BF16), jnp.asarray(inv), pool_w,
      pool_scale.reshape(1, POOL_WIDTH))
    return out


def _bucket_thresholds():
    d = np.arange(0, 4 * MAX_DISTANCE)
    max_exact = NUM_BUCKETS // 2
    df = np.maximum(d, 1).astype(np.float32)
    large = max_exact + (np.log(df / np.float32(max_exact)) / np.float32(math.log(MAX_DISTANCE / max_exact))
                         * np.float32(NUM_BUCKETS - max_exact)).astype(np.int32)
    bucket = np.where(d < max_exact, d, np.minimum(large, NUM_BUCKETS - 1))
    assert np.all(np.diff(bucket) >= 0) and np.all(bucket[MAX_DISTANCE + 1:] == NUM_BUCKETS - 1)
    return [int(np.argmax(bucket >= b)) for b in range(NUM_BUCKETS)]


def _bias_prompt_body(rb_ref, o_ref):
    h = pl.program_id(0)
    thr = _bucket_thresholds()
    base = lax.broadcasted_iota(I32, (LANES, LANES), 0) - lax.broadcasted_iota(I32, (LANES, LANES), 1)
    for j in range(2):
        dist = base + j * LANES
        tile = jnp.full((LANES, LANES), rb_ref[0, h], F32)
        for b in range(1, NUM_BUCKETS):
            tile = jnp.where(dist >= thr[b], rb_ref[b, h], tile)
        o_ref[0, j] = (tile - rb_ref[NUM_BUCKETS - 1, h]) * LOG2E


def _bias_sample_body(rbx_ref, page_ref, new_ref, *, t):
    thr = _bucket_thresholds()
    for o_ref, rows, off in ((page_ref, PAGE_SIZE, PAGE_SIZE), (new_ref, t, 0)):
        qq = lax.broadcasted_iota(I32, (rows, LANES), 1) % t
        dist = off + qq - lax.broadcasted_iota(I32, (rows, LANES), 0)
        tile = jnp.broadcast_to(rbx_ref[0:1, :], (rows, LANES))
        for b in range(1, NUM_BUCKETS):
            tile = jnp.where(dist >= thr[b], rbx_ref[b:b + 1, :], tile)
        o_ref[...] = tile - rbx_ref[NUM_BUCKETS - 1:NUM_BUCKETS, :]


def _bias_tables(rel_bias, t):
    rb = rel_bias.astype(F32)
    tp = pl.pallas_call(
        _bias_prompt_body,
        grid=(N_HEADS,),
        in_specs=[pl.BlockSpec(memory_space=pltpu.SMEM)],
        out_specs=pl.BlockSpec((1, 2, LANES, LANES), lambda h: (h, 0, 0, 0)),
        out_shape=jax.ShapeDtypeStruct((N_HEADS, 2, LANES, LANES), F32),
        name="bias_prompt",
    )(rb)
    rbx = jnp.repeat(rb, LANES // N_HEADS, axis=1)
    ts_page, ts_new = pl.pallas_call(
        functools.partial(_bias_sample_body, t=t),
        out_shape=[jax.ShapeDtypeStruct((PAGE_SIZE, LANES), F32), jax.ShapeDtypeStruct((t, LANES), F32)],
        name="bias_sample",
    )(rbx)
    return tp, ts_page, ts_new


SEL_RB = 128


def _sortable(score):
    bits = lax.bitcast_convert_type(score, I32)
    return bits ^ ((bits >> 31) & 0x7FFFFFFF)


def _select_topk(key_ref, jcut_ref, krow, nblk, tq, blocks_per_iter=1):
    def counts(pred):
        def body(r, acc):
            for u in range(blocks_per_iter):
                row0 = pl.multiple_of((r * blocks_per_iter + u) * SEL_RB, SEL_RB)
                hit = pred(key_ref[pl.ds(row0, SEL_RB), :], row0).astype(I32)
                acc = acc + jnp.sum(hit.reshape(SEL_RB // 8, 8, tq), axis=0)
            return acc
        acc = lax.fori_loop(0, nblk // blocks_per_iter, body, jnp.zeros((8, tq), I32))
        return jnp.sum(acc, axis=0, keepdims=True)

    def count_ge(cand):
        return counts(lambda blk, row0: blk >= cand)

    zero = jnp.zeros((1, tq), I32)
    v = jnp.where(count_ge(zero) >= krow, zero, jnp.full((1, tq), INT_MIN, I32))

    def bit_body(bi, v):
        cand = v | jnp.left_shift(jnp.int32(1), 30 - bi)
        return jnp.where(count_ge(cand) >= krow, cand, v)

    v = lax.fori_loop(0, 31, bit_body, v)

    n_gt = counts(lambda blk, row0: blk > v)
    n_ge = counts(lambda blk, row0: blk >= v)
    need = krow - n_gt
    jcut_ref[...] = jnp.full((8, tq), nblk * SEL_RB, I32)

    @pl.when(jnp.max(n_ge - krow) > 0)
    def _():
        def jbit(bi, x):
            cand = x + jnp.left_shift(jnp.int32(1), 15 - bi)

            def pred(blk, row0):
                row = row0 + lax.broadcasted_iota(I32, (SEL_RB, tq), 0)
                return (blk == v) & (row <= cand)
            return jnp.where(counts(pred) < need, cand, x)
        x = lax.fori_loop(0, 16, jbit, jnp.full((1, tq), -1, I32))
        jcut_ref[...] = jnp.broadcast_to(x + 2, (8, tq))

    return v


def _selected(key_ref, jcut_ref, v, row0, tq):
    blk = key_ref[pl.ds(row0, SEL_RB), :]
    row = row0 + lax.broadcasted_iota(I32, (SEL_RB, tq), 0)
    return (blk > v) | ((blk == v) & (row < jcut_ref[0:1, :]))


P1_TQ = 256
P1_CK = SEL_RB


def _p1_body(qi_ref, kbd_ref, wt_ref, amask_ref, key_ref, jcut_ref, *, s):
    i = pl.program_id(1)
    t0 = i * P1_TQ
    n_ch = (i + 1) * (P1_TQ // P1_CK)
    tq_iota = t0 + lax.broadcasted_iota(I32, (P1_CK, P1_TQ), 1)

    def chunk_body(c, _):
        kb = kbd_ref[0, pl.ds(pl.multiple_of(c * 2 * P1_CK, 2 * P1_CK), 2 * P1_CK), :]
        acc = jnp.zeros((P1_CK, P1_TQ), F32)
        for hp in range(IDX_HEADS // 2):
            d2 = lax.dot_general(kb, qi_ref[0, :, hp * LANES:(hp + 1) * LANES], (((1,), (1,)), ((), ())),
                                 preferred_element_type=F32)
            w0 = wt_ref[0, 2 * hp:2 * hp + 1, :] * IDX_W_SCALE
            w1 = wt_ref[0, 2 * hp + 1:2 * hp + 2, :] * IDX_W_SCALE
            acc = acc + w0 * jnp.maximum(d2[:P1_CK], 0.0) + w1 * jnp.maximum(d2[P1_CK:], 0.0)
        row0 = pl.multiple_of(c * P1_CK, P1_CK)
        srow = row0 + lax.broadcasted_iota(I32, (P1_CK, P1_TQ), 0)
        key_ref[pl.ds(row0, P1_CK), :] = jnp.where(srow <= tq_iota, _sortable(acc), INT_MIN)
        return 0

    lax.fori_loop(0, n_ch, chunk_body, 0)

    krow = jnp.minimum(TOPK_MAX, t0 + lax.broadcasted_iota(I32, (1, P1_TQ), 1) + 1)
    v = _select_topk(key_ref, jcut_ref, krow, n_ch, P1_TQ, blocks_per_iter=P1_TQ // P1_CK)

    def out_body(r, _):
        row0 = pl.multiple_of(r * SEL_RB, SEL_RB)
        am = jnp.where(_selected(key_ref, jcut_ref, v, row0, P1_TQ), 0.0, NEG)
        for h in range(P1_TQ // LANES):
            amask_ref[0, h * LANES:(h + 1) * LANES, pl.ds(row0, SEL_RB)] = jnp.transpose(
                am[:, h * LANES:(h + 1) * LANES])
        return 0

    lax.fori_loop(0, n_ch, out_body, 0)

    def fill_body(r, _):
        amask_ref[0, :, pl.ds(pl.multiple_of(r * SEL_RB, SEL_RB), SEL_RB)] = jnp.full((P1_TQ, SEL_RB), NEG, F32)
        return 0

    lax.fori_loop(n_ch, s // SEL_RB, fill_body, 0)


def _prompt_select(qi, kbd, wt):
    b, s, _ = qi.shape
    return pl.pallas_call(
        functools.partial(_p1_body, s=s),
        grid=(b, s // P1_TQ),
        in_specs=[pl.BlockSpec((1, P1_TQ, IDX_HEADS * IDX_DIM), lambda bb, i: (bb, i, 0)),
                  pl.BlockSpec((1, 2 * s, 2 * IDX_DIM), lambda bb, i: (bb, 0, 0)),
                  pl.BlockSpec((1, IDX_HEADS, P1_TQ), lambda bb, i: (bb, 0, i))],
        out_specs=pl.BlockSpec((1, P1_TQ, s), lambda bb, i: (bb, i, 0)),
        out_shape=jax.ShapeDtypeStruct((b, s, s), F32),
        scratch_shapes=[pltpu.VMEM((s, P1_TQ), I32), pltpu.VMEM((8, P1_TQ), I32)],
        compiler_params=_cparams(("parallel", "arbitrary")),
        name="prompt_select",
    )(qi, kbd, wt)


P2_TQ = 512
P2_CK = 1024
P2_HEADS = 2


def _p2_body(q_ref, k_ref, v_ref, amask_ref, tb_ref, o_ref, *lg_refs, nc, i0):
    i = i0 + pl.program_id(1)
    tq = P2_TQ
    chunks = [(c * P2_CK, (c + 1) * P2_CK) for c in range(nc)]

    def lane_fold(x, op):
        out = x[:, :LANES]
        for j in range(1, x.shape[1] // LANES):
            out = op(out, x[:, j * LANES:(j + 1) * LANES])
        return out

    for hh, lg_ref in enumerate(lg_refs):
        hd = slice(hh * HEAD_DIM, (hh + 1) * HEAD_DIM)
        q = q_ref[0, :, hd]
        for lo, hi in chunks:
            sc = lax.dot_general(q, k_ref[0, lo:hi, hd], (((1,), (1,)), ((), ())), preferred_element_type=F32)
            lg_ref[:, lo:hi] = sc * (ATTN_SCALE * LOG2E) + amask_ref[0, :, lo:hi]

        for a in range(tq // LANES):
            rows = slice(a * LANES, (a + 1) * LANES)
            dcol = pl.multiple_of(i * tq + a * LANES, LANES)
            lg_ref[rows, pl.ds(dcol, LANES)] += tb_ref[hh, 0]
            if a > 0:
                scol = pl.multiple_of(i * tq + (a - 1) * LANES, LANES)
                lg_ref[rows, pl.ds(scol, LANES)] += tb_ref[hh, 1]

        @pl.when(i > 0)
        def _():
            scol = pl.multiple_of(i * tq - LANES, LANES)
            lg_ref[0:LANES, pl.ds(scol, LANES)] += tb_ref[hh, 1]

    for hh, lg_ref in enumerate(lg_refs):
        hd = slice(hh * HEAD_DIM, (hh + 1) * HEAD_DIM)
        m = jnp.full((tq, LANES), NEG, F32)
        for lo, hi in chunks:
            m = jnp.maximum(m, lane_fold(lg_ref[:, lo:hi], jnp.maximum))
        m = jnp.max(m, axis=1, keepdims=True)
        l = jnp.zeros((tq, LANES), F32)
        acc = jnp.zeros((tq, HEAD_DIM), F32)
        for lo, hi in chunks:
            p = jnp.exp2(lg_ref[:, lo:hi] - m)
            l = l + lane_fold(p, jnp.add)
            acc = acc + jnp.dot(p.astype(BF16), v_ref[0, lo:hi, hd], preferred_element_type=F32)
        o_ref[0, :, hd] = (acc / jnp.sum(l, axis=1, keepdims=True)).astype(o_ref.dtype)


def _prompt_attention(q, k, v, amask, tb):
    b, s, _ = q.shape
    wd = P2_HEADS * HEAD_DIM
    per_call = P2_CK // P2_TQ
    outs = []
    for nc in range(1, s // P2_CK + 1):
        i0 = (nc - 1) * per_call
        outs.append(pl.pallas_call(
            functools.partial(_p2_body, nc=nc, i0=i0),
            grid=(b, per_call, N_HEADS // P2_HEADS),
            in_specs=[pl.BlockSpec((1, P2_TQ, wd), lambda bb, i, h, i0=i0: (bb, i0 + i, h)),
                      pl.BlockSpec((1, nc * P2_CK, wd), lambda bb, i, h: (bb, 0, h)),
                      pl.BlockSpec((1, nc * P2_CK, wd), lambda bb, i, h: (bb, 0, h)),
                      pl.BlockSpec((1, P2_TQ, nc * P2_CK), lambda bb, i, h, i0=i0: (bb, i0 + i, 0)),
                      pl.BlockSpec((P2_HEADS, 2, LANES, LANES), lambda bb, i, h: (h, 0, 0, 0))],
            out_specs=pl.BlockSpec((1, P2_TQ, wd), lambda bb, i, h: (bb, i, h)),
            out_shape=jax.ShapeDtypeStruct((b, per_call * P2_TQ, ATTN_WIDTH), BF16),
            scratch_shapes=[pltpu.VMEM((P2_TQ, nc * P2_CK), F32) for _ in range(P2_HEADS)],
            compiler_params=_cparams(("parallel", "parallel", "arbitrary")),
            name=f"prompt_attention_{nc}",
        )(q, k, v, amask, tb))
    return jnp.concatenate(outs, axis=1)


def _s1_body(pt_ref, qi_ref, w_ref, *refs, n_pages, t):
    page_refs = refs[:n_pages + 1]
    o_ref = refs[n_pages + 1]
    qi = qi_ref[0]
    w = jnp.broadcast_to(w_ref[0] * IDX_W_SCALE, (IDX_HEADS * t, LANES))
    for p, kp_ref in enumerate(page_refs):
        d = jnp.dot(qi, kp_ref[0].astype(BF16), preferred_element_type=F32)
        wr = (w * jnp.maximum(d, 0.0)).reshape(IDX_HEADS, t, LANES)
        o_ref[0, :, p * LANES:(p + 1) * LANES] = jnp.sum(wr, axis=0)


def _sample_scores(qi_hq, wi_hq, cache_kidx, ki_new_pad, page_table):
    n, rows, _ = qi_hq.shape
    t = rows // IDX_HEADS
    n_pages = page_table.shape[1]
    lp = (n_pages + 1) * PAGE_SIZE

    def page_spec(p):
        return pl.BlockSpec((1, IDX_DIM, PAGE_SIZE), lambda b, pt: (pt[b, p], 0, 0))

    grid_spec = pltpu.PrefetchScalarGridSpec(
        num_scalar_prefetch=1,
        grid=(n,),
        in_specs=[pl.BlockSpec((1, rows, IDX_DIM), lambda b, pt: (b, 0, 0)),
                  pl.BlockSpec((1, rows, 1), lambda b, pt: (b, 0, 0))]
                 + [page_spec(p) for p in range(n_pages)]
                 + [pl.BlockSpec((1, IDX_DIM, PAGE_SIZE), lambda b, pt: (b, 0, 0))],
        out_specs=pl.BlockSpec((1, t, lp), lambda b, pt: (b, 0, 0)),
    )
    return pl.pallas_call(
        functools.partial(_s1_body, n_pages=n_pages, t=t),
        grid_spec=grid_spec,
        out_shape=jax.ShapeDtypeStruct((n, t, lp), F32),
        compiler_params=_cparams(("arbitrary",)),
        name="sample_scores",
    )(page_table, qi_hq, wi_hq, *([cache_kidx] * n_pages), ki_new_pad)


S2_TQ = 256


def _s2_body(sc_ref, mask_ref, key_ref, jcut_ref, *, t, lp):
    c0 = pl.program_id(0) * S2_TQ
    nblk = lp // SEL_RB
    lim = PAST_LEN + (c0 + lax.broadcasted_iota(I32, (SEL_RB, S2_TQ), 1)) % t + 1
    for r in range(nblk):
        rows = slice(r * SEL_RB, (r + 1) * SEL_RB)
        srow = r * SEL_RB + lax.broadcasted_iota(I32, (SEL_RB, S2_TQ), 0)
        key_ref[rows, :] = jnp.where(srow < lim, _sortable(sc_ref[rows, :]), INT_MIN)
    krow = jnp.full((1, S2_TQ), min(TOPK_MAX, (PAST_LEN + t) // 4), I32)
    v = _select_topk(key_ref, jcut_ref, krow, nblk, S2_TQ)
    for r in range(nblk):
        sel = _selected(key_ref, jcut_ref, v, r * SEL_RB, S2_TQ)
        mask_ref[r * SEL_RB:(r + 1) * SEL_RB, :] = jnp.where(sel, 1.0, 0.0).astype(BF16)


def _sample_select(scores_t, t):
    lp, r = scores_t.shape
    return pl.pallas_call(
        functools.partial(_s2_body, t=t, lp=lp),
        grid=(r // S2_TQ,),
        in_specs=[pl.BlockSpec((lp, S2_TQ), lambda i: (0, i))],
        out_specs=pl.BlockSpec((lp, S2_TQ), lambda i: (0, i)),
        out_shape=jax.ShapeDtypeStruct((lp, r), BF16),
        scratch_shapes=[pltpu.VMEM((lp, S2_TQ), I32), pltpu.VMEM((8, S2_TQ), I32)],
        compiler_params=_cparams(("parallel",)),
        name="sample_select",
    )(scores_t)


S3_GROUP = 4
S3_SLOTS = 4
S3_SEQS = LANES // 8
S3_PITCH = 24
S3_PAGE_ROWS = PAGE_SIZE * S3_PITCH


def _s3_body(pt_ref, qbd_ref, mask_ref, e_ref, knew_ref, vnew_ref, tsp_ref, tsn_ref, ck_hbm, cv_hbm, o_ref,
             buf_ref, sem, lg_ref, acc_ref, stage_ref, vpad_ref, *, n_seq, n_pages, t):
    ng = n_pages // S3_GROUP
    steps = 2 * ng
    b = pl.program_id(0)
    g = pl.program_id(1)
    st = b * steps + g
    past = n_pages * PAGE_SIZE
    rows_g = S3_GROUP * PAGE_SIZE
    buf3 = buf_ref.reshape(S3_SLOTS * S3_GROUP * PAGE_SIZE, S3_PITCH, HEAD_DIM)

    def page_copy(cache_hbm, pid, slot, j):
        dst = buf3.at[pl.ds((slot * S3_GROUP + j) * PAGE_SIZE, PAGE_SIZE), pl.ds(0, N_HEADS), :]
        return pltpu.make_async_copy(cache_hbm.at[pid], dst, sem.at[slot, j])

    def fetch(ahead):
        g2 = g + ahead
        b2 = b + g2 // steps
        g2 = g2 % steps
        slot2 = (st + ahead) % S3_SLOTS

        @pl.when((b2 < n_seq) & (g2 < ng))
        def _():
            for j in range(S3_GROUP):
                page_copy(ck_hbm, pt_ref[b2, g2 * S3_GROUP + j], slot2, j).start()

        @pl.when((b2 < n_seq) & (g2 >= ng))
        def _():
            for j in range(S3_GROUP):
                page_copy(cv_hbm, pt_ref[b2, (g2 - ng) * S3_GROUP + j], slot2, j).start()

    @pl.when(st == 0)
    def _():
        for ahead in range(S3_SLOTS - 1):
            fetch(ahead)

    fetch(S3_SLOTS - 1)
    slot = st % S3_SLOTS
    for j in range(S3_GROUP):
        page_copy(ck_hbm, 0, slot, j).wait()

    def stage():
        for j in range(S3_GROUP):
            base = (slot * S3_GROUP + j) * S3_PAGE_ROWS
            for h in range(N_HEADS):
                stage_ref[j * PAGE_SIZE:(j + 1) * PAGE_SIZE, h * HEAD_DIM:(h + 1) * HEAD_DIM] = (
                    buf_ref[pl.ds(base + h, PAGE_SIZE, stride=S3_PITCH), :].astype(BF16))

    def masked(lg, msk_rows):
        sel = jnp.dot(msk_rows, e_ref[0], preferred_element_type=F32)
        return jnp.where(sel > 0.5, lg, NEG)

    @pl.when(g < ng)
    def _():
        stage()
        row0 = pl.multiple_of(g * rows_g, rows_g)
        lg = jnp.dot(stage_ref[...], qbd_ref[0], preferred_element_type=F32) * ATTN_SCALE
        lg_ref[pl.ds(row0, rows_g), :] = masked(lg, mask_ref[pl.ds(row0, rows_g), :])

    @pl.when(g == ng - 1)
    def _():
        lg_ref[past - PAGE_SIZE:past, :] += tsp_ref[...]
        lgn = jnp.dot(knew_ref[0].astype(BF16), qbd_ref[0], preferred_element_type=F32) * ATTN_SCALE + tsn_ref[...]
        lgn = jnp.concatenate([lgn, jnp.zeros((t, LANES), F32)], axis=0)
        lg_ref[past:past + t, :] = masked(lgn, mask_ref[past:past + 2 * t, :])[:t]
        lg_ref[past + t:, :] = jnp.full((PAGE_SIZE - t, LANES), NEG, F32)

    @pl.when(g == ng)
    def _():
        lg = lg_ref[...]
        p = jnp.exp(lg - jnp.max(lg, axis=0, keepdims=True))
        lg_ref[...] = p * (1.0 / jnp.sum(p, axis=0, keepdims=True))
        acc_ref[...] = jnp.zeros_like(acc_ref)

    def probs_t(row0, npages):
        return jnp.concatenate([jnp.transpose(lg_ref[pl.ds(row0 + j * PAGE_SIZE, PAGE_SIZE), :])
                                for j in range(npages)], axis=1).astype(BF16)

    @pl.when(g >= ng)
    def _():
        stage()
        row0 = pl.multiple_of((g - ng) * rows_g, rows_g)
        acc_ref[...] += jnp.dot(probs_t(row0, S3_GROUP), stage_ref[...], preferred_element_type=F32)

    @pl.when(g == steps - 1)
    def _():
        vpad_ref[...] = jnp.zeros_like(vpad_ref)
        vpad_ref[0:t, :] = vnew_ref[0]
        acc = acc_ref[...] + jnp.dot(probs_t(past, 1), vpad_ref[...].astype(BF16), preferred_element_type=F32)
        for h in range(N_HEADS):
            o_ref[0, :, h * HEAD_DIM:(h + 1) * HEAD_DIM] = acc[h * t:(h + 1) * t, h * HEAD_DIM:(h + 1) * HEAD_DIM]


def _sample_attention(qbd, mask_t, cache_k, cache_v, k_new, v_new, ts_page, ts_new, page_table):
    n, t, _ = k_new.shape
    n_pages = page_table.shape[1]
    ng = n_pages // S3_GROUP
    lp = mask_t.shape[0]
    assert t * S3_SEQS == LANES and t * N_HEADS == LANES
    sq = np.arange(LANES)
    expand = np.stack([(sq[:, None] // t == s) & (sq[:, None] % t == sq[None, :] % t) for s in range(S3_SEQS)])

    per_seq = lambda b, g, pt: (b, 0, 0)
    const2 = lambda b, g, pt: (0, 0)
    grid_spec = pltpu.PrefetchScalarGridSpec(
        num_scalar_prefetch=1,
        grid=(n, 2 * ng),
        in_specs=[pl.BlockSpec((1, ATTN_WIDTH, LANES), per_seq),
                  pl.BlockSpec((lp, LANES), lambda b, g, pt: (0, b // S3_SEQS)),
                  pl.BlockSpec((1, LANES, LANES), lambda b, g, pt: (b % S3_SEQS, 0, 0)),
                  pl.BlockSpec((1, t, ATTN_WIDTH), per_seq), pl.BlockSpec((1, t, ATTN_WIDTH), per_seq),
                  pl.BlockSpec((PAGE_SIZE, LANES), const2), pl.BlockSpec((t, LANES), const2),
                  pl.BlockSpec(memory_space=pl.ANY), pl.BlockSpec(memory_space=pl.ANY)],
        out_specs=pl.BlockSpec((1, t, ATTN_WIDTH), per_seq),
        scratch_shapes=[pltpu.VMEM((S3_SLOTS * S3_GROUP * S3_PAGE_ROWS, HEAD_DIM), F32),
                        pltpu.SemaphoreType.DMA((S3_SLOTS, S3_GROUP)),
                        pltpu.VMEM((lp, LANES), F32), pltpu.VMEM((LANES, ATTN_WIDTH), F32),
                        pltpu.VMEM((S3_GROUP * PAGE_SIZE, ATTN_WIDTH), BF16),
                        pltpu.VMEM((PAGE_SIZE, ATTN_WIDTH), F32)],
    )
    return pl.pallas_call(
        functools.partial(_s3_body, n_seq=n, n_pages=n_pages, t=t),
        grid_spec=grid_spec,
        out_shape=jax.ShapeDtypeStruct((n, t, ATTN_WIDTH), F32),
        compiler_params=_cparams(("arbitrary", "arbitrary")),
        name="sample_attention",
    )(page_table, qbd, mask_t, jnp.asarray(expand, BF16), k_new, v_new, ts_page, ts_new, cache_k, cache_v)


W_IN_SPLITS = (POOL_WIDTH, ATTN_WIDTH, ATTN_WIDTH, ATTN_WIDTH, IDX_HEADS * IDX_DIM, IDX_DIM, IDX_HEADS, D_MODEL, D_MODEL)
W_IN_OFFS = tuple(int(v) for v in np.cumsum((0,) + W_IN_SPLITS))


def _split_w_in(w):
    w16 = w.astype(BF16)
    return dict(all=w16, gp=w16[:, W_IN_OFFS[7]:W_IN_OFFS[8]], ga=w16[:, W_IN_OFFS[8]:W_IN_OFFS[9]])


def _project(x2d, ln1, wi):
    m = x2d.shape[0]
    tm = min(m, 1024)
    w, o = wi["all"], W_IN_OFFS
    xn = _rmsnorm(x2d, ln1, BF16)
    (u,) = _matmul(xn, w, [F32], tm=tm, tn=1024, name="proj_u", col0=o[0], n=POOL_WIDTH)
    (q,) = _matmul(xn, w, [BF16], tm=tm, tn=1024, name="proj_q", col0=o[1], n=ATTN_WIDTH)
    k, k16 = _matmul_heads(xn, w, tm=min(m, 512), tn=1024, name="proj_k", col0=o[2])
    v, v16 = _matmul_heads(xn, w, tm=min(m, 512), tn=1024, name="proj_v", col0=o[3])
    (qi,) = _matmul(xn, w, [BF16], tm=tm, tn=1024, name="proj_qi", col0=o[4], n=IDX_HEADS * IDX_DIM)
    (kiwi,) = _matmul(xn, w, [F32], tm=tm, tn=LANES, name="proj_kiwi", col0=o[5], n=LANES)
    ki = kiwi[:, :IDX_DIM]
    wgt = kiwi[:, IDX_DIM:IDX_DIM + IDX_HEADS]
    return xn, u, q, k, k16, v, v16, qi, ki, wgt


def _finish(x2d, xn, pool_o, attn_o, wi, w_bp, w_ba, w_out, ln2, w1, w2, ln_f):
    m = x2d.shape[0]
    tm = min(m, 1024)
    mg = _merge(xn, pool_o, attn_o, wi["gp"], wi["ga"], w_bp, w_ba)
    (x1,) = _matmul(mg, w_out, [F32], tm=tm, tn=1024, res=x2d, name="out_proj")
    hn = _rmsnorm(x1, ln2, BF16)
    (h,) = _matmul(hn, w1, [BF16], tm=tm, tn=1024, act="relu2", name="mlp_in")
    x2 = _matmul_ktiled(h, w2, x1, tm=tm, tn=1024, tk=2048, name="mlp_out")
    return _rmsnorm(x2, ln_f, F32)


def kernel(x_prompt, x_sample, cache_k, cache_v, cache_kidx, state_pool, page_table, ln1, w_in, pool_w,
           pool_scale, rel_bias, w_branch_pool, w_branch_attn, w_out, ln2, w_mlp_in, w_mlp_out, ln_f):
    bn, s, _ = x_prompt.shape
    n, t, _ = x_sample.shape
    depth = w_in.shape[0]
    assert depth == 1, "one layer: the caches and pooling state of layer 0 are the only ones read"
    n_pages = page_table.shape[1]
    n_phys = cache_k.shape[1]

    wi = _split_w_in(w_in[0])
    pw = pool_w[0].astype(BF16)
    w_bp = w_branch_pool[0].astype(BF16)
    w_ba = w_branch_attn[0].astype(BF16)
    wo = w_out[0].astype(BF16)
    w1 = w_mlp_in[0].astype(BF16)
    w2 = w_mlp_out[0].astype(BF16)
    tb_prompt, ts_page, ts_new = _bias_tables(rel_bias, t)

    xp = x_prompt.reshape(bn * s, D_MODEL)
    xn, u, q, k, k16, v, v16, qi, ki, wgt = _project(xp, ln1[0], wi)
    pool_o = _pool_prompt(u.reshape(bn, s, POOL_WIDTH), pw, pool_scale[0]).reshape(bn * s, POOL_WIDTH)
    ki_c = ki.astype(BF16).reshape(bn, s // P1_CK, P1_CK, IDX_DIM)
    zeros = jnp.zeros_like(ki_c)
    kbd = jnp.concatenate([jnp.concatenate([ki_c, zeros], axis=-1), jnp.concatenate([zeros, ki_c], axis=-1)],
                          axis=2).reshape(bn, 2 * s, 2 * IDX_DIM)
    amask = _prompt_select(qi.reshape(bn, s, IDX_HEADS * IDX_DIM), kbd,
                           jnp.swapaxes(wgt.reshape(bn, s, IDX_HEADS), 1, 2))
    attn_o = _prompt_attention(q.reshape(bn, s, ATTN_WIDTH), k16.reshape(bn, s, ATTN_WIDTH),
                               v16.reshape(bn, s, ATTN_WIDTH), amask, tb_prompt).reshape(bn * s, ATTN_WIDTH)
    y_prompt = _finish(xp, xn, pool_o, attn_o, wi, w_bp, w_ba, wo, ln2[0], w1, w2, ln_f).reshape(bn, s, D_MODEL)
    k_prompt = k.reshape(1, bn, s, N_HEADS, HEAD_DIM)
    v_prompt = v.reshape(1, bn, s, N_HEADS, HEAD_DIM)
    kidx_prompt = ki.reshape(1, bn, s, IDX_DIM)
    pool_prompt = u.reshape(bn, s, POOL_WIDTH)[:, s - POOL_PREFIX:, :][None]

    xs = x_sample.reshape(n * t, D_MODEL)
    xn, u, q, k, k16, v, v16, qi, ki, wgt = _project(xs, ln1[0], wi)
    u3 = u.reshape(n, t, POOL_WIDTH)
    pool_o = _pool_sample(u3, state_pool[0], pw, pool_scale[0])
    qi_hq = qi.reshape(n, t, IDX_HEADS, IDX_DIM).transpose(0, 2, 1, 3).reshape(n, IDX_HEADS * t, IDX_DIM)
    wi_hq = wgt.reshape(n, t, IDX_HEADS).transpose(0, 2, 1).reshape(n, IDX_HEADS * t, 1)
    ki_new_pad = jnp.pad(jnp.swapaxes(ki.reshape(n, t, IDX_DIM), 1, 2), ((0, 0), (0, 0), (0, PAGE_SIZE - t)))
    scores = _sample_scores(qi_hq, wi_hq, jnp.swapaxes(cache_kidx[0], 1, 2), ki_new_pad, page_table)
    lp = scores.shape[-1]
    mask_t = _sample_select(scores.reshape(n * t, lp).T, t)
    q_t = q.reshape(n, t, N_HEADS, HEAD_DIM).transpose(0, 2, 3, 1)
    eye = jnp.eye(N_HEADS, dtype=BF16)
    qbd = (q_t[:, :, :, None, :] * eye[None, :, None, :, None]).reshape(n, ATTN_WIDTH, N_HEADS * t)
    attn_o = _sample_attention(qbd, mask_t, cache_k[0], cache_v[0], k.reshape(n, t, ATTN_WIDTH),
                               v.reshape(n, t, ATTN_WIDTH), ts_page, ts_new, page_table)
    attn_o = attn_o.reshape(n * t, ATTN_WIDTH).astype(BF16)
    y_sample = _finish(xs, xn, pool_o, attn_o, wi, w_bp, w_ba, wo, ln2[0], w1, w2, ln_f).reshape(n, t, D_MODEL)
    k_sample = k.reshape(1, n, t, N_HEADS, HEAD_DIM)
    v_sample = v.reshape(1, n, t, N_HEADS, HEAD_DIM)
    kidx_sample = ki.reshape(1, n, t, IDX_DIM)
    pool_sample = jnp.concatenate([state_pool[0][:, t:, :], u3], axis=1)[None]

    return (y_prompt, y_sample, k_prompt, v_prompt, kidx_prompt, pool_prompt,
            k_sample, v_sample, kidx_sample, pool_sample)
```

```python
import functools
import math

import numpy as np
import jax
import jax.numpy as jnp
from jax import lax
from jax.experimental import pallas as pl
from jax.experimental.pallas import tpu as pltpu

D_MODEL = 4096
PAST_LEN = 2048
PAGE_SIZE = 128
POOL_WIDTH = D_MODEL // 2
POOL_WINDOWS = (2, 4, 8, 16)
N_POOL_GROUPS = len(POOL_WINDOWS)
POOL_GROUP = POOL_WIDTH // N_POOL_GROUPS
POOL_PREFIX = max(POOL_WINDOWS) - 1
HEAD_DIM = 128
N_HEADS = (D_MODEL // 2) // HEAD_DIM
ATTN_WIDTH = N_HEADS * HEAD_DIM
IDX_HEADS = 32
IDX_DIM = 64
TOPK_MAX = 256
NUM_BUCKETS = 32
MAX_DISTANCE = 128
D_FF = 4 * D_MODEL
EPS = 1e-6

F32 = jnp.float32
BF16 = jnp.bfloat16
I32 = jnp.int32

LANES = 128
VMEM_LIMIT = 56 * 1024 * 1024
INT_MIN = -(2 ** 31)
NEG = -1e30
HALO = 16
PROMPT_HALO = 128
IDX_W_SCALE = (IDX_HEADS ** -0.5) * (IDX_DIM ** -0.5)
ATTN_SCALE = HEAD_DIM ** -0.5
LOG2E = math.log2(math.e)


def _cparams(sem):
    return pltpu.CompilerParams(dimension_semantics=sem, vmem_limit_bytes=VMEM_LIMIT)


def _rmsnorm_body(x_ref, g_ref, o_ref):
    x = x_ref[...]
    y = x * lax.rsqrt(jnp.mean(x * x, axis=-1, keepdims=True) + EPS)
    o_ref[...] = (y * g_ref[...]).astype(o_ref.dtype)


def _rmsnorm(x, g, out_dtype, tm=512):
    m, d = x.shape
    return pl.pallas_call(
        _rmsnorm_body,
        grid=(m // tm,),
        in_specs=[pl.BlockSpec((tm, d), lambda i: (i, 0)), pl.BlockSpec((1, d), lambda i: (0, 0))],
        out_specs=pl.BlockSpec((tm, d), lambda i: (i, 0)),
        out_shape=jax.ShapeDtypeStruct((m, d), out_dtype),
        compiler_params=_cparams(("parallel",)),
        name="rmsnorm",
    )(x, g.reshape(1, d))


def _mm_body(*refs, n_out, act, has_res):
    x_ref, w_ref = refs[0], refs[1]
    res_ref = refs[2] if has_res else None
    outs = refs[2 + int(has_res):2 + int(has_res) + n_out]
    acc = jnp.dot(x_ref[...], w_ref[...], preferred_element_type=F32)
    if act == "relu2":
        acc = jnp.square(jnp.maximum(acc, 0.0))
    if has_res:
        acc = res_ref[...] + acc
    for o in outs:
        o[...] = acc.astype(o.dtype)


def _matmul(x, w, out_dtypes, *, tm, tn, act=None, res=None, name="matmul", col0=0, n=None):
    m, k = x.shape
    n = w.shape[1] if n is None else n
    assert col0 % tn == 0
    cb = col0 // tn
    in_specs = [pl.BlockSpec((tm, k), lambda i, j: (i, 0)), pl.BlockSpec((k, tn), lambda i, j: (0, cb + j))]
    args = [x, w]
    if res is not None:
        in_specs.append(pl.BlockSpec((tm, tn), lambda i, j: (i, j)))
        args.append(res)
    outs = pl.pallas_call(
        functools.partial(_mm_body, n_out=len(out_dtypes), act=act, has_res=res is not None),
        grid=(m // tm, n // tn),
        in_specs=in_specs,
        out_specs=[pl.BlockSpec((tm, tn), lambda i, j: (i, j)) for _ in out_dtypes],
        out_shape=[jax.ShapeDtypeStruct((m, n), dt) for dt in out_dtypes],
        compiler_params=_cparams(("parallel", "parallel")),
        name=name,
    )(*args)
    return outs


def _mmh_body(x_ref, w_ref, o32_ref, o16_ref, *, heads):
    j = pl.program_id(1)
    acc = jnp.dot(x_ref[...], w_ref[...], preferred_element_type=F32)
    o16_ref[...] = acc.astype(BF16)
    tm = acc.shape[0]
    for hh in range(heads):
        o32_ref[pl.ds(j * heads + hh, tm, stride=N_HEADS), :] = acc[:, hh * HEAD_DIM:(hh + 1) * HEAD_DIM]


def _matmul_heads(x, w, *, tm, tn, name, col0=0):
    m, k = x.shape
    n = N_HEADS * HEAD_DIM
    assert col0 % tn == 0
    cb = col0 // tn
    return pl.pallas_call(
        functools.partial(_mmh_body, heads=tn // HEAD_DIM),
        grid=(m // tm, n // tn),
        in_specs=[pl.BlockSpec((tm, k), lambda i, j: (i, 0)), pl.BlockSpec((k, tn), lambda i, j: (0, cb + j))],
        out_specs=[pl.BlockSpec((tm * N_HEADS, HEAD_DIM), lambda i, j: (i, 0)),
                   pl.BlockSpec((tm, tn), lambda i, j: (i, j))],
        out_shape=[jax.ShapeDtypeStruct((m * N_HEADS, HEAD_DIM), F32), jax.ShapeDtypeStruct((m, n), BF16)],
        compiler_params=_cparams(("parallel", "arbitrary")),
        name=name,
    )(x, w)


def _mmk_body(x_ref, w_ref, res_ref, o_ref, acc_ref, *, nk):
    k = pl.program_id(2)

    @pl.when(k == 0)
    def _():
        acc_ref[...] = jnp.zeros_like(acc_ref)

    acc_ref[...] += jnp.dot(x_ref[...], w_ref[...], preferred_element_type=F32)

    @pl.when(k == nk - 1)
    def _():
        o_ref[...] = res_ref[...] + acc_ref[...]


def _matmul_ktiled(x, w, res, *, tm, tn, tk, name):
    m, k = x.shape
    n = w.shape[1]
    nk = k // tk
    return pl.pallas_call(
        functools.partial(_mmk_body, nk=nk),
        grid=(m // tm, n // tn, nk),
        in_specs=[pl.BlockSpec((tm, tk), lambda i, j, kk: (i, kk)),
                  pl.BlockSpec((tk, tn), lambda i, j, kk: (kk, j)),
                  pl.BlockSpec((tm, tn), lambda i, j, kk: (i, j))],
        out_specs=pl.BlockSpec((tm, tn), lambda i, j, kk: (i, j)),
        out_shape=jax.ShapeDtypeStruct((m, n), F32),
        scratch_shapes=[pltpu.VMEM((tm, tn), F32)],
        compiler_params=_cparams(("parallel", "parallel", "arbitrary")),
        name=name,
    )(x, w, res)


def _merge_body(xn_ref, po_ref, ao_ref, wgp_ref, wga_ref, wbp_ref, wba_ref, o_ref):
    xn = xn_ref[...]
    gp = jnp.dot(xn, wgp_ref[...], preferred_element_type=F32)
    ga = jnp.dot(xn, wga_ref[...], preferred_element_type=F32)
    a = jnp.dot(po_ref[...], wbp_ref[...], preferred_element_type=F32)
    b = jnp.dot(ao_ref[...], wba_ref[...], preferred_element_type=F32)
    o_ref[...] = (jax.nn.sigmoid(gp) * a + jax.nn.sigmoid(ga) * b).astype(o_ref.dtype)


def _merge(xn, po, ao, wgp, wga, wbp, wba, *, tm=512, tn=512):
    m, d = xn.shape
    kb = po.shape[1]
    n = wgp.shape[1]
    row = lambda i, j: (i, 0)
    col = lambda i, j: (0, j)
    return pl.pallas_call(
        _merge_body,
        grid=(m // tm, n // tn),
        in_specs=[pl.BlockSpec((tm, d), row), pl.BlockSpec((tm, kb), row), pl.BlockSpec((tm, kb), row),
                  pl.BlockSpec((d, tn), col), pl.BlockSpec((d, tn), col),
                  pl.BlockSpec((kb, tn), col), pl.BlockSpec((kb, tn), col)],
        out_specs=pl.BlockSpec((tm, tn), lambda i, j: (i, j)),
        out_shape=jax.ShapeDtypeStruct((m, n), BF16),
        compiler_params=_cparams(("parallel", "parallel")),
        name="merge",
    )(xn, po, ao, wgp, wga, wbp, wba)


def _pool_body(*refs, has_halo):
    if has_halo:
        halo_ref, cur_ref, s_ref, inv_ref, pw_ref, sc_ref, o_ref = refs
        halo = jnp.where(pl.program_id(1) == 0, 0.0, halo_ref[0])
        cur = cur_ref[0]
        ext = jnp.concatenate([halo, cur], axis=0)
    else:
        ext_ref, cur_ref, s_ref, inv_ref, pw_ref, sc_ref, o_ref = refs
        ext = ext_ref[...]
        cur = cur_ref[...]
    hi = ext.astype(BF16)
    lo = (ext - hi.astype(F32)).astype(BF16)
    band = s_ref[0]
    win = (jnp.dot(band, hi, preferred_element_type=F32) + jnp.dot(band, lo, preferred_element_type=F32))
    pooled = win * inv_ref[0] - cur
    y = jnp.dot(pooled.astype(BF16), pw_ref[0], preferred_element_type=F32) * sc_ref[...]
    if has_halo:
        o_ref[0] = y.astype(o_ref.dtype)
    else:
        o_ref[...] = y.astype(o_ref.dtype)


def _pool_prompt(u, pool_w, pool_scale, *, tb=512):
    n, t, _ = u.shape
    halo = PROMPT_HALO
    band = np.zeros((N_POOL_GROUPS, tb, halo + tb), np.float32)
    inv = np.zeros((N_POOL_GROUPS, t, 1), np.float32)
    r = np.arange(tb)[:, None]
    c = np.arange(halo + tb)[None, :]
    for g, w in enumerate(POOL_WINDOWS):
        band[g] = ((c >= r + halo - w + 1) & (c <= r + halo)).astype(np.float32)
        inv[g, :, 0] = 1.0 / np.minimum(w, np.arange(t) + 1)
    hb = tb // halo
    return pl.pallas_call(
        functools.partial(_pool_body, has_halo=True),
        grid=(n, t // tb, N_POOL_GROUPS),
        in_specs=[pl.BlockSpec((1, halo, POOL_GROUP), lambda b, i, g: (b, jnp.maximum(i * hb - 1, 0), g)),
                  pl.BlockSpec((1, tb, POOL_GROUP), lambda b, i, g: (b, i, g)),
                  pl.BlockSpec((1, tb, halo + tb), lambda b, i, g: (g, 0, 0)),
                  pl.BlockSpec((1, tb, 1), lambda b, i, g: (g, i, 0)),
                  pl.BlockSpec((1, POOL_GROUP, POOL_GROUP), lambda b, i, g: (g, 0, 0)),
                  pl.BlockSpec((1, POOL_GROUP), lambda b, i, g: (0, g))],
        out_specs=pl.BlockSpec((1, tb, POOL_GROUP), lambda b, i, g: (b, i, g)),
        out_shape=jax.ShapeDtypeStruct((n, t, POOL_WIDTH), BF16),
        compiler_params=_cparams(("parallel", "arbitrary", "arbitrary")),
        name="pool_prompt",
    )(u, u, jnp.asarray(band, BF16), jnp.asarray(inv), pool_w, pool_scale.reshape(1, POOL_WIDTH))


def _pool_sample(u, prefix, pool_w, pool_scale, *, sb=16):
    n, t, _ = u.shape
    e = HALO + t
    ext = jnp.concatenate([jnp.zeros((n, HALO - POOL_PREFIX, POOL_WIDTH), F32), prefix.astype(F32), u], axis=1)
    ext = ext.reshape(n * e, POOL_WIDTH)
    band = np.zeros((N_POOL_GROUPS, sb * t, sb * e), np.float32)
    inv = np.zeros((N_POOL_GROUPS, sb * t, 1), np.float32)
    r = np.arange(sb * t)[:, None]
    c = np.arange(sb * e)[None, :]
    for g, w in enumerate(POOL_WINDOWS):
        pos = HALO + r % t
        band[g] = ((r // t == c // e) & (c % e >= pos - w + 1) & (c % e <= pos)).astype(np.float32)
        inv[g] = 1.0 / w
    out = pl.pallas_call(
        functools.partial(_pool_body, has_halo=False),
        grid=(n // sb, N_POOL_GROUPS),
        in_specs=[pl.BlockSpec((sb * e, POOL_GROUP), lambda i, g: (i, g)),
                  pl.BlockSpec((sb * t, POOL_GROUP), lambda i, g: (i, g)),
                  pl.BlockSpec((1, sb * t, sb * e), lambda i, g: (g, 0, 0)),
                  pl.BlockSpec((1, sb * t, 1), lambda i, g: (g, 0, 0)),
                  pl.BlockSpec((1, POOL_GROUP, POOL_GROUP), lambda i, g: (g, 0, 0)),
                  pl.BlockSpec((1, POOL_GROUP), lambda i, g: (0, g))],
        out_specs=pl.BlockSpec((sb * t, POOL_GROUP), lambda i, g: (i, g)),
        out_shape=jax.ShapeDtypeStruct((n * t, POOL_WIDTH), BF16),
        compiler_params=_cparams(("parallel", "arbitrary")),
        name="pool_sample",
    )(ext, u.reshape(n * t, POOL_WIDTH), jnp.asarray(band, BF16), jnp.asarray(inv), pool_w,
      pool_scale.reshape(1, POOL_WIDTH))
    return out


def _bucket_thresholds():
    d = np.arange(0, 4 * MAX_DISTANCE)
    max_exact = NUM_BUCKETS // 2
    df = np.maximum(d, 1).astype(np.float32)
    large = max_exact + (np.log(df / np.float32(max_exact)) / np.float32(math.log(MAX_DISTANCE / max_exact))
                         * np.float32(NUM_BUCKETS - max_exact)).astype(np.int32)
    bucket = np.where(d < max_exact, d, np.minimum(large, NUM_BUCKETS - 1))
    assert np.all(np.diff(bucket) >= 0) and np.all(bucket[MAX_DISTANCE + 1:] == NUM_BUCKETS - 1)
    return [int(np.argmax(bucket >= b)) for b in range(NUM_BUCKETS)]


def _bias_prompt_body(rb_ref, o_ref):
    h = pl.program_id(0)
    thr = _bucket_thresholds()
    base = lax.broadcasted_iota(I32, (LANES, LANES), 0) - lax.broadcasted_iota(I32, (LANES, LANES), 1)
    for j in range(2):
        dist = base + j * LANES
        tile = jnp.full((LANES, LANES), rb_ref[0, h], F32)
        for b in range(1, NUM_BUCKETS):
            tile = jnp.where(dist >= thr[b], rb_ref[b, h], tile)
        o_ref[0, j] = (tile - rb_ref[NUM_BUCKETS - 1, h]) * LOG2E


def _bias_sample_body(rbx_ref, page_ref, new_ref, *, t):
    thr = _bucket_thresholds()
    for o_ref, rows, off in ((page_ref, PAGE_SIZE, PAGE_SIZE), (new_ref, t, 0)):
        qq = lax.broadcasted_iota(I32, (rows, LANES), 1) % t
        dist = off + qq - lax.broadcasted_iota(I32, (rows, LANES), 0)
        tile = jnp.broadcast_to(rbx_ref[0:1, :], (rows, LANES))
        for b in range(1, NUM_BUCKETS):
            tile = jnp.where(dist >= thr[b], rbx_ref[b:b + 1, :], tile)
        o_ref[...] = tile - rbx_ref[NUM_BUCKETS - 1:NUM_BUCKETS, :]


def _bias_tables(rel_bias, t):
    rb = rel_bias.astype(F32)
    tp = pl.pallas_call(
        _bias_prompt_body,
        grid=(N_HEADS,),
        in_specs=[pl.BlockSpec(memory_space=pltpu.SMEM)],
        out_specs=pl.BlockSpec((1, 2, LANES, LANES), lambda h: (h, 0, 0, 0)),
        out_shape=jax.ShapeDtypeStruct((N_HEADS, 2, LANES, LANES), F32),
        name="bias_prompt",
    )(rb)
    rbx = jnp.repeat(rb, LANES // N_HEADS, axis=1)
    ts_page, ts_new = pl.pallas_call(
        functools.partial(_bias_sample_body, t=t),
        out_shape=[jax.ShapeDtypeStruct((PAGE_SIZE, LANES), F32), jax.ShapeDtypeStruct((t, LANES), F32)],
        name="bias_sample",
    )(rbx)
    return tp, ts_page, ts_new


SEL_RB = 128


def _sortable(score):
    bits = lax.bitcast_convert_type(score, I32)
    return bits ^ ((bits >> 31) & 0x7FFFFFFF)


def _select_topk(key_ref, jcut_ref, krow, nblk, tq, blocks_per_iter=1):
    def counts(pred):
        def body(r, acc):
            for u in range(blocks_per_iter):
                row0 = pl.multiple_of((r * blocks_per_iter + u) * SEL_RB, SEL_RB)
                hit = pred(key_ref[pl.ds(row0, SEL_RB), :], row0).astype(I32)
                acc = acc + jnp.sum(hit.reshape(SEL_RB // 8, 8, tq), axis=0)
            return acc
        acc = lax.fori_loop(0, nblk // blocks_per_iter, body, jnp.zeros((8, tq), I32))
        return jnp.sum(acc, axis=0, keepdims=True)

    def count_ge(cand):
        return counts(lambda blk, row0: blk >= cand)

    zero = jnp.zeros((1, tq), I32)
    v = jnp.where(count_ge(zero) >= krow, zero, jnp.full((1, tq), INT_MIN, I32))

    def bit_body(bi, v):
        cand = v | jnp.left_shift(jnp.int32(1), 30 - bi)
        return jnp.where(count_ge(cand) >= krow, cand, v)

    v = lax.fori_loop(0, 31, bit_body, v)

    n_gt = counts(lambda blk, row0: blk > v)
    n_ge = counts(lambda blk, row0: blk >= v)
    need = krow - n_gt
    jcut_ref[...] = jnp.full((8, tq), nblk * SEL_RB, I32)

    @pl.when(jnp.max(n_ge - krow) > 0)
    def _():
        def jbit(bi, x):
            cand = x + jnp.left_shift(jnp.int32(1), 15 - bi)

            def pred(blk, row0):
                row = row0 + lax.broadcasted_iota(I32, (SEL_RB, tq), 0)
                return (blk == v) & (row <= cand)
            return jnp.where(counts(pred) < need, cand, x)
        x = lax.fori_loop(0, 16, jbit, jnp.full((1, tq), -1, I32))
        jcut_ref[...] = jnp.broadcast_to(x + 2, (8, tq))

    return v


def _selected(key_ref, jcut_ref, v, row0, tq):
    blk = key_ref[pl.ds(row0, SEL_RB), :]
    row = row0 + lax.broadcasted_iota(I32, (SEL_RB, tq), 0)
    return (blk > v) | ((blk == v) & (row < jcut_ref[0:1, :]))


P1_TQ = 256
P1_CK = 256


def _p1_body(qi_ref, kbd_ref, wt_ref, amask_ref, key_ref, jcut_ref, acc_ref, *, s):
    i = pl.program_id(1)
    t0 = i * P1_TQ
    n_ch = (i + 1) * (P1_TQ // P1_CK)
    n_blk = (i + 1) * (P1_TQ // SEL_RB)
    tq_iota = t0 + lax.broadcasted_iota(I32, (P1_CK, P1_TQ), 1)

    def chunk_body(c, _):
        kb = kbd_ref[0, pl.ds(pl.multiple_of(c * 2 * P1_CK, 2 * P1_CK), 2 * P1_CK), :]
        for hp in range(IDX_HEADS // 2):
            d2 = lax.dot_general(kb, qi_ref[0, :, hp * LANES:(hp + 1) * LANES], (((1,), (1,)), ((), ())),
                                 preferred_element_type=F32)
            w0 = wt_ref[0, 2 * hp:2 * hp + 1, :] * IDX_W_SCALE
            w1 = wt_ref[0, 2 * hp + 1:2 * hp + 2, :] * IDX_W_SCALE
            part = w0 * jnp.maximum(d2[:P1_CK], 0.0) + w1 * jnp.maximum(d2[P1_CK:], 0.0)
            if hp == 0:
                acc_ref[...] = part
            else:
                acc_ref[...] += part
        row0 = pl.multiple_of(c * P1_CK, P1_CK)
        srow = row0 + lax.broadcasted_iota(I32, (P1_CK, P1_TQ), 0)
        key_ref[pl.ds(row0, P1_CK), :] = jnp.where(srow <= tq_iota, _sortable(acc_ref[...]), INT_MIN)
        return 0

    lax.fori_loop(0, n_ch, chunk_body, 0)

    krow = jnp.minimum(TOPK_MAX, t0 + lax.broadcasted_iota(I32, (1, P1_TQ), 1) + 1)
    v = _select_topk(key_ref, jcut_ref, krow, n_blk, P1_TQ, blocks_per_iter=P1_TQ // SEL_RB)

    def out_body(r, _):
        row0 = pl.multiple_of(r * SEL_RB, SEL_RB)
        am = jnp.where(_selected(key_ref, jcut_ref, v, row0, P1_TQ), 0.0, NEG)
        for h in range(P1_TQ // LANES):
            amask_ref[0, h * LANES:(h + 1) * LANES, pl.ds(row0, SEL_RB)] = jnp.transpose(
                am[:, h * LANES:(h + 1) * LANES])
        return 0

    lax.fori_loop(0, n_blk, out_body, 0)

    def fill_body(r, _):
        amask_ref[0, :, pl.ds(pl.multiple_of(r * SEL_RB, SEL_RB), SEL_RB)] = jnp.full((P1_TQ, SEL_RB), NEG, F32)
        return 0

    lax.fori_loop(n_blk, s // SEL_RB, fill_body, 0)


def _prompt_select(qi, kbd, wt):
    b, s, _ = qi.shape
    return pl.pallas_call(
        functools.partial(_p1_body, s=s),
        grid=(b, s // P1_TQ),
        in_specs=[pl.BlockSpec((1, P1_TQ, IDX_HEADS * IDX_DIM), lambda bb, i: (bb, i, 0)),
                  pl.BlockSpec((1, 2 * s, 2 * IDX_DIM), lambda bb, i: (bb, 0, 0)),
                  pl.BlockSpec((1, IDX_HEADS, P1_TQ), lambda bb, i: (bb, 0, i))],
        out_specs=pl.BlockSpec((1, P1_TQ, s), lambda bb, i: (bb, i, 0)),
        out_shape=jax.ShapeDtypeStruct((b, s, s), F32),
        scratch_shapes=[pltpu.VMEM((s, P1_TQ), I32), pltpu.VMEM((8, P1_TQ), I32),
                        pltpu.VMEM((P1_CK, P1_TQ), F32)],
        compiler_params=_cparams(("parallel", "arbitrary")),
        name="prompt_select",
    )(qi, kbd, wt)


P2_TQ = 512
P2_CK = 1024
P2_HEADS = 2


def _p2_body(q_ref, k_ref, v_ref, amask_ref, tb_ref, o_ref, *lg_refs, nc, i0):
    i = i0 + pl.program_id(1)
    tq = P2_TQ
    chunks = [(c * P2_CK, (c + 1) * P2_CK) for c in range(nc)]

    def lane_fold(x, op):
        out = x[:, :LANES]
        for j in range(1, x.shape[1] // LANES):
            out = op(out, x[:, j * LANES:(j + 1) * LANES])
        return out

    for hh, lg_ref in enumerate(lg_refs):
        hd = slice(hh * HEAD_DIM, (hh + 1) * HEAD_DIM)
        q = q_ref[0, :, hd]
        for lo, hi in chunks:
            sc = lax.dot_general(q, k_ref[0, lo:hi, hd], (((1,), (1,)), ((), ())), preferred_element_type=F32)
            lg_ref[:, lo:hi] = sc * (ATTN_SCALE * LOG2E) + amask_ref[0, :, lo:hi]

        for a in range(tq // LANES):
            rows = slice(a * LANES, (a + 1) * LANES)
            dcol = pl.multiple_of(i * tq + a * LANES, LANES)
            lg_ref[rows, pl.ds(dcol, LANES)] += tb_ref[hh, 0]
            if a > 0:
                scol = pl.multiple_of(i * tq + (a - 1) * LANES, LANES)
                lg_ref[rows, pl.ds(scol, LANES)] += tb_ref[hh, 1]

        @pl.when(i > 0)
        def _():
            scol = pl.multiple_of(i * tq - LANES, LANES)
            lg_ref[0:LANES, pl.ds(scol, LANES)] += tb_ref[hh, 1]

    for hh, lg_ref in enumerate(lg_refs):
        hd = slice(hh * HEAD_DIM, (hh + 1) * HEAD_DIM)
        m = jnp.full((tq, LANES), NEG, F32)
        for lo, hi in chunks:
            m = jnp.maximum(m, lane_fold(lg_ref[:, lo:hi], jnp.maximum))
        m = jnp.max(m, axis=1, keepdims=True)
        l = jnp.zeros((tq, LANES), F32)
        acc = jnp.zeros((tq, HEAD_DIM), F32)
        for lo, hi in chunks:
            p = jnp.exp2(lg_ref[:, lo:hi] - m)
            l = l + lane_fold(p, jnp.add)
            acc = acc + jnp.dot(p.astype(BF16), v_ref[0, lo:hi, hd], preferred_element_type=F32)
        o_ref[0, :, hd] = (acc / jnp.sum(l, axis=1, keepdims=True)).astype(o_ref.dtype)


def _prompt_attention(q, k, v, amask, tb):
    b, s, _ = q.shape
    wd = P2_HEADS * HEAD_DIM
    per_call = P2_CK // P2_TQ
    outs = []
    for nc in range(1, s // P2_CK + 1):
        i0 = (nc - 1) * per_call
        outs.append(pl.pallas_call(
            functools.partial(_p2_body, nc=nc, i0=i0),
            grid=(b, per_call, N_HEADS // P2_HEADS),
            in_specs=[pl.BlockSpec((1, P2_TQ, wd), lambda bb, i, h, i0=i0: (bb, i0 + i, h)),
                      pl.BlockSpec((1, nc * P2_CK, wd), lambda bb, i, h: (bb, 0, h)),
                      pl.BlockSpec((1, nc * P2_CK, wd), lambda bb, i, h: (bb, 0, h)),
                      pl.BlockSpec((1, P2_TQ, nc * P2_CK), lambda bb, i, h, i0=i0: (bb, i0 + i, 0)),
                      pl.BlockSpec((P2_HEADS, 2, LANES, LANES), lambda bb, i, h: (h, 0, 0, 0))],
            out_specs=pl.BlockSpec((1, P2_TQ, wd), lambda bb, i, h: (bb, i, h)),
            out_shape=jax.ShapeDtypeStruct((b, per_call * P2_TQ, ATTN_WIDTH), BF16),
            scratch_shapes=[pltpu.VMEM((P2_TQ, nc * P2_CK), F32) for _ in range(P2_HEADS)],
            compiler_params=_cparams(("parallel", "parallel", "arbitrary")),
            name=f"prompt_attention_{nc}",
        )(q, k, v, amask, tb))
    return jnp.concatenate(outs, axis=1)


def _s1_body(pt_ref, qi_ref, w_ref, *refs, n_pages, t):
    page_refs = refs[:n_pages + 1]
    o_ref = refs[n_pages + 1]
    qi = qi_ref[0]
    w = jnp.broadcast_to(w_ref[0] * IDX_W_SCALE, (IDX_HEADS * t, LANES))
    for p, kp_ref in enumerate(page_refs):
        d = jnp.dot(qi, kp_ref[0].astype(BF16), preferred_element_type=F32)
        wr = (w * jnp.maximum(d, 0.0)).reshape(IDX_HEADS, t, LANES)
        o_ref[0, :, p * LANES:(p + 1) * LANES] = jnp.sum(wr, axis=0)


def _sample_scores(qi_hq, wi_hq, cache_kidx, ki_new_pad, page_table):
    n, rows, _ = qi_hq.shape
    t = rows // IDX_HEADS
    n_pages = page_table.shape[1]
    lp = (n_pages + 1) * PAGE_SIZE

    def page_spec(p):
        return pl.BlockSpec((1, IDX_DIM, PAGE_SIZE), lambda b, pt: (pt[b, p], 0, 0))

    grid_spec = pltpu.PrefetchScalarGridSpec(
        num_scalar_prefetch=1,
        grid=(n,),
        in_specs=[pl.BlockSpec((1, rows, IDX_DIM), lambda b, pt: (b, 0, 0)),
                  pl.BlockSpec((1, rows, 1), lambda b, pt: (b, 0, 0))]
                 + [page_spec(p) for p in range(n_pages)]
                 + [pl.BlockSpec((1, IDX_DIM, PAGE_SIZE), lambda b, pt: (b, 0, 0))],
        out_specs=pl.BlockSpec((1, t, lp), lambda b, pt: (b, 0, 0)),
    )
    return pl.pallas_call(
        functools.partial(_s1_body, n_pages=n_pages, t=t),
        grid_spec=grid_spec,
        out_shape=jax.ShapeDtypeStruct((n, t, lp), F32),
        compiler_params=_cparams(("arbitrary",)),
        name="sample_scores",
    )(page_table, qi_hq, wi_hq, *([cache_kidx] * n_pages), ki_new_pad)


S2_TQ = 256


def _s2_body(sc_ref, mask_ref, key_ref, jcut_ref, *, t, lp):
    c0 = pl.program_id(0) * S2_TQ
    nblk = lp // SEL_RB
    lim = PAST_LEN + (c0 + lax.broadcasted_iota(I32, (SEL_RB, S2_TQ), 1)) % t + 1
    for r in range(nblk):
        rows = slice(r * SEL_RB, (r + 1) * SEL_RB)
        srow = r * SEL_RB + lax.broadcasted_iota(I32, (SEL_RB, S2_TQ), 0)
        key_ref[rows, :] = jnp.where(srow < lim, _sortable(sc_ref[rows, :]), INT_MIN)
    krow = jnp.full((1, S2_TQ), min(TOPK_MAX, (PAST_LEN + t) // 4), I32)
    v = _select_topk(key_ref, jcut_ref, krow, nblk, S2_TQ)
    for r in range(nblk):
        sel = _selected(key_ref, jcut_ref, v, r * SEL_RB, S2_TQ)
        mask_ref[r * SEL_RB:(r + 1) * SEL_RB, :] = jnp.where(sel, 1.0, 0.0).astype(BF16)


def _sample_select(scores_t, t):
    lp, r = scores_t.shape
    return pl.pallas_call(
        functools.partial(_s2_body, t=t, lp=lp),
        grid=(r // S2_TQ,),
        in_specs=[pl.BlockSpec((lp, S2_TQ), lambda i: (0, i))],
        out_specs=pl.BlockSpec((lp, S2_TQ), lambda i: (0, i)),
        out_shape=jax.ShapeDtypeStruct((lp, r), BF16),
        scratch_shapes=[pltpu.VMEM((lp, S2_TQ), I32), pltpu.VMEM((8, S2_TQ), I32)],
        compiler_params=_cparams(("parallel",)),
        name="sample_select",
    )(scores_t)


S3_GROUP = 4
S3_SLOTS = 4
S3_SEQS = LANES // 8
S3_PITCH = 24
S3_PAGE_ROWS = PAGE_SIZE * S3_PITCH


def _s3_body(pt_ref, qbd_ref, mask_ref, e_ref, knew_ref, vnew_ref, tsp_ref, tsn_ref, ck_hbm, cv_hbm, o_ref,
             buf_ref, sem, lg_ref, acc_ref, stage_ref, vpad_ref, *, n_seq, n_pages, t):
    ng = n_pages // S3_GROUP
    steps = 2 * ng
    b = pl.program_id(0)
    g = pl.program_id(1)
    st = b * steps + g
    past = n_pages * PAGE_SIZE
    rows_g = S3_GROUP * PAGE_SIZE
    buf3 = buf_ref.reshape(S3_SLOTS * S3_GROUP * PAGE_SIZE, S3_PITCH, HEAD_DIM)

    def page_copy(cache_hbm, pid, slot, j):
        dst = buf3.at[pl.ds((slot * S3_GROUP + j) * PAGE_SIZE, PAGE_SIZE), pl.ds(0, N_HEADS), :]
        return pltpu.make_async_copy(cache_hbm.at[pid], dst, sem.at[slot, j])

    def fetch(ahead):
        g2 = g + ahead
        b2 = b + g2 // steps
        g2 = g2 % steps
        slot2 = (st + ahead) % S3_SLOTS

        @pl.when((b2 < n_seq) & (g2 < ng))
        def _():
            for j in range(S3_GROUP):
                page_copy(ck_hbm, pt_ref[b2, g2 * S3_GROUP + j], slot2, j).start(priority=j % 2)

        @pl.when((b2 < n_seq) & (g2 >= ng))
        def _():
            for j in range(S3_GROUP):
                page_copy(cv_hbm, pt_ref[b2, (g2 - ng) * S3_GROUP + j], slot2, j).start(priority=j % 2)

    @pl.when(st == 0)
    def _():
        for ahead in range(S3_SLOTS - 1):
            fetch(ahead)

    fetch(S3_SLOTS - 1)
    slot = st % S3_SLOTS
    for j in range(S3_GROUP):
        page_copy(ck_hbm, 0, slot, j).wait()

    def stage():
        for j in range(S3_GROUP):
            base = (slot * S3_GROUP + j) * S3_PAGE_ROWS
            for h in range(N_HEADS):
                stage_ref[j * PAGE_SIZE:(j + 1) * PAGE_SIZE, h * HEAD_DIM:(h + 1) * HEAD_DIM] = (
                    buf_ref[pl.ds(base + h, PAGE_SIZE, stride=S3_PITCH), :].astype(BF16))

    def masked(lg, msk_rows):
        sel = jnp.dot(msk_rows, e_ref[0], preferred_element_type=F32)
        return jnp.where(sel > 0.5, lg, NEG)

    @pl.when(g < ng)
    def _():
        stage()
        row0 = pl.multiple_of(g * rows_g, rows_g)
        lg = jnp.dot(stage_ref[...], qbd_ref[0], preferred_element_type=F32) * ATTN_SCALE
        lg_ref[pl.ds(row0, rows_g), :] = masked(lg, mask_ref[pl.ds(row0, rows_g), :])

    @pl.when(g == ng - 1)
    def _():
        lg_ref[past - PAGE_SIZE:past, :] += tsp_ref[...]
        lgn = jnp.dot(knew_ref[0].astype(BF16), qbd_ref[0], preferred_element_type=F32) * ATTN_SCALE + tsn_ref[...]
        lgn = jnp.concatenate([lgn, jnp.zeros((t, LANES), F32)], axis=0)
        lg_ref[past:past + t, :] = masked(lgn, mask_ref[past:past + 2 * t, :])[:t]
        lg_ref[past + t:, :] = jnp.full((PAGE_SIZE - t, LANES), NEG, F32)

    @pl.when(g == ng)
    def _():
        lg = lg_ref[...]
        p = jnp.exp(lg - jnp.max(lg, axis=0, keepdims=True))
        lg_ref[...] = p * (1.0 / jnp.sum(p, axis=0, keepdims=True))
        acc_ref[...] = jnp.zeros_like(acc_ref)

    def probs_t(row0, npages):
        return jnp.concatenate([jnp.transpose(lg_ref[pl.ds(row0 + j * PAGE_SIZE, PAGE_SIZE), :])
                                for j in range(npages)], axis=1).astype(BF16)

    @pl.when(g >= ng)
    def _():
        stage()
        row0 = pl.multiple_of((g - ng) * rows_g, rows_g)
        acc_ref[...] += jnp.dot(probs_t(row0, S3_GROUP), stage_ref[...], preferred_element_type=F32)

    @pl.when(g == steps - 1)
    def _():
        vpad_ref[...] = jnp.zeros_like(vpad_ref)
        vpad_ref[0:t, :] = vnew_ref[0]
        acc = acc_ref[...] + jnp.dot(probs_t(past, 1), vpad_ref[...].astype(BF16), preferred_element_type=F32)
        for h in range(N_HEADS):
            o_ref[0, :, h * HEAD_DIM:(h + 1) * HEAD_DIM] = acc[h * t:(h + 1) * t, h * HEAD_DIM:(h + 1) * HEAD_DIM]


def _sample_attention(qbd, mask_t, cache_k, cache_v, k_new, v_new, ts_page, ts_new, page_table):
    n, t, _ = k_new.shape
    n_pages = page_table.shape[1]
    ng = n_pages // S3_GROUP
    lp = mask_t.shape[0]
    assert t * S3_SEQS == LANES and t * N_HEADS == LANES
    sq = np.arange(LANES)
    expand = np.stack([(sq[:, None] // t == s) & (sq[:, None] % t == sq[None, :] % t) for s in range(S3_SEQS)])

    per_seq = lambda b, g, pt: (b, 0, 0)
    const2 = lambda b, g, pt: (0, 0)
    grid_spec = pltpu.PrefetchScalarGridSpec(
        num_scalar_prefetch=1,
        grid=(n, 2 * ng),
        in_specs=[pl.BlockSpec((1, ATTN_WIDTH, LANES), per_seq),
                  pl.BlockSpec((lp, LANES), lambda b, g, pt: (0, b // S3_SEQS)),
                  pl.BlockSpec((1, LANES, LANES), lambda b, g, pt: (b % S3_SEQS, 0, 0)),
                  pl.BlockSpec((1, t, ATTN_WIDTH), per_seq), pl.BlockSpec((1, t, ATTN_WIDTH), per_seq),
                  pl.BlockSpec((PAGE_SIZE, LANES), const2), pl.BlockSpec((t, LANES), const2),
                  pl.BlockSpec(memory_space=pl.ANY), pl.BlockSpec(memory_space=pl.ANY)],
        out_specs=pl.BlockSpec((1, t, ATTN_WIDTH), per_seq),
        scratch_shapes=[pltpu.VMEM((S3_SLOTS * S3_GROUP * S3_PAGE_ROWS, HEAD_DIM), F32),
                        pltpu.SemaphoreType.DMA((S3_SLOTS, S3_GROUP)),
                        pltpu.VMEM((lp, LANES), F32), pltpu.VMEM((LANES, ATTN_WIDTH), F32),
                        pltpu.VMEM((S3_GROUP * PAGE_SIZE, ATTN_WIDTH), BF16),
                        pltpu.VMEM((PAGE_SIZE, ATTN_WIDTH), F32)],
    )
    return pl.pallas_call(
        functools.partial(_s3_body, n_seq=n, n_pages=n_pages, t=t),
        grid_spec=grid_spec,
        out_shape=jax.ShapeDtypeStruct((n, t, ATTN_WIDTH), F32),
        compiler_params=_cparams(("arbitrary", "arbitrary")),
        name="sample_attention",
    )(page_table, qbd, mask_t, jnp.asarray(expand, BF16), k_new, v_new, ts_page, ts_new, cache_k, cache_v)


W_IN_SPLITS = (POOL_WIDTH, ATTN_WIDTH, ATTN_WIDTH, ATTN_WIDTH, IDX_HEADS * IDX_DIM, IDX_DIM, IDX_HEADS, D_MODEL, D_MODEL)
W_IN_OFFS = tuple(int(v) for v in np.cumsum((0,) + W_IN_SPLITS))


def _split_w_in(w):
    w16 = w.astype(BF16)
    return dict(all=w16, gp=w16[:, W_IN_OFFS[7]:W_IN_OFFS[8]], ga=w16[:, W_IN_OFFS[8]:W_IN_OFFS[9]])


def _project(x2d, ln1, wi):
    m = x2d.shape[0]
    tm = min(m, 1024)
    w, o = wi["all"], W_IN_OFFS
    xn = _rmsnorm(x2d, ln1, BF16)
    (u,) = _matmul(xn, w, [F32], tm=tm, tn=1024, name="proj_u", col0=o[0], n=POOL_WIDTH)
    (q,) = _matmul(xn, w, [BF16], tm=tm, tn=1024, name="proj_q", col0=o[1], n=ATTN_WIDTH)
    k, k16 = _matmul_heads(xn, w, tm=min(m, 512), tn=1024, name="proj_k", col0=o[2])
    v, v16 = _matmul_heads(xn, w, tm=min(m, 512), tn=1024, name="proj_v", col0=o[3])
    (qi,) = _matmul(xn, w, [BF16], tm=tm, tn=1024, name="proj_qi", col0=o[4], n=IDX_HEADS * IDX_DIM)
    (kiwi,) = _matmul(xn, w, [F32], tm=tm, tn=LANES, name="proj_kiwi", col0=o[5], n=LANES)
    ki = kiwi[:, :IDX_DIM]
    wgt = kiwi[:, IDX_DIM:IDX_DIM + IDX_HEADS]
    return xn, u, q, k, k16, v, v16, qi, ki, wgt


def _finish(x2d, xn, pool_o, attn_o, wi, w_bp, w_ba, w_out, ln2, w1, w2, ln_f):
    m = x2d.shape[0]
    tm = min(m, 1024)
    mg = _merge(xn, pool_o, attn_o, wi["gp"], wi["ga"], w_bp, w_ba)
    (x1,) = _matmul(mg, w_out, [F32], tm=tm, tn=1024, res=x2d, name="out_proj")
    hn = _rmsnorm(x1, ln2, BF16)
    (h,) = _matmul(hn, w1, [BF16], tm=tm, tn=1024, act="relu2", name="mlp_in")
    x2 = _matmul_ktiled(h, w2, x1, tm=tm, tn=1024, tk=2048, name="mlp_out")
    return _rmsnorm(x2, ln_f, F32)


def kernel(x_prompt, x_sample, cache_k, cache_v, cache_kidx, state_pool, page_table, ln1, w_in, pool_w,
           pool_scale, rel_bias, w_branch_pool, w_branch_attn, w_out, ln2, w_mlp_in, w_mlp_out, ln_f):
    bn, s, _ = x_prompt.shape
    n, t, _ = x_sample.shape
    depth = w_in.shape[0]
    assert depth == 1, "one layer: the caches and pooling state of layer 0 are the only ones read"
    n_pages = page_table.shape[1]
    n_phys = cache_k.shape[1]

    wi = _split_w_in(w_in[0])
    pw = pool_w[0].astype(BF16)
    w_bp = w_branch_pool[0].astype(BF16)
    w_ba = w_branch_attn[0].astype(BF16)
    wo = w_out[0].astype(BF16)
    w1 = w_mlp_in[0].astype(BF16)
    w2 = w_mlp_out[0].astype(BF16)
    tb_prompt, ts_page, ts_new = _bias_tables(rel_bias, t)

    xp = x_prompt.reshape(bn * s, D_MODEL)
    xn, u, q, k, k16, v, v16, qi, ki, wgt = _project(xp, ln1[0], wi)
    pool_o = _pool_prompt(u.reshape(bn, s, POOL_WIDTH), pw, pool_scale[0]).reshape(bn * s, POOL_WIDTH)
    ki_c = ki.astype(BF16).reshape(bn, s // P1_CK, P1_CK, IDX_DIM)
    zeros = jnp.zeros_like(ki_c)
    kbd = jnp.concatenate([jnp.concatenate([ki_c, zeros], axis=-1), jnp.concatenate([zeros, ki_c], axis=-1)],
                          axis=2).reshape(bn, 2 * s, 2 * IDX_DIM)
    amask = _prompt_select(qi.reshape(bn, s, IDX_HEADS * IDX_DIM), kbd,
                           jnp.swapaxes(wgt.reshape(bn, s, IDX_HEADS), 1, 2))
    attn_o = _prompt_attention(q.reshape(bn, s, ATTN_WIDTH), k16.reshape(bn, s, ATTN_WIDTH),
                               v16.reshape(bn, s, ATTN_WIDTH), amask, tb_prompt).reshape(bn * s, ATTN_WIDTH)
    y_prompt = _finish(xp, xn, pool_o, attn_o, wi, w_bp, w_ba, wo, ln2[0], w1, w2, ln_f).reshape(bn, s, D_MODEL)
    k_prompt = k.reshape(1, bn, s, N_HEADS, HEAD_DIM)
    v_prompt = v.reshape(1, bn, s, N_HEADS, HEAD_DIM)
    kidx_prompt = ki.reshape(1, bn, s, IDX_DIM)
    pool_prompt = u.reshape(bn, s, POOL_WIDTH)[:, s - POOL_PREFIX:, :][None]

    xs = x_sample.reshape(n * t, D_MODEL)
    xn, u, q, k, k16, v, v16, qi, ki, wgt = _project(xs, ln1[0], wi)
    u3 = u.reshape(n, t, POOL_WIDTH)
    pool_o = _pool_sample(u3, state_pool[0], pw, pool_scale[0])
    qi_hq = qi.reshape(n, t, IDX_HEADS, IDX_DIM).transpose(0, 2, 1, 3).reshape(n, IDX_HEADS * t, IDX_DIM)
    wi_hq = wgt.reshape(n, t, IDX_HEADS).transpose(0, 2, 1).reshape(n, IDX_HEADS * t, 1)
    ki_new_pad = jnp.pad(jnp.swapaxes(ki.reshape(n, t, IDX_DIM), 1, 2), ((0, 0), (0, 0), (0, PAGE_SIZE - t)))
    scores = _sample_scores(qi_hq, wi_hq, jnp.swapaxes(cache_kidx[0], 1, 2), ki_new_pad, page_table)
    lp = scores.shape[-1]
    mask_t = _sample_select(scores.reshape(n * t, lp).T, t)
    q_t = q.reshape(n, t, N_HEADS, HEAD_DIM).transpose(0, 2, 3, 1)
    eye = jnp.eye(N_HEADS, dtype=BF16)
    qbd = (q_t[:, :, :, None, :] * eye[None, :, None, :, None]).reshape(n, ATTN_WIDTH, N_HEADS * t)
    attn_o = _sample_attention(qbd, mask_t, cache_k[0], cache_v[0], k.reshape(n, t, ATTN_WIDTH),
                               v.reshape(n, t, ATTN_WIDTH), ts_page, ts_new, page_table)
    attn_o = attn_o.reshape(n * t, ATTN_WIDTH).astype(BF16)
    y_sample = _finish(xs, xn, pool_o, attn_o, wi, w_bp, w_ba, wo, ln2[0], w1, w2, ln_f).reshape(n, t, D_MODEL)
    k_sample = k.reshape(1, n, t, N_HEADS, HEAD_DIM)
    v_sample = v.reshape(1, n, t, N_HEADS, HEAD_DIM)
    kidx_sample = ki.reshape(1, n, t, IDX_DIM)
    pool_sample = jnp.concatenate([state_pool[0][:, t:, :], u3], axis=1)[None]

    return (y_prompt, y_sample, k_prompt, v_prompt, kidx_prompt, pool_prompt,
            k_sample, v_sample, kidx_sample, pool_sample)
```

```python
import functools
import math

import numpy as np
import jax
import jax.numpy as jnp
from jax import lax
from jax.experimental import pallas as pl
from jax.experimental.pallas import tpu as pltpu

D_MODEL = 4096
PAST_LEN = 2048
PAGE_SIZE = 128
POOL_WIDTH = D_MODEL // 2
POOL_WINDOWS = (2, 4, 8, 16)
N_POOL_GROUPS = len(POOL_WINDOWS)
POOL_GROUP = POOL_WIDTH // N_POOL_GROUPS
POOL_PREFIX = max(POOL_WINDOWS) - 1
HEAD_DIM = 128
N_HEADS = (D_MODEL // 2) // HEAD_DIM
ATTN_WIDTH = N_HEADS * HEAD_DIM
IDX_HEADS = 32
IDX_DIM = 64
TOPK_MAX = 256
NUM_BUCKETS = 32
MAX_DISTANCE = 128
D_FF = 4 * D_MODEL
EPS = 1e-6

F32 = jnp.float32
BF16 = jnp.bfloat16
I32 = jnp.int32

LANES = 128
VMEM_LIMIT = 56 * 1024 * 1024
INT_MIN = -(2 ** 31)
NEG = -1e30
HALO = 16
PROMPT_HALO = 128
IDX_W_SCALE = (IDX_HEADS ** -0.5) * (IDX_DIM ** -0.5)
ATTN_SCALE = HEAD_DIM ** -0.5
LOG2E = math.log2(math.e)


def _cparams(sem):
    return pltpu.CompilerParams(dimension_semantics=sem, vmem_limit_bytes=VMEM_LIMIT)


def _rmsnorm_body(x_ref, g_ref, o_ref):
    x = x_ref[...]
    y = x * lax.rsqrt(jnp.mean(x * x, axis=-1, keepdims=True) + EPS)
    o_ref[...] = (y * g_ref[...]).astype(o_ref.dtype)


def _rmsnorm(x, g, out_dtype, tm=512):
    m, d = x.shape
    return pl.pallas_call(
        _rmsnorm_body,
        grid=(m // tm,),
        in_specs=[pl.BlockSpec((tm, d), lambda i: (i, 0)), pl.BlockSpec((1, d), lambda i: (0, 0))],
        out_specs=pl.BlockSpec((tm, d), lambda i: (i, 0)),
        out_shape=jax.ShapeDtypeStruct((m, d), out_dtype),
        compiler_params=_cparams(("parallel",)),
        name="rmsnorm",
    )(x, g.reshape(1, d))


def _mm_body(*refs, n_out, act, has_res):
    x_ref, w_ref = refs[0], refs[1]
    res_ref = refs[2] if has_res else None
    outs = refs[2 + int(has_res):2 + int(has_res) + n_out]
    acc = jnp.dot(x_ref[...], w_ref[...].astype(BF16), preferred_element_type=F32)
    if act == "relu2":
        acc = jnp.square(jnp.maximum(acc, 0.0))
    if has_res:
        acc = res_ref[...] + acc
    for o in outs:
        o[...] = acc.astype(o.dtype)


def _matmul(x, w, out_dtypes, *, tm, tn, act=None, res=None, name="matmul", col0=0, n=None):
    m, k = x.shape
    n = w.shape[1] if n is None else n
    assert col0 % tn == 0
    cb = col0 // tn
    in_specs = [pl.BlockSpec((tm, k), lambda i, j: (i, 0)), pl.BlockSpec((k, tn), lambda i, j: (0, cb + j))]
    args = [x, w]
    if res is not None:
        in_specs.append(pl.BlockSpec((tm, tn), lambda i, j: (i, j)))
        args.append(res)
    outs = pl.pallas_call(
        functools.partial(_mm_body, n_out=len(out_dtypes), act=act, has_res=res is not None),
        grid=(m // tm, n // tn),
        in_specs=in_specs,
        out_specs=[pl.BlockSpec((tm, tn), lambda i, j: (i, j)) for _ in out_dtypes],
        out_shape=[jax.ShapeDtypeStruct((m, n), dt) for dt in out_dtypes],
        compiler_params=_cparams(("parallel", "parallel")),
        name=name,
    )(*args)
    return outs


def _mmh_body(x_ref, w_ref, o32_ref, o16_ref, *, heads):
    j = pl.program_id(1)
    acc = jnp.dot(x_ref[...], w_ref[...], preferred_element_type=F32)
    o16_ref[...] = acc.astype(BF16)
    tm = acc.shape[0]
    for hh in range(heads):
        o32_ref[pl.ds(j * heads + hh, tm, stride=N_HEADS), :] = acc[:, hh * HEAD_DIM:(hh + 1) * HEAD_DIM]


def _matmul_heads(x, w, *, tm, tn, name, col0=0):
    m, k = x.shape
    n = N_HEADS * HEAD_DIM
    assert col0 % tn == 0
    cb = col0 // tn
    return pl.pallas_call(
        functools.partial(_mmh_body, heads=tn // HEAD_DIM),
        grid=(m // tm, n // tn),
        in_specs=[pl.BlockSpec((tm, k), lambda i, j: (i, 0)), pl.BlockSpec((k, tn), lambda i, j: (0, cb + j))],
        out_specs=[pl.BlockSpec((tm * N_HEADS, HEAD_DIM), lambda i, j: (i, 0)),
                   pl.BlockSpec((tm, tn), lambda i, j: (i, j))],
        out_shape=[jax.ShapeDtypeStruct((m * N_HEADS, HEAD_DIM), F32), jax.ShapeDtypeStruct((m, n), BF16)],
        compiler_params=_cparams(("parallel", "arbitrary")),
        name=name,
    )(x, w)


def _mmk_body(x_ref, w_ref, res_ref, o_ref, acc_ref, *, nk):
    k = pl.program_id(2)

    @pl.when(k == 0)
    def _():
        acc_ref[...] = jnp.zeros_like(acc_ref)

    acc_ref[...] += jnp.dot(x_ref[...], w_ref[...], preferred_element_type=F32)

    @pl.when(k == nk - 1)
    def _():
        o_ref[...] = res_ref[...] + acc_ref[...]


def _matmul_ktiled(x, w, res, *, tm, tn, tk, name):
    m, k = x.shape
    n = w.shape[1]
    nk = k // tk
    return pl.pallas_call(
        functools.partial(_mmk_body, nk=nk),
        grid=(m // tm, n // tn, nk),
        in_specs=[pl.BlockSpec((tm, tk), lambda i, j, kk: (i, kk)),
                  pl.BlockSpec((tk, tn), lambda i, j, kk: (kk, j)),
                  pl.BlockSpec((tm, tn), lambda i, j, kk: (i, j))],
        out_specs=pl.BlockSpec((tm, tn), lambda i, j, kk: (i, j)),
        out_shape=jax.ShapeDtypeStruct((m, n), F32),
        scratch_shapes=[pltpu.VMEM((tm, tn), F32)],
        compiler_params=_cparams(("parallel", "parallel", "arbitrary")),
        name=name,
    )(x, w, res)


def _merge_body(xn_ref, po_ref, ao_ref, wgp_ref, wga_ref, wbp_ref, wba_ref, o_ref):
    xn = xn_ref[...]
    gp = jnp.dot(xn, wgp_ref[...], preferred_element_type=F32)
    ga = jnp.dot(xn, wga_ref[...], preferred_element_type=F32)
    a = jnp.dot(po_ref[...], wbp_ref[...], preferred_element_type=F32)
    b = jnp.dot(ao_ref[...], wba_ref[...], preferred_element_type=F32)
    o_ref[...] = (jax.nn.sigmoid(gp) * a + jax.nn.sigmoid(ga) * b).astype(o_ref.dtype)


def _merge(xn, po, ao, wgp, wga, wbp, wba, *, tm=512, tn=512):
    m, d = xn.shape
    kb = po.shape[1]
    n = wgp.shape[1]
    row = lambda i, j: (i, 0)
    col = lambda i, j: (0, j)
    return pl.pallas_call(
        _merge_body,
        grid=(m // tm, n // tn),
        in_specs=[pl.BlockSpec((tm, d), row), pl.BlockSpec((tm, kb), row), pl.BlockSpec((tm, kb), row),
                  pl.BlockSpec((d, tn), col), pl.BlockSpec((d, tn), col),
                  pl.BlockSpec((kb, tn), col), pl.BlockSpec((kb, tn), col)],
        out_specs=pl.BlockSpec((tm, tn), lambda i, j: (i, j)),
        out_shape=jax.ShapeDtypeStruct((m, n), BF16),
        compiler_params=_cparams(("parallel", "parallel")),
        name="merge",
    )(xn, po, ao, wgp, wga, wbp, wba)


def _pool_body(*refs, has_halo):
    if has_halo:
        halo_ref, cur_ref, s_ref, inv_ref, pw_ref, sc_ref, o_ref = refs
        halo = jnp.where(pl.program_id(1) == 0, 0.0, halo_ref[0])
        cur = cur_ref[0]
        ext = jnp.concatenate([halo, cur], axis=0)
    else:
        ext_ref, cur_ref, s_ref, inv_ref, pw_ref, sc_ref, o_ref = refs
        ext = ext_ref[...]
        cur = cur_ref[...]
    hi = ext.astype(BF16)
    lo = (ext - hi.astype(F32)).astype(BF16)
    band = s_ref[0]
    win = (jnp.dot(band, hi, preferred_element_type=F32) + jnp.dot(band, lo, preferred_element_type=F32))
    pooled = win * inv_ref[0] - cur
    y = jnp.dot(pooled.astype(BF16), pw_ref[0], preferred_element_type=F32) * sc_ref[...]
    if has_halo:
        o_ref[0] = y.astype(o_ref.dtype)
    else:
        o_ref[...] = y.astype(o_ref.dtype)


def _pool_prompt(u, pool_w, pool_scale, *, tb=512):
    n, t, _ = u.shape
    halo = PROMPT_HALO
    band = np.zeros((N_POOL_GROUPS, tb, halo + tb), np.float32)
    inv = np.zeros((N_POOL_GROUPS, t, 1), np.float32)
    r = np.arange(tb)[:, None]
    c = np.arange(halo + tb)[None, :]
    for g, w in enumerate(POOL_WINDOWS):
        band[g] = ((c >= r + halo - w + 1) & (c <= r + halo)).astype(np.float32)
        inv[g, :, 0] = 1.0 / np.minimum(w, np.arange(t) + 1)
    hb = tb // halo
    return pl.pallas_call(
        functools.partial(_pool_body, has_halo=True),
        grid=(n, t // tb, N_POOL_GROUPS),
        in_specs=[pl.BlockSpec((1, halo, POOL_GROUP), lambda b, i, g: (b, jnp.maximum(i * hb - 1, 0), g)),
                  pl.BlockSpec((1, tb, POOL_GROUP), lambda b, i, g: (b, i, g)),
                  pl.BlockSpec((1, tb, halo + tb), lambda b, i, g: (g, 0, 0)),
                  pl.BlockSpec((1, tb, 1), lambda b, i, g: (g, i, 0)),
                  pl.BlockSpec((1, POOL_GROUP, POOL_GROUP), lambda b, i, g: (g, 0, 0)),
                  pl.BlockSpec((1, POOL_GROUP), lambda b, i, g: (0, g))],
        out_specs=pl.BlockSpec((1, tb, POOL_GROUP), lambda b, i, g: (b, i, g)),
        out_shape=jax.ShapeDtypeStruct((n, t, POOL_WIDTH), BF16),
        compiler_params=_cparams(("parallel", "arbitrary", "arbitrary")),
        name="pool_prompt",
    )(u, u, jnp.asarray(band, BF16), jnp.asarray(inv), pool_w, pool_scale.reshape(1, POOL_WIDTH))


def _pool_sample(u, prefix, pool_w, pool_scale, *, sb=16):
    n, t, _ = u.shape
    e = HALO + t
    ext = jnp.concatenate([jnp.zeros((n, HALO - POOL_PREFIX, POOL_WIDTH), F32), prefix.astype(F32), u], axis=1)
    ext = ext.reshape(n * e, POOL_WIDTH)
    band = np.zeros((N_POOL_GROUPS, sb * t, sb * e), np.float32)
    inv = np.zeros((N_POOL_GROUPS, sb * t, 1), np.float32)
    r = np.arange(sb * t)[:, None]
    c = np.arange(sb * e)[None, :]
    for g, w in enumerate(POOL_WINDOWS):
        pos = HALO + r % t
        band[g] = ((r // t == c // e) & (c % e >= pos - w + 1) & (c % e <= pos)).astype(np.float32)
        inv[g] = 1.0 / w
    out = pl.pallas_call(
        functools.partial(_pool_body, has_halo=False),
        grid=(n // sb, N_POOL_GROUPS),
        in_specs=[pl.BlockSpec((sb * e, POOL_GROUP), lambda i, g: (i, g)),
                  pl.BlockSpec((sb * t, POOL_GROUP), lambda i, g: (i, g)),
                  pl.BlockSpec((1, sb * t, sb * e), lambda i, g: (g, 0, 0)),
                  pl.BlockSpec((1, sb * t, 1), lambda i, g: (g, 0, 0)),
                  pl.BlockSpec((1, POOL_GROUP, POOL_GROUP), lambda i, g: (g, 0, 0)),
                  pl.BlockSpec((1, POOL_GROUP), lambda i, g: (0, g))],
        out_specs=pl.BlockSpec((sb * t, POOL_GROUP), lambda i, g: (i, g)),
        out_shape=jax.ShapeDtypeStruct((n * t, POOL_WIDTH), BF16),
        compiler_params=_cparams(("parallel", "arbitrary")),
        name="pool_sample",
    )(ext, u.reshape(n * t, POOL_WIDTH), jnp.asarray(band, BF16), jnp.asarray(inv), pool_w,
      pool_scale.reshape(1, POOL_WIDTH))
    return out


def _bucket_thresholds():
    d = np.arange(0, 4 * MAX_DISTANCE)
    max_exact = NUM_BUCKETS // 2
    df = np.maximum(d, 1).astype(np.float32)
    large = max_exact + (np.log(df / np.float32(max_exact)) / np.float32(math.log(MAX_DISTANCE / max_exact))
                         * np.float32(NUM_BUCKETS - max_exact)).astype(np.int32)
    bucket = np.where(d < max_exact, d, np.minimum(large, NUM_BUCKETS - 1))
    assert np.all(np.diff(bucket) >= 0) and np.all(bucket[MAX_DISTANCE + 1:] == NUM_BUCKETS - 1)
    return [int(np.argmax(bucket >= b)) for b in range(NUM_BUCKETS)]


def _bias_prompt_body(rb_ref, o_ref):
    h = pl.program_id(0)
    thr = _bucket_thresholds()
    base = lax.broadcasted_iota(I32, (LANES, LANES), 0) - lax.broadcasted_iota(I32, (LANES, LANES), 1)
    for j in range(2):
        dist = base + j * LANES
        tile = jnp.full((LANES, LANES), rb_ref[0, h], F32)
        for b in range(1, NUM_BUCKETS):
            tile = jnp.where(dist >= thr[b], rb_ref[b, h], tile)
        o_ref[0, j] = (tile - rb_ref[NUM_BUCKETS - 1, h]) * LOG2E


def _bias_sample_body(rbx_ref, page_ref, new_ref, *, t):
    thr = _bucket_thresholds()
    for o_ref, rows, off in ((page_ref, PAGE_SIZE, PAGE_SIZE), (new_ref, t, 0)):
        qq = lax.broadcasted_iota(I32, (rows, LANES), 1) % t
        dist = off + qq - lax.broadcasted_iota(I32, (rows, LANES), 0)
        tile = jnp.broadcast_to(rbx_ref[0:1, :], (rows, LANES))
        for b in range(1, NUM_BUCKETS):
            tile = jnp.where(dist >= thr[b], rbx_ref[b:b + 1, :], tile)
        o_ref[...] = tile - rbx_ref[NUM_BUCKETS - 1:NUM_BUCKETS, :]


def _bias_tables(rel_bias, t):
    rb = rel_bias.astype(F32)
    tp = pl.pallas_call(
        _bias_prompt_body,
        grid=(N_HEADS,),
        in_specs=[pl.BlockSpec(memory_space=pltpu.SMEM)],
        out_specs=pl.BlockSpec((1, 2, LANES, LANES), lambda h: (h, 0, 0, 0)),
        out_shape=jax.ShapeDtypeStruct((N_HEADS, 2, LANES, LANES), F32),
        name="bias_prompt",
    )(rb)
    rbx = jnp.repeat(rb, LANES // N_HEADS, axis=1)
    ts_page, ts_new = pl.pallas_call(
        functools.partial(_bias_sample_body, t=t),
        out_shape=[jax.ShapeDtypeStruct((PAGE_SIZE, LANES), F32), jax.ShapeDtypeStruct((t, LANES), F32)],
        name="bias_sample",
    )(rbx)
    return tp, ts_page, ts_new


SEL_RB = 128


def _sortable(score):
    bits = lax.bitcast_convert_type(score, I32)
    return bits ^ ((bits >> 31) & 0x7FFFFFFF)


def _select_topk(key_ref, jcut_ref, krow, nblk, tq, blocks_per_iter=1):
    def counts(pred):
        def body(r, acc):
            for u in range(blocks_per_iter):
                row0 = pl.multiple_of((r * blocks_per_iter + u) * SEL_RB, SEL_RB)
                hit = pred(key_ref[pl.ds(row0, SEL_RB), :], row0).astype(I32)
                acc = acc + jnp.sum(hit.reshape(SEL_RB // 8, 8, tq), axis=0)
            return acc
        acc = lax.fori_loop(0, nblk // blocks_per_iter, body, jnp.zeros((8, tq), I32))
        return jnp.sum(acc, axis=0, keepdims=True)

    def count_ge(cand):
        return counts(lambda blk, row0: blk >= cand)

    zero = jnp.zeros((1, tq), I32)
    v = jnp.where(count_ge(zero) >= krow, zero, jnp.full((1, tq), INT_MIN, I32))

    def bit_body(bi, v):
        cand = v | jnp.left_shift(jnp.int32(1), 30 - bi)
        return jnp.where(count_ge(cand) >= krow, cand, v)

    v = lax.fori_loop(0, 31, bit_body, v)

    n_gt = counts(lambda blk, row0: blk > v)
    n_ge = counts(lambda blk, row0: blk >= v)
    need = krow - n_gt
    jcut_ref[...] = jnp.full((8, tq), nblk * SEL_RB, I32)

    @pl.when(jnp.max(n_ge - krow) > 0)
    def _():
        def jbit(bi, x):
            cand = x + jnp.left_shift(jnp.int32(1), 15 - bi)

            def pred(blk, row0):
                row = row0 + lax.broadcasted_iota(I32, (SEL_RB, tq), 0)
                return (blk == v) & (row <= cand)
            return jnp.where(counts(pred) < need, cand, x)
        x = lax.fori_loop(0, 16, jbit, jnp.full((1, tq), -1, I32))
        jcut_ref[...] = jnp.broadcast_to(x + 2, (8, tq))

    return v


def _selected(key_ref, jcut_ref, v, row0, tq):
    blk = key_ref[pl.ds(row0, SEL_RB), :]
    row = row0 + lax.broadcasted_iota(I32, (SEL_RB, tq), 0)
    return (blk > v) | ((blk == v) & (row < jcut_ref[0:1, :]))


P1_TQ = 256
P1_CK = 256


def _p1_body(qi_ref, kbd_ref, wt_ref, amask_ref, key_ref, jcut_ref, acc_ref, *, s):
    i = pl.program_id(1)
    t0 = i * P1_TQ
    n_ch = (i + 1) * (P1_TQ // P1_CK)
    n_blk = (i + 1) * (P1_TQ // SEL_RB)
    tq_iota = t0 + lax.broadcasted_iota(I32, (P1_CK, P1_TQ), 1)

    def chunk_body(c, _):
        kb = kbd_ref[0, pl.ds(pl.multiple_of(c * 2 * P1_CK, 2 * P1_CK), 2 * P1_CK), :]
        for hp in range(IDX_HEADS // 2):
            d2 = lax.dot_general(kb, qi_ref[0, :, hp * LANES:(hp + 1) * LANES], (((1,), (1,)), ((), ())),
                                 preferred_element_type=F32)
            w0 = wt_ref[0, 2 * hp:2 * hp + 1, :] * IDX_W_SCALE
            w1 = wt_ref[0, 2 * hp + 1:2 * hp + 2, :] * IDX_W_SCALE
            part = w0 * jnp.maximum(d2[:P1_CK], 0.0) + w1 * jnp.maximum(d2[P1_CK:], 0.0)
            if hp == 0:
                acc_ref[...] = part
            else:
                acc_ref[...] += part
        row0 = pl.multiple_of(c * P1_CK, P1_CK)
        srow = row0 + lax.broadcasted_iota(I32, (P1_CK, P1_TQ), 0)
        key_ref[pl.ds(row0, P1_CK), :] = jnp.where(srow <= tq_iota, _sortable(acc_ref[...]), INT_MIN)
        return 0

    lax.fori_loop(0, n_ch, chunk_body, 0)

    krow = jnp.minimum(TOPK_MAX, t0 + lax.broadcasted_iota(I32, (1, P1_TQ), 1) + 1)
    v = _select_topk(key_ref, jcut_ref, krow, n_blk, P1_TQ, blocks_per_iter=P1_TQ // SEL_RB)

    def out_body(r, _):
        row0 = pl.multiple_of(r * SEL_RB, SEL_RB)
        am = jnp.where(_selected(key_ref, jcut_ref, v, row0, P1_TQ), 0.0, NEG)
        for h in range(P1_TQ // LANES):
            amask_ref[0, h * LANES:(h + 1) * LANES, pl.ds(row0, SEL_RB)] = jnp.transpose(
                am[:, h * LANES:(h + 1) * LANES])
        return 0

    lax.fori_loop(0, n_blk, out_body, 0)

    def fill_body(r, _):
        amask_ref[0, :, pl.ds(pl.multiple_of(r * SEL_RB, SEL_RB), SEL_RB)] = jnp.full((P1_TQ, SEL_RB), NEG, F32)
        return 0

    lax.fori_loop(n_blk, s // SEL_RB, fill_body, 0)


def _prompt_select(qi, kbd, wt):
    b, s, _ = qi.shape
    return pl.pallas_call(
        functools.partial(_p1_body, s=s),
        grid=(b, s // P1_TQ),
        in_specs=[pl.BlockSpec((1, P1_TQ, IDX_HEADS * IDX_DIM), lambda bb, i: (bb, i, 0)),
                  pl.BlockSpec((1, 2 * s, 2 * IDX_DIM), lambda bb, i: (bb, 0, 0)),
                  pl.BlockSpec((1, IDX_HEADS, P1_TQ), lambda bb, i: (bb, 0, i))],
        out_specs=pl.BlockSpec((1, P1_TQ, s), lambda bb, i: (bb, i, 0)),
        out_shape=jax.ShapeDtypeStruct((b, s, s), F32),
        scratch_shapes=[pltpu.VMEM((s, P1_TQ), I32), pltpu.VMEM((8, P1_TQ), I32),
                        pltpu.VMEM((P1_CK, P1_TQ), F32)],
        compiler_params=_cparams(("parallel", "arbitrary")),
        name="prompt_select",
    )(qi, kbd, wt)


P2_TQ = 512
P2_CK = 1024
P2_HEADS = 2


def _p2_body(q_ref, k_ref, v_ref, amask_ref, tb_ref, o_ref, *lg_refs, nc, i0):
    i = i0 + pl.program_id(1)
    tq = P2_TQ
    chunks = [(c * P2_CK, (c + 1) * P2_CK) for c in range(nc)]

    def lane_fold(x, op):
        out = x[:, :LANES]
        for j in range(1, x.shape[1] // LANES):
            out = op(out, x[:, j * LANES:(j + 1) * LANES])
        return out

    for hh, lg_ref in enumerate(lg_refs):
        hd = slice(hh * HEAD_DIM, (hh + 1) * HEAD_DIM)
        q = q_ref[0, :, hd]
        for lo, hi in chunks:
            sc = lax.dot_general(q, k_ref[0, lo:hi, hd], (((1,), (1,)), ((), ())), preferred_element_type=F32)
            lg_ref[:, lo:hi] = sc * (ATTN_SCALE * LOG2E) + amask_ref[0, :, lo:hi]

        for a in range(tq // LANES):
            rows = slice(a * LANES, (a + 1) * LANES)
            dcol = pl.multiple_of(i * tq + a * LANES, LANES)
            lg_ref[rows, pl.ds(dcol, LANES)] += tb_ref[hh, 0]
            if a > 0:
                scol = pl.multiple_of(i * tq + (a - 1) * LANES, LANES)
                lg_ref[rows, pl.ds(scol, LANES)] += tb_ref[hh, 1]

        @pl.when(i > 0)
        def _():
            scol = pl.multiple_of(i * tq - LANES, LANES)
            lg_ref[0:LANES, pl.ds(scol, LANES)] += tb_ref[hh, 1]

    for hh, lg_ref in enumerate(lg_refs):
        hd = slice(hh * HEAD_DIM, (hh + 1) * HEAD_DIM)
        m = jnp.full((tq, LANES), NEG, F32)
        for lo, hi in chunks:
            m = jnp.maximum(m, lane_fold(lg_ref[:, lo:hi], jnp.maximum))
        m = jnp.max(m, axis=1, keepdims=True)
        l = jnp.zeros((tq, LANES), F32)
        acc = jnp.zeros((tq, HEAD_DIM), F32)
        for lo, hi in chunks:
            p = jnp.exp2(lg_ref[:, lo:hi] - m)
            l = l + lane_fold(p, jnp.add)
            acc = acc + jnp.dot(p.astype(BF16), v_ref[0, lo:hi, hd], preferred_element_type=F32)
        o_ref[0, :, hd] = (acc / jnp.sum(l, axis=1, keepdims=True)).astype(o_ref.dtype)


def _prompt_attention(q, k, v, amask, tb):
    b, s, _ = q.shape
    wd = P2_HEADS * HEAD_DIM
    per_call = P2_CK // P2_TQ
    outs = []
    for nc in range(1, s // P2_CK + 1):
        i0 = (nc - 1) * per_call
        outs.append(pl.pallas_call(
            functools.partial(_p2_body, nc=nc, i0=i0),
            grid=(b, per_call, N_HEADS // P2_HEADS),
            in_specs=[pl.BlockSpec((1, P2_TQ, wd), lambda bb, i, h, i0=i0: (bb, i0 + i, h)),
                      pl.BlockSpec((1, nc * P2_CK, wd), lambda bb, i, h: (bb, 0, h)),
                      pl.BlockSpec((1, nc * P2_CK, wd), lambda bb, i, h: (bb, 0, h)),
                      pl.BlockSpec((1, P2_TQ, nc * P2_CK), lambda bb, i, h, i0=i0: (bb, i0 + i, 0)),
                      pl.BlockSpec((P2_HEADS, 2, LANES, LANES), lambda bb, i, h: (h, 0, 0, 0))],
            out_specs=pl.BlockSpec((1, P2_TQ, wd), lambda bb, i, h: (bb, i, h)),
            out_shape=jax.ShapeDtypeStruct((b, per_call * P2_TQ, ATTN_WIDTH), BF16),
            scratch_shapes=[pltpu.VMEM((P2_TQ, nc * P2_CK), F32) for _ in range(P2_HEADS)],
            compiler_params=_cparams(("parallel", "parallel", "arbitrary")),
            name=f"prompt_attention_{nc}",
        )(q, k, v, amask, tb))
    return jnp.concatenate(outs, axis=1)


def _s1_body(pt_ref, qi_ref, w_ref, *refs, n_pages, t):
    page_refs = refs[:n_pages + 1]
    o_ref = refs[n_pages + 1]
    qi = qi_ref[0]
    w = jnp.broadcast_to(w_ref[0] * IDX_W_SCALE, (IDX_HEADS * t, LANES))
    for p, kp_ref in enumerate(page_refs):
        d = jnp.dot(qi, kp_ref[0].astype(BF16), preferred_element_type=F32)
        wr = (w * jnp.maximum(d, 0.0)).reshape(IDX_HEADS, t, LANES)
        o_ref[0, :, p * LANES:(p + 1) * LANES] = jnp.sum(wr, axis=0)


def _sample_scores(qi_hq, wi_hq, cache_kidx, ki_new_pad, page_table):
    n, rows, _ = qi_hq.shape
    t = rows // IDX_HEADS
    n_pages = page_table.shape[1]
    lp = (n_pages + 1) * PAGE_SIZE

    def page_spec(p):
        return pl.BlockSpec((1, IDX_DIM, PAGE_SIZE), lambda b, pt: (pt[b, p], 0, 0))

    grid_spec = pltpu.PrefetchScalarGridSpec(
        num_scalar_prefetch=1,
        grid=(n,),
        in_specs=[pl.BlockSpec((1, rows, IDX_DIM), lambda b, pt: (b, 0, 0)),
                  pl.BlockSpec((1, rows, 1), lambda b, pt: (b, 0, 0))]
                 + [page_spec(p) for p in range(n_pages)]
                 + [pl.BlockSpec((1, IDX_DIM, PAGE_SIZE), lambda b, pt: (b, 0, 0))],
        out_specs=pl.BlockSpec((1, t, lp), lambda b, pt: (b, 0, 0)),
    )
    return pl.pallas_call(
        functools.partial(_s1_body, n_pages=n_pages, t=t),
        grid_spec=grid_spec,
        out_shape=jax.ShapeDtypeStruct((n, t, lp), F32),
        compiler_params=_cparams(("arbitrary",)),
        name="sample_scores",
    )(page_table, qi_hq, wi_hq, *([cache_kidx] * n_pages), ki_new_pad)


S2_TQ = 256


def _s2_body(sc_ref, mask_ref, key_ref, jcut_ref, *, t, lp):
    c0 = pl.program_id(0) * S2_TQ
    nblk = lp // SEL_RB
    lim = PAST_LEN + (c0 + lax.broadcasted_iota(I32, (SEL_RB, S2_TQ), 1)) % t + 1
    for r in range(nblk):
        rows = slice(r * SEL_RB, (r + 1) * SEL_RB)
        srow = r * SEL_RB + lax.broadcasted_iota(I32, (SEL_RB, S2_TQ), 0)
        key_ref[rows, :] = jnp.where(srow < lim, _sortable(sc_ref[rows, :]), INT_MIN)
    krow = jnp.full((1, S2_TQ), min(TOPK_MAX, (PAST_LEN + t) // 4), I32)
    v = _select_topk(key_ref, jcut_ref, krow, nblk, S2_TQ)
    for r in range(nblk):
        sel = _selected(key_ref, jcut_ref, v, r * SEL_RB, S2_TQ)
        mask_ref[r * SEL_RB:(r + 1) * SEL_RB, :] = jnp.where(sel, 1.0, 0.0).astype(BF16)


def _sample_select(scores_t, t):
    lp, r = scores_t.shape
    return pl.pallas_call(
        functools.partial(_s2_body, t=t, lp=lp),
        grid=(r // S2_TQ,),
        in_specs=[pl.BlockSpec((lp, S2_TQ), lambda i: (0, i))],
        out_specs=pl.BlockSpec((lp, S2_TQ), lambda i: (0, i)),
        out_shape=jax.ShapeDtypeStruct((lp, r), BF16),
        scratch_shapes=[pltpu.VMEM((lp, S2_TQ), I32), pltpu.VMEM((8, S2_TQ), I32)],
        compiler_params=_cparams(("parallel",)),
        name="sample_select",
    )(scores_t)


S3_GROUP = 4
S3_SLOTS = 4
S3_SEQS = LANES // 8
S3_PITCH = 24
S3_PAGE_ROWS = PAGE_SIZE * S3_PITCH


def _s3_body(pt_ref, qbd_ref, mask_ref, e_ref, knew_ref, vnew_ref, tsp_ref, tsn_ref, ck_hbm, cv_hbm, o_ref,
             buf_ref, sem, lg_ref, acc_ref, stage_ref, vpad_ref, *, n_seq, n_pages, t):
    ng = n_pages // S3_GROUP
    steps = 2 * ng
    b = pl.program_id(0)
    g = pl.program_id(1)
    st = b * steps + g
    past = n_pages * PAGE_SIZE
    rows_g = S3_GROUP * PAGE_SIZE
    buf3 = buf_ref.reshape(S3_SLOTS * S3_GROUP * PAGE_SIZE, S3_PITCH, HEAD_DIM)

    def page_copy(cache_hbm, pid, slot, j):
        dst = buf3.at[pl.ds((slot * S3_GROUP + j) * PAGE_SIZE, PAGE_SIZE), pl.ds(0, N_HEADS), :]
        return pltpu.make_async_copy(cache_hbm.at[pid], dst, sem.at[slot, j])

    def fetch(ahead):
        g2 = g + ahead
        b2 = b + g2 // steps
        g2 = g2 % steps
        slot2 = (st + ahead) % S3_SLOTS

        @pl.when((b2 < n_seq) & (g2 < ng))
        def _():
            for j in range(S3_GROUP):
                page_copy(ck_hbm, pt_ref[b2, g2 * S3_GROUP + j], slot2, j).start(priority=j % 2)

        @pl.when((b2 < n_seq) & (g2 >= ng))
        def _():
            for j in range(S3_GROUP):
                page_copy(cv_hbm, pt_ref[b2, (g2 - ng) * S3_GROUP + j], slot2, j).start(priority=j % 2)

    @pl.when(st == 0)
    def _():
        for ahead in range(S3_SLOTS - 1):
            fetch(ahead)

    fetch(S3_SLOTS - 1)
    slot = st % S3_SLOTS
    for j in range(S3_GROUP):
        page_copy(ck_hbm, 0, slot, j).wait()

    def stage():
        for j in range(S3_GROUP):
            base = (slot * S3_GROUP + j) * S3_PAGE_ROWS
            for h in range(N_HEADS):
                stage_ref[j * PAGE_SIZE:(j + 1) * PAGE_SIZE, h * HEAD_DIM:(h + 1) * HEAD_DIM] = (
                    buf_ref[pl.ds(base + h, PAGE_SIZE, stride=S3_PITCH), :].astype(BF16))

    def masked(lg, msk_rows):
        sel = jnp.dot(msk_rows, e_ref[0], preferred_element_type=F32)
        return jnp.where(sel > 0.5, lg, NEG)

    def logits(k16):
        return lax.dot_general(k16, qbd_ref[0], (((1,), (1,)), ((), ())), preferred_element_type=F32) * ATTN_SCALE

    @pl.when(g < ng)
    def _():
        stage()
        row0 = pl.multiple_of(g * rows_g, rows_g)
        lg_ref[pl.ds(row0, rows_g), :] = masked(logits(stage_ref[...]), mask_ref[pl.ds(row0, rows_g), :])

    @pl.when(g == ng - 1)
    def _():
        lg_ref[past - PAGE_SIZE:past, :] += tsp_ref[...]
        lgn = logits(knew_ref[0].astype(BF16)) + tsn_ref[...]
        lgn = jnp.concatenate([lgn, jnp.zeros((t, LANES), F32)], axis=0)
        lg_ref[past:past + t, :] = masked(lgn, mask_ref[past:past + 2 * t, :])[:t]
        lg_ref[past + t:, :] = jnp.full((PAGE_SIZE - t, LANES), NEG, F32)

    @pl.when(g == ng)
    def _():
        lg = lg_ref[...]
        p = jnp.exp(lg - jnp.max(lg, axis=0, keepdims=True))
        lg_ref[...] = p * (1.0 / jnp.sum(p, axis=0, keepdims=True))
        acc_ref[...] = jnp.zeros_like(acc_ref)

    def probs_t(row0, npages):
        return jnp.concatenate([jnp.transpose(lg_ref[pl.ds(row0 + j * PAGE_SIZE, PAGE_SIZE), :])
                                for j in range(npages)], axis=1).astype(BF16)

    @pl.when(g >= ng)
    def _():
        stage()
        row0 = pl.multiple_of((g - ng) * rows_g, rows_g)
        acc_ref[...] += jnp.dot(probs_t(row0, S3_GROUP), stage_ref[...], preferred_element_type=F32)

    @pl.when(g == steps - 1)
    def _():
        vpad_ref[...] = jnp.zeros_like(vpad_ref)
        vpad_ref[0:t, :] = vnew_ref[0]
        acc = acc_ref[...] + jnp.dot(probs_t(past, 1), vpad_ref[...].astype(BF16), preferred_element_type=F32)
        for h in range(N_HEADS):
            o_ref[0, :, h * HEAD_DIM:(h + 1) * HEAD_DIM] = acc[h * t:(h + 1) * t, h * HEAD_DIM:(h + 1) * HEAD_DIM]


def _sample_attention(qbd, mask_t, cache_k, cache_v, k_new, v_new, ts_page, ts_new, page_table):
    n, t, _ = k_new.shape
    n_pages = page_table.shape[1]
    ng = n_pages // S3_GROUP
    lp = mask_t.shape[0]
    assert t * S3_SEQS == LANES and t * N_HEADS == LANES
    sq = np.arange(LANES)
    expand = np.stack([(sq[:, None] // t == s) & (sq[:, None] % t == sq[None, :] % t) for s in range(S3_SEQS)])

    per_seq = lambda b, g, pt: (b, 0, 0)
    const2 = lambda b, g, pt: (0, 0)
    grid_spec = pltpu.PrefetchScalarGridSpec(
        num_scalar_prefetch=1,
        grid=(n, 2 * ng),
        in_specs=[pl.BlockSpec((1, LANES, ATTN_WIDTH), per_seq),
                  pl.BlockSpec((lp, LANES), lambda b, g, pt: (0, b // S3_SEQS)),
                  pl.BlockSpec((1, LANES, LANES), lambda b, g, pt: (b % S3_SEQS, 0, 0)),
                  pl.BlockSpec((1, t, ATTN_WIDTH), per_seq), pl.BlockSpec((1, t, ATTN_WIDTH), per_seq),
                  pl.BlockSpec((PAGE_SIZE, LANES), const2), pl.BlockSpec((t, LANES), const2),
                  pl.BlockSpec(memory_space=pl.ANY), pl.BlockSpec(memory_space=pl.ANY)],
        out_specs=pl.BlockSpec((1, t, ATTN_WIDTH), per_seq),
        scratch_shapes=[pltpu.VMEM((S3_SLOTS * S3_GROUP * S3_PAGE_ROWS, HEAD_DIM), F32),
                        pltpu.SemaphoreType.DMA((S3_SLOTS, S3_GROUP)),
                        pltpu.VMEM((lp, LANES), F32), pltpu.VMEM((LANES, ATTN_WIDTH), F32),
                        pltpu.VMEM((S3_GROUP * PAGE_SIZE, ATTN_WIDTH), BF16),
                        pltpu.VMEM((PAGE_SIZE, ATTN_WIDTH), F32)],
    )
    return pl.pallas_call(
        functools.partial(_s3_body, n_seq=n, n_pages=n_pages, t=t),
        grid_spec=grid_spec,
        out_shape=jax.ShapeDtypeStruct((n, t, ATTN_WIDTH), F32),
        compiler_params=_cparams(("arbitrary", "arbitrary")),
        name="sample_attention",
    )(page_table, qbd, mask_t, jnp.asarray(expand, BF16), k_new, v_new, ts_page, ts_new, cache_k, cache_v)


W_IN_SPLITS = (POOL_WIDTH, ATTN_WIDTH, ATTN_WIDTH, ATTN_WIDTH, IDX_HEADS * IDX_DIM, IDX_DIM, IDX_HEADS, D_MODEL, D_MODEL)
W_IN_OFFS = tuple(int(v) for v in np.cumsum((0,) + W_IN_SPLITS))


def _split_w_in(w):
    w16 = w.astype(BF16)
    return dict(all=w16, gp=w16[:, W_IN_OFFS[7]:W_IN_OFFS[8]], ga=w16[:, W_IN_OFFS[8]:W_IN_OFFS[9]])


def _project(x2d, ln1, wi):
    m = x2d.shape[0]
    tm = min(m, 1024)
    w, o = wi["all"], W_IN_OFFS
    xn = _rmsnorm(x2d, ln1, BF16)
    (u,) = _matmul(xn, w, [F32], tm=tm, tn=1024, name="proj_u", col0=o[0], n=POOL_WIDTH)
    (q,) = _matmul(xn, w, [BF16], tm=tm, tn=1024, name="proj_q", col0=o[1], n=ATTN_WIDTH)
    k, k16 = _matmul_heads(xn, w, tm=min(m, 512), tn=1024, name="proj_k", col0=o[2])
    v, v16 = _matmul_heads(xn, w, tm=min(m, 512), tn=1024, name="proj_v", col0=o[3])
    (qi,) = _matmul(xn, w, [BF16], tm=tm, tn=1024, name="proj_qi", col0=o[4], n=IDX_HEADS * IDX_DIM)
    (kiwi,) = _matmul(xn, w, [F32], tm=tm, tn=LANES, name="proj_kiwi", col0=o[5], n=LANES)
    ki = kiwi[:, :IDX_DIM]
    wgt = kiwi[:, IDX_DIM:IDX_DIM + IDX_HEADS]
    return xn, u, q, k, k16, v, v16, qi, ki, wgt


def _finish(x2d, xn, pool_o, attn_o, wi, w_bp, w_ba, w_out, ln2, w1, w2, ln_f):
    m = x2d.shape[0]
    tm = min(m, 1024)
    mg = _merge(xn, pool_o, attn_o, wi["gp"], wi["ga"], w_bp, w_ba)
    (x1,) = _matmul(mg, w_out, [F32], tm=tm, tn=1024, res=x2d, name="out_proj")
    hn = _rmsnorm(x1, ln2, BF16)
    (h,) = _matmul(hn, w1, [BF16], tm=tm, tn=512, act="relu2", name="mlp_in")
    x2 = _matmul_ktiled(h, w2, x1, tm=tm, tn=1024, tk=2048, name="mlp_out")
    return _rmsnorm(x2, ln_f, F32)


def kernel(x_prompt, x_sample, cache_k, cache_v, cache_kidx, state_pool, page_table, ln1, w_in, pool_w,
           pool_scale, rel_bias, w_branch_pool, w_branch_attn, w_out, ln2, w_mlp_in, w_mlp_out, ln_f):
    bn, s, _ = x_prompt.shape
    n, t, _ = x_sample.shape
    depth = w_in.shape[0]
    assert depth == 1, "one layer: the caches and pooling state of layer 0 are the only ones read"
    n_pages = page_table.shape[1]
    n_phys = cache_k.shape[1]

    wi = _split_w_in(w_in[0])
    pw = pool_w[0].astype(BF16)
    w_bp = w_branch_pool[0].astype(BF16)
    w_ba = w_branch_attn[0].astype(BF16)
    wo = w_out[0].astype(BF16)
    w1 = w_mlp_in[0]
    w2 = w_mlp_out[0].astype(BF16)
    tb_prompt, ts_page, ts_new = _bias_tables(rel_bias, t)

    xp = x_prompt.reshape(bn * s, D_MODEL)
    xn, u, q, k, k16, v, v16, qi, ki, wgt = _project(xp, ln1[0], wi)
    pool_o = _pool_prompt(u.reshape(bn, s, POOL_WIDTH), pw, pool_scale[0]).reshape(bn * s, POOL_WIDTH)
    ki_c = ki.astype(BF16).reshape(bn, s // P1_CK, P1_CK, IDX_DIM)
    zeros = jnp.zeros_like(ki_c)
    kbd = jnp.concatenate([jnp.concatenate([ki_c, zeros], axis=-1), jnp.concatenate([zeros, ki_c], axis=-1)],
                          axis=2).reshape(bn, 2 * s, 2 * IDX_DIM)
    amask = _prompt_select(qi.reshape(bn, s, IDX_HEADS * IDX_DIM), kbd,
                           jnp.swapaxes(wgt.reshape(bn, s, IDX_HEADS), 1, 2))
    attn_o = _prompt_attention(q.reshape(bn, s, ATTN_WIDTH), k16.reshape(bn, s, ATTN_WIDTH),
                               v16.reshape(bn, s, ATTN_WIDTH), amask, tb_prompt).reshape(bn * s, ATTN_WIDTH)
    y_prompt = _finish(xp, xn, pool_o, attn_o, wi, w_bp, w_ba, wo, ln2[0], w1, w2, ln_f).reshape(bn, s, D_MODEL)
    k_prompt = k.reshape(1, bn, s, N_HEADS, HEAD_DIM)
    v_prompt = v.reshape(1, bn, s, N_HEADS, HEAD_DIM)
    kidx_prompt = ki.reshape(1, bn, s, IDX_DIM)
    pool_prompt = u.reshape(bn, s, POOL_WIDTH)[:, s - POOL_PREFIX:, :][None]

    xs = x_sample.reshape(n * t, D_MODEL)
    xn, u, q, k, k16, v, v16, qi, ki, wgt = _project(xs, ln1[0], wi)
    u3 = u.reshape(n, t, POOL_WIDTH)
    pool_o = _pool_sample(u3, state_pool[0], pw, pool_scale[0])
    qi_hq = qi.reshape(n, t, IDX_HEADS, IDX_DIM).transpose(0, 2, 1, 3).reshape(n, IDX_HEADS * t, IDX_DIM)
    wi_hq = wgt.reshape(n, t, IDX_HEADS).transpose(0, 2, 1).reshape(n, IDX_HEADS * t, 1)
    ki_new_pad = jnp.pad(jnp.swapaxes(ki.reshape(n, t, IDX_DIM), 1, 2), ((0, 0), (0, 0), (0, PAGE_SIZE - t)))
    scores = _sample_scores(qi_hq, wi_hq, jnp.swapaxes(cache_kidx[0], 1, 2), ki_new_pad, page_table)
    lp = scores.shape[-1]
    mask_t = _sample_select(scores.reshape(n * t, lp).T, t)
    head_of_col = np.arange(ATTN_WIDTH) // HEAD_DIM
    diag = jnp.asarray(head_of_col[None, :] == np.arange(N_HEADS)[:, None], BF16)
    qbd = (q.reshape(n, 1, t, ATTN_WIDTH) * diag[None, :, None, :]).reshape(n, N_HEADS * t, ATTN_WIDTH)
    attn_o = _sample_attention(qbd, mask_t, cache_k[0], cache_v[0], k.reshape(n, t, ATTN_WIDTH),
                               v.reshape(n, t, ATTN_WIDTH), ts_page, ts_new, page_table)
    attn_o = attn_o.reshape(n * t, ATTN_WIDTH).astype(BF16)
    y_sample = _finish(xs, xn, pool_o, attn_o, wi, w_bp, w_ba, wo, ln2[0], w1, w2, ln_f).reshape(n, t, D_MODEL)
    k_sample = k.reshape(1, n, t, N_HEADS, HEAD_DIM)
    v_sample = v.reshape(1, n, t, N_HEADS, HEAD_DIM)
    kidx_sample = ki.reshape(1, n, t, IDX_DIM)
    pool_sample = jnp.concatenate([state_pool[0][:, t:, :], u3], axis=1)[None]

    return (y_prompt, y_sample, k_prompt, v_prompt, kidx_prompt, pool_prompt,
            k_sample, v_sample, kidx_sample, pool_sample)
```

```python
import functools
import math

import numpy as np
import jax
import jax.numpy as jnp
from jax import lax
from jax.experimental import pallas as pl
from jax.experimental.pallas import tpu as pltpu

D_MODEL = 4096
PAST_LEN = 2048
PAGE_SIZE = 128
POOL_WIDTH = D_MODEL // 2
POOL_WINDOWS = (2, 4, 8, 16)
N_POOL_GROUPS = len(POOL_WINDOWS)
POOL_GROUP = POOL_WIDTH // N_POOL_GROUPS
POOL_PREFIX = max(POOL_WINDOWS) - 1
HEAD_DIM = 128
N_HEADS = (D_MODEL // 2) // HEAD_DIM
ATTN_WIDTH = N_HEADS * HEAD_DIM
IDX_HEADS = 32
IDX_DIM = 64
TOPK_MAX = 256
NUM_BUCKETS = 32
MAX_DISTANCE = 128
D_FF = 4 * D_MODEL
EPS = 1e-6

F32 = jnp.float32
BF16 = jnp.bfloat16
I32 = jnp.int32

LANES = 128
VMEM_LIMIT = 56 * 1024 * 1024
INT_MIN = -(2 ** 31)
NEG = -1e30
HALO = 16
PROMPT_HALO = 128
IDX_W_SCALE = (IDX_HEADS ** -0.5) * (IDX_DIM ** -0.5)
ATTN_SCALE = HEAD_DIM ** -0.5
LOG2E = math.log2(math.e)


def _cparams(sem):
    return pltpu.CompilerParams(dimension_semantics=sem, vmem_limit_bytes=VMEM_LIMIT)


def _rmsnorm_body(x_ref, g_ref, o_ref):
    x = x_ref[...]
    y = x * lax.rsqrt(jnp.mean(x * x, axis=-1, keepdims=True) + EPS)
    o_ref[...] = (y * g_ref[...]).astype(o_ref.dtype)


def _rmsnorm(x, g, out_dtype, tm=512):
    m, d = x.shape
    return pl.pallas_call(
        _rmsnorm_body,
        grid=(m // tm,),
        in_specs=[pl.BlockSpec((tm, d), lambda i: (i, 0)), pl.BlockSpec((1, d), lambda i: (0, 0))],
        out_specs=pl.BlockSpec((tm, d), lambda i: (i, 0)),
        out_shape=jax.ShapeDtypeStruct((m, d), out_dtype),
        compiler_params=_cparams(("parallel",)),
        name="rmsnorm",
    )(x, g.reshape(1, d))


def _dot_w(x, w, w_rows_are_outputs):
    dims = (((1,), (1,)), ((), ())) if w_rows_are_outputs else (((1,), (0,)), ((), ()))
    return lax.dot_general(x, w.astype(BF16), dims, preferred_element_type=F32)


def _mm_body(*refs, n_out, act, has_res, w_t):
    x_ref, w_ref = refs[0], refs[1]
    res_ref = refs[2] if has_res else None
    outs = refs[2 + int(has_res):2 + int(has_res) + n_out]
    acc = _dot_w(x_ref[...], w_ref[...], w_t)
    if act == "relu2":
        acc = jnp.square(jnp.maximum(acc, 0.0))
    if has_res:
        acc = res_ref[...] + acc
    for o in outs:
        o[...] = acc.astype(o.dtype)


def _w_spec(k, tn, cb, w_t):
    if w_t:
        return pl.BlockSpec((tn, k), lambda i, j: (cb + j, 0))
    return pl.BlockSpec((k, tn), lambda i, j: (0, cb + j))


def _matmul(x, w, out_dtypes, *, tm, tn, act=None, res=None, name="matmul", col0=0, n=None, w_t=False):
    m, k = x.shape
    n = w.shape[0 if w_t else 1] if n is None else n
    assert col0 % tn == 0
    cb = col0 // tn
    in_specs = [pl.BlockSpec((tm, k), lambda i, j: (i, 0)), _w_spec(k, tn, cb, w_t)]
    args = [x, w]
    if res is not None:
        in_specs.append(pl.BlockSpec((tm, tn), lambda i, j: (i, j)))
        args.append(res)
    outs = pl.pallas_call(
        functools.partial(_mm_body, n_out=len(out_dtypes), act=act, has_res=res is not None, w_t=w_t),
        grid=(m // tm, n // tn),
        in_specs=in_specs,
        out_specs=[pl.BlockSpec((tm, tn), lambda i, j: (i, j)) for _ in out_dtypes],
        out_shape=[jax.ShapeDtypeStruct((m, n), dt) for dt in out_dtypes],
        compiler_params=_cparams(("parallel", "parallel")),
        name=name,
    )(*args)
    return outs


def _mmh_body(x_ref, w_ref, o32_ref, o16_ref, *, heads, w_t):
    j = pl.program_id(1)
    acc = _dot_w(x_ref[...], w_ref[...], w_t)
    o16_ref[...] = acc.astype(BF16)
    tm = acc.shape[0]
    for hh in range(heads):
        o32_ref[pl.ds(j * heads + hh, tm, stride=N_HEADS), :] = acc[:, hh * HEAD_DIM:(hh + 1) * HEAD_DIM]


def _matmul_heads(x, w, *, tm, tn, name, col0=0, w_t=False):
    m, k = x.shape
    n = N_HEADS * HEAD_DIM
    assert col0 % tn == 0
    cb = col0 // tn
    return pl.pallas_call(
        functools.partial(_mmh_body, heads=tn // HEAD_DIM, w_t=w_t),
        grid=(m // tm, n // tn),
        in_specs=[pl.BlockSpec((tm, k), lambda i, j: (i, 0)), _w_spec(k, tn, cb, w_t)],
        out_specs=[pl.BlockSpec((tm * N_HEADS, HEAD_DIM), lambda i, j: (i, 0)),
                   pl.BlockSpec((tm, tn), lambda i, j: (i, j))],
        out_shape=[jax.ShapeDtypeStruct((m * N_HEADS, HEAD_DIM), F32), jax.ShapeDtypeStruct((m, n), BF16)],
        compiler_params=_cparams(("parallel", "arbitrary")),
        name=name,
    )(x, w)


def _mmk_body(x_ref, w_ref, res_ref, o_ref, acc_ref, *, nk):
    k = pl.program_id(2)

    @pl.when(k == 0)
    def _():
        acc_ref[...] = jnp.zeros_like(acc_ref)

    acc_ref[...] += jnp.dot(x_ref[...], w_ref[...], preferred_element_type=F32)

    @pl.when(k == nk - 1)
    def _():
        o_ref[...] = res_ref[...] + acc_ref[...]


def _matmul_ktiled(x, w, res, *, tm, tn, tk, name):
    m, k = x.shape
    n = w.shape[1]
    nk = k // tk
    return pl.pallas_call(
        functools.partial(_mmk_body, nk=nk),
        grid=(m // tm, n // tn, nk),
        in_specs=[pl.BlockSpec((tm, tk), lambda i, j, kk: (i, kk)),
                  pl.BlockSpec((tk, tn), lambda i, j, kk: (kk, j)),
                  pl.BlockSpec((tm, tn), lambda i, j, kk: (i, j))],
        out_specs=pl.BlockSpec((tm, tn), lambda i, j, kk: (i, j)),
        out_shape=jax.ShapeDtypeStruct((m, n), F32),
        scratch_shapes=[pltpu.VMEM((tm, tn), F32)],
        compiler_params=_cparams(("parallel", "parallel", "arbitrary")),
        name=name,
    )(x, w, res)


def _merge_body(xn_ref, po_ref, ao_ref, wgp_ref, wga_ref, wbp_ref, wba_ref, o_ref):
    xn = xn_ref[...]
    gp = _dot_w(xn, wgp_ref[...], True)
    ga = _dot_w(xn, wga_ref[...], True)
    a = _dot_w(po_ref[...], wbp_ref[...], False)
    b = _dot_w(ao_ref[...], wba_ref[...], False)
    o_ref[...] = (jax.nn.sigmoid(gp) * a + jax.nn.sigmoid(ga) * b).astype(o_ref.dtype)


def _merge(xn, po, ao, wgp_t, wga_t, wbp, wba, *, tm=512, tn=512):
    m, d = xn.shape
    kb = po.shape[1]
    n = wgp_t.shape[0]
    row = lambda i, j: (i, 0)
    col = lambda i, j: (0, j)
    return pl.pallas_call(
        _merge_body,
        grid=(m // tm, n // tn),
        in_specs=[pl.BlockSpec((tm, d), row), pl.BlockSpec((tm, kb), row), pl.BlockSpec((tm, kb), row),
                  pl.BlockSpec((tn, d), lambda i, j: (j, 0)), pl.BlockSpec((tn, d), lambda i, j: (j, 0)),
                  pl.BlockSpec((kb, tn), col), pl.BlockSpec((kb, tn), col)],
        out_specs=pl.BlockSpec((tm, tn), lambda i, j: (i, j)),
        out_shape=jax.ShapeDtypeStruct((m, n), BF16),
        compiler_params=_cparams(("parallel", "parallel")),
        name="merge",
    )(xn, po, ao, wgp_t, wga_t, wbp, wba)


def _pool_body(*refs, has_halo):
    if has_halo:
        halo_ref, cur_ref, s_ref, inv_ref, pw_ref, sc_ref, o_ref = refs
        halo = jnp.where(pl.program_id(1) == 0, 0.0, halo_ref[0])
        cur = cur_ref[0]
        ext = jnp.concatenate([halo, cur], axis=0)
    else:
        ext_ref, cur_ref, s_ref, inv_ref, pw_ref, sc_ref, o_ref = refs
        ext = ext_ref[...]
        cur = cur_ref[...]
    hi = ext.astype(BF16)
    lo = (ext - hi.astype(F32)).astype(BF16)
    band = s_ref[0]
    win = (jnp.dot(band, hi, preferred_element_type=F32) + jnp.dot(band, lo, preferred_element_type=F32))
    pooled = win * inv_ref[0] - cur
    y = jnp.dot(pooled.astype(BF16), pw_ref[0], preferred_element_type=F32) * sc_ref[...]
    if has_halo:
        o_ref[0] = y.astype(o_ref.dtype)
    else:
        o_ref[...] = y.astype(o_ref.dtype)


def _pool_prompt(u, pool_w, pool_scale, *, tb=512):
    n, t, _ = u.shape
    halo = PROMPT_HALO
    band = np.zeros((N_POOL_GROUPS, tb, halo + tb), np.float32)
    inv = np.zeros((N_POOL_GROUPS, t, 1), np.float32)
    r = np.arange(tb)[:, None]
    c = np.arange(halo + tb)[None, :]
    for g, w in enumerate(POOL_WINDOWS):
        band[g] = ((c >= r + halo - w + 1) & (c <= r + halo)).astype(np.float32)
        inv[g, :, 0] = 1.0 / np.minimum(w, np.arange(t) + 1)
    hb = tb // halo
    return pl.pallas_call(
        functools.partial(_pool_body, has_halo=True),
        grid=(n, t // tb, N_POOL_GROUPS),
        in_specs=[pl.BlockSpec((1, halo, POOL_GROUP), lambda b, i, g: (b, jnp.maximum(i * hb - 1, 0), g)),
                  pl.BlockSpec((1, tb, POOL_GROUP), lambda b, i, g: (b, i, g)),
                  pl.BlockSpec((1, tb, halo + tb), lambda b, i, g: (g, 0, 0)),
                  pl.BlockSpec((1, tb, 1), lambda b, i, g: (g, i, 0)),
                  pl.BlockSpec((1, POOL_GROUP, POOL_GROUP), lambda b, i, g: (g, 0, 0)),
                  pl.BlockSpec((1, POOL_GROUP), lambda b, i, g: (0, g))],
        out_specs=pl.BlockSpec((1, tb, POOL_GROUP), lambda b, i, g: (b, i, g)),
        out_shape=jax.ShapeDtypeStruct((n, t, POOL_WIDTH), BF16),
        compiler_params=_cparams(("parallel", "arbitrary", "arbitrary")),
        name="pool_prompt",
    )(u, u, jnp.asarray(band, BF16), jnp.asarray(inv), pool_w, pool_scale.reshape(1, POOL_WIDTH))


def _pool_sample(u, prefix, pool_w, pool_scale, *, sb=16):
    n, t, _ = u.shape
    e = HALO + t
    ext = jnp.concatenate([jnp.zeros((n, HALO - POOL_PREFIX, POOL_WIDTH), F32), prefix.astype(F32), u], axis=1)
    ext = ext.reshape(n * e, POOL_WIDTH)
    band = np.zeros((N_POOL_GROUPS, sb * t, sb * e), np.float32)
    inv = np.zeros((N_POOL_GROUPS, sb * t, 1), np.float32)
    r = np.arange(sb * t)[:, None]
    c = np.arange(sb * e)[None, :]
    for g, w in enumerate(POOL_WINDOWS):
        pos = HALO + r % t
        band[g] = ((r // t == c // e) & (c % e >= pos - w + 1) & (c % e <= pos)).astype(np.float32)
        inv[g] = 1.0 / w
    out = pl.pallas_call(
        functools.partial(_pool_body, has_halo=False),
        grid=(n // sb, N_POOL_GROUPS),
        in_specs=[pl.BlockSpec((sb * e, POOL_GROUP), lambda i, g: (i, g)),
                  pl.BlockSpec((sb * t, POOL_GROUP), lambda i, g: (i, g)),
                  pl.BlockSpec((1, sb * t, sb * e), lambda i, g: (g, 0, 0)),
                  pl.BlockSpec((1, sb * t, 1), lambda i, g: (g, 0, 0)),
                  pl.BlockSpec((1, POOL_GROUP, POOL_GROUP), lambda i, g: (g, 0, 0)),
                  pl.BlockSpec((1, POOL_GROUP), lambda i, g: (0, g))],
        out_specs=pl.BlockSpec((sb * t, POOL_GROUP), lambda i, g: (i, g)),
        out_shape=jax.ShapeDtypeStruct((n * t, POOL_WIDTH), BF16),
        compiler_params=_cparams(("parallel", "arbitrary")),
        name="pool_sample",
    )(ext, u.reshape(n * t, POOL_WIDTH), jnp.asarray(band, BF16), jnp.asarray(inv), pool_w,
      pool_scale.reshape(1, POOL_WIDTH))
    return out


def _bucket_thresholds():
    d = np.arange(0, 4 * MAX_DISTANCE)
    max_exact = NUM_BUCKETS // 2
    df = np.maximum(d, 1).astype(np.float32)
    large = max_exact + (np.log(df / np.float32(max_exact)) / np.float32(math.log(MAX_DISTANCE / max_exact))
                         * np.float32(NUM_BUCKETS - max_exact)).astype(np.int32)
    bucket = np.where(d < max_exact, d, np.minimum(large, NUM_BUCKETS - 1))
    assert np.all(np.diff(bucket) >= 0) and np.all(bucket[MAX_DISTANCE + 1:] == NUM_BUCKETS - 1)
    return [int(np.argmax(bucket >= b)) for b in range(NUM_BUCKETS)]


def _bias_prompt_body(rb_ref, o_ref):
    h = pl.program_id(0)
    thr = _bucket_thresholds()
    base = lax.broadcasted_iota(I32, (LANES, LANES), 0) - lax.broadcasted_iota(I32, (LANES, LANES), 1)
    for j in range(2):
        dist = base + j * LANES
        tile = jnp.full((LANES, LANES), rb_ref[0, h], F32)
        for b in range(1, NUM_BUCKETS):
            tile = jnp.where(dist >= thr[b], rb_ref[b, h], tile)
        o_ref[0, j] = (tile - rb_ref[NUM_BUCKETS - 1, h]) * LOG2E


def _bias_sample_body(rbx_ref, page_ref, new_ref, *, t):
    thr = _bucket_thresholds()
    for o_ref, rows, off in ((page_ref, PAGE_SIZE, PAGE_SIZE), (new_ref, t, 0)):
        qq = lax.broadcasted_iota(I32, (rows, LANES), 1) % t
        dist = off + qq - lax.broadcasted_iota(I32, (rows, LANES), 0)
        tile = jnp.broadcast_to(rbx_ref[0:1, :], (rows, LANES))
        for b in range(1, NUM_BUCKETS):
            tile = jnp.where(dist >= thr[b], rbx_ref[b:b + 1, :], tile)
        o_ref[...] = tile - rbx_ref[NUM_BUCKETS - 1:NUM_BUCKETS, :]


def _bias_tables(rel_bias, t):
    rb = rel_bias.astype(F32)
    tp = pl.pallas_call(
        _bias_prompt_body,
        grid=(N_HEADS,),
        in_specs=[pl.BlockSpec(memory_space=pltpu.SMEM)],
        out_specs=pl.BlockSpec((1, 2, LANES, LANES), lambda h: (h, 0, 0, 0)),
        out_shape=jax.ShapeDtypeStruct((N_HEADS, 2, LANES, LANES), F32),
        name="bias_prompt",
    )(rb)
    rbx = jnp.repeat(rb, LANES // N_HEADS, axis=1)
    ts_page, ts_new = pl.pallas_call(
        functools.partial(_bias_sample_body, t=t),
        out_shape=[jax.ShapeDtypeStruct((PAGE_SIZE, LANES), F32), jax.ShapeDtypeStruct((t, LANES), F32)],
        name="bias_sample",
    )(rbx)
    return tp, ts_page, ts_new


SEL_RB = 128


def _sortable(score):
    bits = lax.bitcast_convert_type(score, I32)
    return bits ^ ((bits >> 31) & 0x7FFFFFFF)


def _select_topk(key_ref, jcut_ref, krow, nblk, tq, blocks_per_iter=1):
    def counts(pred):
        def body(r, acc):
            for u in range(blocks_per_iter):
                row0 = pl.multiple_of((r * blocks_per_iter + u) * SEL_RB, SEL_RB)
                hit = pred(key_ref[pl.ds(row0, SEL_RB), :], row0).astype(I32)
                acc = acc + jnp.sum(hit.reshape(SEL_RB // 8, 8, tq), axis=0)
            return acc
        acc = lax.fori_loop(0, nblk // blocks_per_iter, body, jnp.zeros((8, tq), I32))
        return jnp.sum(acc, axis=0, keepdims=True)

    def count_ge(cand):
        return counts(lambda blk, row0: blk >= cand)

    zero = jnp.zeros((1, tq), I32)
    v = jnp.where(count_ge(zero) >= krow, zero, jnp.full((1, tq), INT_MIN, I32))

    def bit_body(bi, v):
        cand = v | jnp.left_shift(jnp.int32(1), 30 - bi)
        return jnp.where(count_ge(cand) >= krow, cand, v)

    v = lax.fori_loop(0, 31, bit_body, v)

    n_gt = counts(lambda blk, row0: blk > v)
    n_ge = counts(lambda blk, row0: blk >= v)
    need = krow - n_gt
    jcut_ref[...] = jnp.full((8, tq), nblk * SEL_RB, I32)

    @pl.when(jnp.max(n_ge - krow) > 0)
    def _():
        def jbit(bi, x):
            cand = x + jnp.left_shift(jnp.int32(1), 15 - bi)

            def pred(blk, row0):
                row = row0 + lax.broadcasted_iota(I32, (SEL_RB, tq), 0)
                return (blk == v) & (row <= cand)
            return jnp.where(counts(pred) < need, cand, x)
        x = lax.fori_loop(0, 16, jbit, jnp.full((1, tq), -1, I32))
        jcut_ref[...] = jnp.broadcast_to(x + 2, (8, tq))

    return v


def _selected(key_ref, jcut_ref, v, row0, tq):
    blk = key_ref[pl.ds(row0, SEL_RB), :]
    row = row0 + lax.broadcasted_iota(I32, (SEL_RB, tq), 0)
    return (blk > v) | ((blk == v) & (row < jcut_ref[0:1, :]))


P1_TQ = 256
P1_CK = 256


def _p1_body(qi_ref, kbd_ref, wt_ref, amask_ref, key_ref, jcut_ref, acc_ref, *, s):
    i = pl.program_id(1)
    t0 = i * P1_TQ
    n_ch = (i + 1) * (P1_TQ // P1_CK)
    n_blk = (i + 1) * (P1_TQ // SEL_RB)
    tq_iota = t0 + lax.broadcasted_iota(I32, (P1_CK, P1_TQ), 1)

    def chunk_body(c, _):
        kb = kbd_ref[0, pl.ds(pl.multiple_of(c * 2 * P1_CK, 2 * P1_CK), 2 * P1_CK), :]
        for hp in range(IDX_HEADS // 2):
            d2 = lax.dot_general(kb, qi_ref[0, :, hp * LANES:(hp + 1) * LANES], (((1,), (1,)), ((), ())),
                                 preferred_element_type=F32)
            w0 = wt_ref[0, 2 * hp:2 * hp + 1, :] * IDX_W_SCALE
            w1 = wt_ref[0, 2 * hp + 1:2 * hp + 2, :] * IDX_W_SCALE
            part = w0 * jnp.maximum(d2[:P1_CK], 0.0) + w1 * jnp.maximum(d2[P1_CK:], 0.0)
            if hp == 0:
                acc_ref[...] = part
            else:
                acc_ref[...] += part
        row0 = pl.multiple_of(c * P1_CK, P1_CK)
        srow = row0 + lax.broadcasted_iota(I32, (P1_CK, P1_TQ), 0)
        key_ref[pl.ds(row0, P1_CK), :] = jnp.where(srow <= tq_iota, _sortable(acc_ref[...]), INT_MIN)
        return 0

    lax.fori_loop(0, n_ch, chunk_body, 0)

    krow = jnp.minimum(TOPK_MAX, t0 + lax.broadcasted_iota(I32, (1, P1_TQ), 1) + 1)
    v = _select_topk(key_ref, jcut_ref, krow, n_blk, P1_TQ, blocks_per_iter=P1_TQ // SEL_RB)

    def out_body(r, _):
        row0 = pl.multiple_of(r * SEL_RB, SEL_RB)
        am = jnp.where(_selected(key_ref, jcut_ref, v, row0, P1_TQ), 0.0, NEG)
        for h in range(P1_TQ // LANES):
            amask_ref[0, h * LANES:(h + 1) * LANES, pl.ds(row0, SEL_RB)] = jnp.transpose(
                am[:, h * LANES:(h + 1) * LANES])
        return 0

    lax.fori_loop(0, n_blk, out_body, 0)

    def fill_body(r, _):
        amask_ref[0, :, pl.ds(pl.multiple_of(r * SEL_RB, SEL_RB), SEL_RB)] = jnp.full((P1_TQ, SEL_RB), NEG, F32)
        return 0

    lax.fori_loop(n_blk, s // SEL_RB, fill_body, 0)


def _prompt_select(qi, kbd, wt):
    b, s, _ = qi.shape
    return pl.pallas_call(
        functools.partial(_p1_body, s=s),
        grid=(b, s // P1_TQ),
        in_specs=[pl.BlockSpec((1, P1_TQ, IDX_HEADS * IDX_DIM), lambda bb, i: (bb, i, 0)),
                  pl.BlockSpec((1, 2 * s, 2 * IDX_DIM), lambda bb, i: (bb, 0, 0)),
                  pl.BlockSpec((1, IDX_HEADS, P1_TQ), lambda bb, i: (bb, 0, i))],
        out_specs=pl.BlockSpec((1, P1_TQ, s), lambda bb, i: (bb, i, 0)),
        out_shape=jax.ShapeDtypeStruct((b, s, s), F32),
        scratch_shapes=[pltpu.VMEM((s, P1_TQ), I32), pltpu.VMEM((8, P1_TQ), I32),
                        pltpu.VMEM((P1_CK, P1_TQ), F32)],
        compiler_params=_cparams(("parallel", "arbitrary")),
        name="prompt_select",
    )(qi, kbd, wt)


P2_TQ = 512
P2_CK = 1024
P2_HEADS = 2


def _p2_body(q_ref, k_ref, v_ref, amask_ref, tb_ref, o_ref, *lg_refs, nc, i0):
    i = i0 + pl.program_id(1)
    tq = P2_TQ
    chunks = [(c * P2_CK, (c + 1) * P2_CK) for c in range(nc)]

    def lane_fold(x, op):
        out = x[:, :LANES]
        for j in range(1, x.shape[1] // LANES):
            out = op(out, x[:, j * LANES:(j + 1) * LANES])
        return out

    for hh, lg_ref in enumerate(lg_refs):
        hd = slice(hh * HEAD_DIM, (hh + 1) * HEAD_DIM)
        q = q_ref[0, :, hd]
        for lo, hi in chunks:
            sc = lax.dot_general(q, k_ref[0, lo:hi, hd], (((1,), (1,)), ((), ())), preferred_element_type=F32)
            lg_ref[:, lo:hi] = sc * (ATTN_SCALE * LOG2E) + amask_ref[0, :, lo:hi]

        for a in range(tq // LANES):
            rows = slice(a * LANES, (a + 1) * LANES)
            dcol = pl.multiple_of(i * tq + a * LANES, LANES)
            lg_ref[rows, pl.ds(dcol, LANES)] += tb_ref[hh, 0]
            if a > 0:
                scol = pl.multiple_of(i * tq + (a - 1) * LANES, LANES)
                lg_ref[rows, pl.ds(scol, LANES)] += tb_ref[hh, 1]

        @pl.when(i > 0)
        def _():
            scol = pl.multiple_of(i * tq - LANES, LANES)
            lg_ref[0:LANES, pl.ds(scol, LANES)] += tb_ref[hh, 1]

    for hh, lg_ref in enumerate(lg_refs):
        hd = slice(hh * HEAD_DIM, (hh + 1) * HEAD_DIM)
        m = jnp.full((tq, LANES), NEG, F32)
        for lo, hi in chunks:
            m = jnp.maximum(m, lane_fold(lg_ref[:, lo:hi], jnp.maximum))
        m = jnp.max(m, axis=1, keepdims=True)
        l = jnp.zeros((tq, LANES), F32)
        acc = jnp.zeros((tq, HEAD_DIM), F32)
        for lo, hi in chunks:
            p = jnp.exp2(lg_ref[:, lo:hi] - m)
            l = l + lane_fold(p, jnp.add)
            acc = acc + jnp.dot(p.astype(BF16), v_ref[0, lo:hi, hd], preferred_element_type=F32)
        o_ref[0, :, hd] = (acc / jnp.sum(l, axis=1, keepdims=True)).astype(o_ref.dtype)


def _prompt_attention(q, k, v, amask, tb):
    b, s, _ = q.shape
    wd = P2_HEADS * HEAD_DIM
    per_call = P2_CK // P2_TQ
    outs = []
    for nc in range(1, s // P2_CK + 1):
        i0 = (nc - 1) * per_call
        outs.append(pl.pallas_call(
            functools.partial(_p2_body, nc=nc, i0=i0),
            grid=(b, per_call, N_HEADS // P2_HEADS),
            in_specs=[pl.BlockSpec((1, P2_TQ, wd), lambda bb, i, h, i0=i0: (bb, i0 + i, h)),
                      pl.BlockSpec((1, nc * P2_CK, wd), lambda bb, i, h: (bb, 0, h)),
                      pl.BlockSpec((1, nc * P2_CK, wd), lambda bb, i, h: (bb, 0, h)),
                      pl.BlockSpec((1, P2_TQ, nc * P2_CK), lambda bb, i, h, i0=i0: (bb, i0 + i, 0)),
                      pl.BlockSpec((P2_HEADS, 2, LANES, LANES), lambda bb, i, h: (h, 0, 0, 0))],
            out_specs=pl.BlockSpec((1, P2_TQ, wd), lambda bb, i, h: (bb, i, h)),
            out_shape=jax.ShapeDtypeStruct((b, per_call * P2_TQ, ATTN_WIDTH), BF16),
            scratch_shapes=[pltpu.VMEM((P2_TQ, nc * P2_CK), F32) for _ in range(P2_HEADS)],
            compiler_params=_cparams(("parallel", "parallel", "arbitrary")),
            name=f"prompt_attention_{nc}",
        )(q, k, v, amask, tb))
    return jnp.concatenate(outs, axis=1)


def _s1_body(pt_ref, qi_ref, w_ref, *refs, n_pages, t):
    page_refs = refs[:n_pages + 1]
    o_ref = refs[n_pages + 1]
    qi = qi_ref[0]
    w = jnp.broadcast_to(w_ref[0] * IDX_W_SCALE, (IDX_HEADS * t, LANES))
    for p, kp_ref in enumerate(page_refs):
        d = jnp.dot(qi, kp_ref[0].astype(BF16), preferred_element_type=F32)
        wr = (w * jnp.maximum(d, 0.0)).reshape(IDX_HEADS, t, LANES)
        o_ref[0, :, p * LANES:(p + 1) * LANES] = jnp.sum(wr, axis=0)


def _sample_scores(qi_hq, wi_hq, cache_kidx, ki_new_pad, page_table):
    n, rows, _ = qi_hq.shape
    t = rows // IDX_HEADS
    n_pages = page_table.shape[1]
    lp = (n_pages + 1) * PAGE_SIZE

    def page_spec(p):
        return pl.BlockSpec((1, IDX_DIM, PAGE_SIZE), lambda b, pt: (pt[b, p], 0, 0))

    grid_spec = pltpu.PrefetchScalarGridSpec(
        num_scalar_prefetch=1,
        grid=(n,),
        in_specs=[pl.BlockSpec((1, rows, IDX_DIM), lambda b, pt: (b, 0, 0)),
                  pl.BlockSpec((1, rows, 1), lambda b, pt: (b, 0, 0))]
                 + [page_spec(p) for p in range(n_pages)]
                 + [pl.BlockSpec((1, IDX_DIM, PAGE_SIZE), lambda b, pt: (b, 0, 0))],
        out_specs=pl.BlockSpec((1, t, lp), lambda b, pt: (b, 0, 0)),
    )
    return pl.pallas_call(
        functools.partial(_s1_body, n_pages=n_pages, t=t),
        grid_spec=grid_spec,
        out_shape=jax.ShapeDtypeStruct((n, t, lp), F32),
        compiler_params=_cparams(("arbitrary",)),
        name="sample_scores",
    )(page_table, qi_hq, wi_hq, *([cache_kidx] * n_pages), ki_new_pad)


S2_TQ = 256


def _s2_body(sc_ref, mask_ref, key_ref, jcut_ref, *, t, lp):
    c0 = pl.program_id(0) * S2_TQ
    nblk = lp // SEL_RB
    lim = PAST_LEN + (c0 + lax.broadcasted_iota(I32, (SEL_RB, S2_TQ), 1)) % t + 1
    for r in range(nblk):
        rows = slice(r * SEL_RB, (r + 1) * SEL_RB)
        srow = r * SEL_RB + lax.broadcasted_iota(I32, (SEL_RB, S2_TQ), 0)
        key_ref[rows, :] = jnp.where(srow < lim, _sortable(sc_ref[rows, :]), INT_MIN)
    krow = jnp.full((1, S2_TQ), min(TOPK_MAX, (PAST_LEN + t) // 4), I32)
    v = _select_topk(key_ref, jcut_ref, krow, nblk, S2_TQ)
    for r in range(nblk):
        sel = _selected(key_ref, jcut_ref, v, r * SEL_RB, S2_TQ)
        mask_ref[r * SEL_RB:(r + 1) * SEL_RB, :] = jnp.where(sel, 1.0, 0.0).astype(BF16)


def _sample_select(scores_t, t):
    lp, r = scores_t.shape
    return pl.pallas_call(
        functools.partial(_s2_body, t=t, lp=lp),
        grid=(r // S2_TQ,),
        in_specs=[pl.BlockSpec((lp, S2_TQ), lambda i: (0, i))],
        out_specs=pl.BlockSpec((lp, S2_TQ), lambda i: (0, i)),
        out_shape=jax.ShapeDtypeStruct((lp, r), BF16),
        scratch_shapes=[pltpu.VMEM((lp, S2_TQ), I32), pltpu.VMEM((8, S2_TQ), I32)],
        compiler_params=_cparams(("parallel",)),
        name="sample_select",
    )(scores_t)


S3_GROUP = 4
S3_SLOTS = 4
S3_SEQS = LANES // 8
S3_PITCH = 24
S3_PAGE_ROWS = PAGE_SIZE * S3_PITCH


def _s3_body(pt_ref, qbd_ref, mask_ref, e_ref, knew_ref, vnew_ref, tsp_ref, tsn_ref, ck_hbm, cv_hbm, o_ref,
             buf_ref, sem, lg_ref, acc_ref, stage_ref, vpad_ref, *, n_seq, n_pages, t):
    ng = n_pages // S3_GROUP
    steps = 2 * ng
    b = pl.program_id(0)
    g = pl.program_id(1)
    st = b * steps + g
    past = n_pages * PAGE_SIZE
    rows_g = S3_GROUP * PAGE_SIZE
    buf3 = buf_ref.reshape(S3_SLOTS * S3_GROUP * PAGE_SIZE, S3_PITCH, HEAD_DIM)

    def page_copy(cache_hbm, pid, slot, j):
        dst = buf3.at[pl.ds((slot * S3_GROUP + j) * PAGE_SIZE, PAGE_SIZE), pl.ds(0, N_HEADS), :]
        return pltpu.make_async_copy(cache_hbm.at[pid], dst, sem.at[slot, j])

    def fetch(ahead):
        g2 = g + ahead
        b2 = b + g2 // steps
        g2 = g2 % steps
        slot2 = (st + ahead) % S3_SLOTS

        @pl.when((b2 < n_seq) & (g2 < ng))
        def _():
            for j in range(S3_GROUP):
                page_copy(ck_hbm, pt_ref[b2, g2 * S3_GROUP + j], slot2, j).start(priority=j % 2)

        @pl.when((b2 < n_seq) & (g2 >= ng))
        def _():
            for j in range(S3_GROUP):
                page_copy(cv_hbm, pt_ref[b2, (g2 - ng) * S3_GROUP + j], slot2, j).start(priority=j % 2)

    @pl.when(st == 0)
    def _():
        for ahead in range(S3_SLOTS - 1):
            fetch(ahead)

    fetch(S3_SLOTS - 1)
    slot = st % S3_SLOTS
    for j in range(S3_GROUP):
        page_copy(ck_hbm, 0, slot, j).wait()

    def stage():
        for j in range(S3_GROUP):
            base = (slot * S3_GROUP + j) * S3_PAGE_ROWS
            for h in range(N_HEADS):
                stage_ref[j * PAGE_SIZE:(j + 1) * PAGE_SIZE, h * HEAD_DIM:(h + 1) * HEAD_DIM] = (
                    buf_ref[pl.ds(base + h, PAGE_SIZE, stride=S3_PITCH), :].astype(BF16))

    def masked(lg, msk_rows):
        sel = jnp.dot(msk_rows, e_ref[0], preferred_element_type=F32)
        return jnp.where(sel > 0.5, lg, NEG)

    def logits(k16):
        return lax.dot_general(k16, qbd_ref[0], (((1,), (1,)), ((), ())), preferred_element_type=F32) * ATTN_SCALE

    @pl.when(g < ng)
    def _():
        stage()
        row0 = pl.multiple_of(g * rows_g, rows_g)
        lg_ref[pl.ds(row0, rows_g), :] = masked(logits(stage_ref[...]), mask_ref[pl.ds(row0, rows_g), :])

    @pl.when(g == ng - 1)
    def _():
        lg_ref[past - PAGE_SIZE:past, :] += tsp_ref[...]
        lgn = logits(knew_ref[0].astype(BF16)) + tsn_ref[...]
        lgn = jnp.concatenate([lgn, jnp.zeros((t, LANES), F32)], axis=0)
        lg_ref[past:past + t, :] = masked(lgn, mask_ref[past:past + 2 * t, :])[:t]
        lg_ref[past + t:, :] = jnp.full((PAGE_SIZE - t, LANES), NEG, F32)

    @pl.when(g == ng)
    def _():
        lg = lg_ref[...]
        p = jnp.exp(lg - jnp.max(lg, axis=0, keepdims=True))
        lg_ref[...] = p * (1.0 / jnp.sum(p, axis=0, keepdims=True))
        acc_ref[...] = jnp.zeros_like(acc_ref)

    def probs_t(row0, npages):
        return jnp.concatenate([jnp.transpose(lg_ref[pl.ds(row0 + j * PAGE_SIZE, PAGE_SIZE), :])
                                for j in range(npages)], axis=1).astype(BF16)

    @pl.when(g >= ng)
    def _():
        stage()
        row0 = pl.multiple_of((g - ng) * rows_g, rows_g)
        acc_ref[...] += jnp.dot(probs_t(row0, S3_GROUP), stage_ref[...], preferred_element_type=F32)

    @pl.when(g == steps - 1)
    def _():
        vpad_ref[...] = jnp.zeros_like(vpad_ref)
        vpad_ref[0:t, :] = vnew_ref[0]
        acc = acc_ref[...] + jnp.dot(probs_t(past, 1), vpad_ref[...].astype(BF16), preferred_element_type=F32)
        for h in range(N_HEADS):
            o_ref[0, :, h * HEAD_DIM:(h + 1) * HEAD_DIM] = acc[h * t:(h + 1) * t, h * HEAD_DIM:(h + 1) * HEAD_DIM]


def _sample_attention(qbd, mask_t, cache_k, cache_v, k_new, v_new, ts_page, ts_new, page_table):
    n, t, _ = k_new.shape
    n_pages = page_table.shape[1]
    ng = n_pages // S3_GROUP
    lp = mask_t.shape[0]
    assert t * S3_SEQS == LANES and t * N_HEADS == LANES
    sq = np.arange(LANES)
    expand = np.stack([(sq[:, None] // t == s) & (sq[:, None] % t == sq[None, :] % t) for s in range(S3_SEQS)])

    per_seq = lambda b, g, pt: (b, 0, 0)
    const2 = lambda b, g, pt: (0, 0)
    grid_spec = pltpu.PrefetchScalarGridSpec(
        num_scalar_prefetch=1,
        grid=(n, 2 * ng),
        in_specs=[pl.BlockSpec((1, LANES, ATTN_WIDTH), per_seq),
                  pl.BlockSpec((lp, LANES), lambda b, g, pt: (0, b // S3_SEQS)),
                  pl.BlockSpec((1, LANES, LANES), lambda b, g, pt: (b % S3_SEQS, 0, 0)),
                  pl.BlockSpec((1, t, ATTN_WIDTH), per_seq), pl.BlockSpec((1, t, ATTN_WIDTH), per_seq),
                  pl.BlockSpec((PAGE_SIZE, LANES), const2), pl.BlockSpec((t, LANES), const2),
                  pl.BlockSpec(memory_space=pl.ANY), pl.BlockSpec(memory_space=pl.ANY)],
        out_specs=pl.BlockSpec((1, t, ATTN_WIDTH), per_seq),
        scratch_shapes=[pltpu.VMEM((S3_SLOTS * S3_GROUP * S3_PAGE_ROWS, HEAD_DIM), F32),
                        pltpu.SemaphoreType.DMA((S3_SLOTS, S3_GROUP)),
                        pltpu.VMEM((lp, LANES), F32), pltpu.VMEM((LANES, ATTN_WIDTH), F32),
                        pltpu.VMEM((S3_GROUP * PAGE_SIZE, ATTN_WIDTH), BF16),
                        pltpu.VMEM((PAGE_SIZE, ATTN_WIDTH), F32)],
    )
    return pl.pallas_call(
        functools.partial(_s3_body, n_seq=n, n_pages=n_pages, t=t),
        grid_spec=grid_spec,
        out_shape=jax.ShapeDtypeStruct((n, t, ATTN_WIDTH), F32),
        compiler_params=_cparams(("arbitrary", "arbitrary")),
        name="sample_attention",
    )(page_table, qbd, mask_t, jnp.asarray(expand, BF16), k_new, v_new, ts_page, ts_new, cache_k, cache_v)


W_IN_SPLITS = (POOL_WIDTH, ATTN_WIDTH, ATTN_WIDTH, ATTN_WIDTH, IDX_HEADS * IDX_DIM, IDX_DIM, IDX_HEADS, D_MODEL, D_MODEL)
W_IN_OFFS = tuple(int(v) for v in np.cumsum((0,) + W_IN_SPLITS))


def _split_w_in(w):
    wt = jnp.swapaxes(w, 0, 1)
    return dict(all_t=wt, gp_t=wt[W_IN_OFFS[7]:W_IN_OFFS[8]].astype(BF16), ga_t=wt[W_IN_OFFS[8]:W_IN_OFFS[9]].astype(BF16))


def _project(x2d, ln1, wi):
    m = x2d.shape[0]
    tm = min(m, 1024)
    w, o = wi["all_t"], W_IN_OFFS
    xn = _rmsnorm(x2d, ln1, BF16)
    (u,) = _matmul(xn, w, [F32], tm=tm, tn=512, name="proj_u", col0=o[0], n=POOL_WIDTH, w_t=True)
    (q,) = _matmul(xn, w, [BF16], tm=tm, tn=512, name="proj_q", col0=o[1], n=ATTN_WIDTH, w_t=True)
    k, k16 = _matmul_heads(xn, w, tm=min(m, 512), tn=512, name="proj_k", col0=o[2], w_t=True)
    v, v16 = _matmul_heads(xn, w, tm=min(m, 512), tn=512, name="proj_v", col0=o[3], w_t=True)
    (qi,) = _matmul(xn, w, [BF16], tm=tm, tn=512, name="proj_qi", col0=o[4], n=IDX_HEADS * IDX_DIM, w_t=True)
    (kiwi,) = _matmul(xn, w, [F32], tm=tm, tn=LANES, name="proj_kiwi", col0=o[5], n=LANES, w_t=True)
    ki = kiwi[:, :IDX_DIM]
    wgt = kiwi[:, IDX_DIM:IDX_DIM + IDX_HEADS]
    return xn, u, q, k, k16, v, v16, qi, ki, wgt


def _finish(x2d, xn, pool_o, attn_o, wi, w_bp, w_ba, w_out, ln2, w1, w2, ln_f):
    m = x2d.shape[0]
    tm = min(m, 1024)
    mg = _merge(xn, pool_o, attn_o, wi["gp_t"], wi["ga_t"], w_bp, w_ba)
    (x1,) = _matmul(mg, w_out, [F32], tm=tm, tn=512, res=x2d, name="out_proj")
    hn = _rmsnorm(x1, ln2, BF16)
    (h,) = _matmul(hn, w1, [BF16], tm=tm, tn=512, act="relu2", name="mlp_in")
    x2 = _matmul_ktiled(h, w2, x1, tm=tm, tn=1024, tk=2048, name="mlp_out")
    return _rmsnorm(x2, ln_f, F32)


def kernel(x_prompt, x_sample, cache_k, cache_v, cache_kidx, state_pool, page_table, ln1, w_in, pool_w,
           pool_scale, rel_bias, w_branch_pool, w_branch_attn, w_out, ln2, w_mlp_in, w_mlp_out, ln_f):
    bn, s, _ = x_prompt.shape
    n, t, _ = x_sample.shape
    depth = w_in.shape[0]
    assert depth == 1, "one layer: the caches and pooling state of layer 0 are the only ones read"
    n_pages = page_table.shape[1]
    n_phys = cache_k.shape[1]

    wi = _split_w_in(w_in[0])
    pw = pool_w[0].astype(BF16)
    w_bp = w_branch_pool[0]
    w_ba = w_branch_attn[0]
    wo = w_out[0]
    w1 = w_mlp_in[0]
    w2 = w_mlp_out[0].astype(BF16)
    tb_prompt, ts_page, ts_new = _bias_tables(rel_bias, t)

    xp = x_prompt.reshape(bn * s, D_MODEL)
    xn, u, q, k, k16, v, v16, qi, ki, wgt = _project(xp, ln1[0], wi)
    pool_o = _pool_prompt(u.reshape(bn, s, POOL_WIDTH), pw, pool_scale[0]).reshape(bn * s, POOL_WIDTH)
    ki_c = ki.astype(BF16).reshape(bn, s // P1_CK, P1_CK, IDX_DIM)
    zeros = jnp.zeros_like(ki_c)
    kbd = jnp.concatenate([jnp.concatenate([ki_c, zeros], axis=-1), jnp.concatenate([zeros, ki_c], axis=-1)],
                          axis=2).reshape(bn, 2 * s, 2 * IDX_DIM)
    amask = _prompt_select(qi.reshape(bn, s, IDX_HEADS * IDX_DIM), kbd,
                           jnp.swapaxes(wgt.reshape(bn, s, IDX_HEADS), 1, 2))
    attn_o = _prompt_attention(q.reshape(bn, s, ATTN_WIDTH), k16.reshape(bn, s, ATTN_WIDTH),
                               v16.reshape(bn, s, ATTN_WIDTH), amask, tb_prompt).reshape(bn * s, ATTN_WIDTH)
    y_prompt = _finish(xp, xn, pool_o, attn_o, wi, w_bp, w_ba, wo, ln2[0], w1, w2, ln_f).reshape(bn, s, D_MODEL)
    k_prompt = k.reshape(1, bn, s, N_HEADS, HEAD_DIM)
    v_prompt = v.reshape(1, bn, s, N_HEADS, HEAD_DIM)
    kidx_prompt = ki.reshape(1, bn, s, IDX_DIM)
    pool_prompt = u.reshape(bn, s, POOL_WIDTH)[:, s - POOL_PREFIX:, :][None]

    xs = x_sample.reshape(n * t, D_MODEL)
    xn, u, q, k, k16, v, v16, qi, ki, wgt = _project(xs, ln1[0], wi)
    u3 = u.reshape(n, t, POOL_WIDTH)
    pool_o = _pool_sample(u3, state_pool[0], pw, pool_scale[0])
    qi_hq = qi.reshape(n, t, IDX_HEADS, IDX_DIM).transpose(0, 2, 1, 3).reshape(n, IDX_HEADS * t, IDX_DIM)
    wi_hq = wgt.reshape(n, t, IDX_HEADS).transpose(0, 2, 1).reshape(n, IDX_HEADS * t, 1)
    ki_new_pad = jnp.pad(jnp.swapaxes(ki.reshape(n, t, IDX_DIM), 1, 2), ((0, 0), (0, 0), (0, PAGE_SIZE - t)))
    scores = _sample_scores(qi_hq, wi_hq, jnp.swapaxes(cache_kidx[0], 1, 2), ki_new_pad, page_table)
    lp = scores.shape[-1]
    mask_t = _sample_select(scores.reshape(n * t, lp).T, t)
    head_of_col = np.arange(ATTN_WIDTH) // HEAD_DIM
    diag = jnp.asarray(head_of_col[None, :] == np.arange(N_HEADS)[:, None], BF16)
    qbd = (q.reshape(n, 1, t, ATTN_WIDTH) * diag[None, :, None, :]).reshape(n, N_HEADS * t, ATTN_WIDTH)
    attn_o = _sample_attention(qbd, mask_t, cache_k[0], cache_v[0], k.reshape(n, t, ATTN_WIDTH),
                               v.reshape(n, t, ATTN_WIDTH), ts_page, ts_new, page_table)
    attn_o = attn_o.reshape(n * t, ATTN_WIDTH).astype(BF16)
    y_sample = _finish(xs, xn, pool_o, attn_o, wi, w_bp, w_ba, wo, ln2[0], w1, w2, ln_f).reshape(n, t, D_MODEL)
    k_sample = k.reshape(1, n, t, N_HEADS, HEAD_DIM)
    v_sample = v.reshape(1, n, t, N_HEADS, HEAD_DIM)
    kidx_sample = ki.reshape(1, n, t, IDX_DIM)
    pool_sample = jnp.concatenate([state_pool[0][:, t:, :], u3], axis=1)[None]

    return (y_prompt, y_sample, k_prompt, v_prompt, kidx_prompt, pool_prompt,
            k_sample, v_sample, kidx_sample, pool_sample)
```

```python
import functools
import math

import numpy as np
import jax
import jax.numpy as jnp
from jax import lax
from jax.experimental import pallas as pl
from jax.experimental.pallas import tpu as pltpu

D_MODEL = 4096
PAST_LEN = 2048
PAGE_SIZE = 128
POOL_WIDTH = D_MODEL // 2
POOL_WINDOWS = (2, 4, 8, 16)
N_POOL_GROUPS = len(POOL_WINDOWS)
POOL_GROUP = POOL_WIDTH // N_POOL_GROUPS
POOL_PREFIX = max(POOL_WINDOWS) - 1
HEAD_DIM = 128
N_HEADS = (D_MODEL // 2) // HEAD_DIM
ATTN_WIDTH = N_HEADS * HEAD_DIM
IDX_HEADS = 32
IDX_DIM = 64
TOPK_MAX = 256
NUM_BUCKETS = 32
MAX_DISTANCE = 128
D_FF = 4 * D_MODEL
EPS = 1e-6

F32 = jnp.float32
BF16 = jnp.bfloat16
I32 = jnp.int32

LANES = 128
VMEM_LIMIT = 56 * 1024 * 1024
INT_MIN = -(2 ** 31)
NEG = -1e30
HALO = 16
PROMPT_HALO = 128
IDX_W_SCALE = (IDX_HEADS ** -0.5) * (IDX_DIM ** -0.5)
ATTN_SCALE = HEAD_DIM ** -0.5
LOG2E = math.log2(math.e)


def _cparams(sem):
    return pltpu.CompilerParams(dimension_semantics=sem, vmem_limit_bytes=VMEM_LIMIT)


def _rmsnorm_body(x_ref, g_ref, o_ref):
    x = x_ref[...]
    y = x * lax.rsqrt(jnp.mean(x * x, axis=-1, keepdims=True) + EPS)
    o_ref[...] = (y * g_ref[...]).astype(o_ref.dtype)


def _rmsnorm(x, g, out_dtype, tm=512):
    m, d = x.shape
    return pl.pallas_call(
        _rmsnorm_body,
        grid=(m // tm,),
        in_specs=[pl.BlockSpec((tm, d), lambda i: (i, 0)), pl.BlockSpec((1, d), lambda i: (0, 0))],
        out_specs=pl.BlockSpec((tm, d), lambda i: (i, 0)),
        out_shape=jax.ShapeDtypeStruct((m, d), out_dtype),
        compiler_params=_cparams(("parallel",)),
        name="rmsnorm",
    )(x, g.reshape(1, d))


def _mm_body(*refs, n_out, act, has_res):
    x_ref, w_ref = refs[0], refs[1]
    res_ref = refs[2] if has_res else None
    outs = refs[2 + int(has_res):2 + int(has_res) + n_out]
    acc = jnp.dot(x_ref[...], w_ref[...].astype(BF16), preferred_element_type=F32)
    if act == "relu2":
        acc = jnp.square(jnp.maximum(acc, 0.0))
    if has_res:
        acc = res_ref[...] + acc
    for o in outs:
        o[...] = acc.astype(o.dtype)


def _matmul(x, w, out_dtypes, *, tm, tn, act=None, res=None, name="matmul", col0=0, n=None):
    m, k = x.shape
    n = w.shape[1] if n is None else n
    assert col0 % tn == 0
    cb = col0 // tn
    in_specs = [pl.BlockSpec((tm, k), lambda i, j: (i, 0)), pl.BlockSpec((k, tn), lambda i, j: (0, cb + j))]
    args = [x, w]
    if res is not None:
        in_specs.append(pl.BlockSpec((tm, tn), lambda i, j: (i, j)))
        args.append(res)
    outs = pl.pallas_call(
        functools.partial(_mm_body, n_out=len(out_dtypes), act=act, has_res=res is not None),
        grid=(m // tm, n // tn),
        in_specs=in_specs,
        out_specs=[pl.BlockSpec((tm, tn), lambda i, j: (i, j)) for _ in out_dtypes],
        out_shape=[jax.ShapeDtypeStruct((m, n), dt) for dt in out_dtypes],
        compiler_params=_cparams(("parallel", "parallel")),
        name=name,
    )(*args)
    return outs


def _mmh_body(x_ref, w_ref, o32_ref, o16_ref, *, heads):
    j = pl.program_id(1)
    acc = jnp.dot(x_ref[...], w_ref[...], preferred_element_type=F32)
    o16_ref[...] = acc.astype(BF16)
    tm = acc.shape[0]
    for hh in range(heads):
        o32_ref[pl.ds(j * heads + hh, tm, stride=N_HEADS), :] = acc[:, hh * HEAD_DIM:(hh + 1) * HEAD_DIM]


def _matmul_heads(x, w, *, tm, tn, name, col0=0):
    m, k = x.shape
    n = N_HEADS * HEAD_DIM
    assert col0 % tn == 0
    cb = col0 // tn
    return pl.pallas_call(
        functools.partial(_mmh_body, heads=tn // HEAD_DIM),
        grid=(m // tm, n // tn),
        in_specs=[pl.BlockSpec((tm, k), lambda i, j: (i, 0)), pl.BlockSpec((k, tn), lambda i, j: (0, cb + j))],
        out_specs=[pl.BlockSpec((tm * N_HEADS, HEAD_DIM), lambda i, j: (i, 0)),
                   pl.BlockSpec((tm, tn), lambda i, j: (i, j))],
        out_shape=[jax.ShapeDtypeStruct((m * N_HEADS, HEAD_DIM), F32), jax.ShapeDtypeStruct((m, n), BF16)],
        compiler_params=_cparams(("parallel", "arbitrary")),
        name=name,
    )(x, w)


def _mmk_body(x_ref, w_ref, res_ref, o_ref, acc_ref, *, nk):
    k = pl.program_id(2)

    @pl.when(k == 0)
    def _():
        acc_ref[...] = jnp.zeros_like(acc_ref)

    acc_ref[...] += jnp.dot(x_ref[...], w_ref[...], preferred_element_type=F32)

    @pl.when(k == nk - 1)
    def _():
        o_ref[...] = res_ref[...] + acc_ref[...]


def _matmul_ktiled(x, w, res, *, tm, tn, tk, name):
    m, k = x.shape
    n = w.shape[1]
    nk = k // tk
    return pl.pallas_call(
        functools.partial(_mmk_body, nk=nk),
        grid=(m // tm, n // tn, nk),
        in_specs=[pl.BlockSpec((tm, tk), lambda i, j, kk: (i, kk)),
                  pl.BlockSpec((tk, tn), lambda i, j, kk: (kk, j)),
                  pl.BlockSpec((tm, tn), lambda i, j, kk: (i, j))],
        out_specs=pl.BlockSpec((tm, tn), lambda i, j, kk: (i, j)),
        out_shape=jax.ShapeDtypeStruct((m, n), F32),
        scratch_shapes=[pltpu.VMEM((tm, tn), F32)],
        compiler_params=_cparams(("parallel", "parallel", "arbitrary")),
        name=name,
    )(x, w, res)


def _merge_body(xn_ref, po_ref, ao_ref, wgp_ref, wga_ref, wbp_ref, wba_ref, o_ref):
    xn = xn_ref[...]
    gp = jnp.dot(xn, wgp_ref[...], preferred_element_type=F32)
    ga = jnp.dot(xn, wga_ref[...], preferred_element_type=F32)
    a = jnp.dot(po_ref[...], wbp_ref[...], preferred_element_type=F32)
    b = jnp.dot(ao_ref[...], wba_ref[...], preferred_element_type=F32)
    o_ref[...] = (jax.nn.sigmoid(gp) * a + jax.nn.sigmoid(ga) * b).astype(o_ref.dtype)


def _merge(xn, po, ao, wgp, wga, wbp, wba, *, tm=512, tn=512):
    m, d = xn.shape
    kb = po.shape[1]
    n = wgp.shape[1]
    row = lambda i, j: (i, 0)
    col = lambda i, j: (0, j)
    return pl.pallas_call(
        _merge_body,
        grid=(m // tm, n // tn),
        in_specs=[pl.BlockSpec((tm, d), row), pl.BlockSpec((tm, kb), row), pl.BlockSpec((tm, kb), row),
                  pl.BlockSpec((d, tn), col), pl.BlockSpec((d, tn), col),
                  pl.BlockSpec((kb, tn), col), pl.BlockSpec((kb, tn), col)],
        out_specs=pl.BlockSpec((tm, tn), lambda i, j: (i, j)),
        out_shape=jax.ShapeDtypeStruct((m, n), BF16),
        compiler_params=_cparams(("parallel", "parallel")),
        name="merge",
    )(xn, po, ao, wgp, wga, wbp, wba)


def _pool_body(*refs, has_halo):
    if has_halo:
        halo_ref, cur_ref, s_ref, inv_ref, pw_ref, sc_ref, o_ref = refs
        halo = jnp.where(pl.program_id(2) == 0, 0.0, halo_ref[0])
        cur = cur_ref[0]
        ext = jnp.concatenate([halo, cur], axis=0)
    else:
        ext_ref, cur_ref, s_ref, inv_ref, pw_ref, sc_ref, o_ref = refs
        ext = ext_ref[...]
        cur = cur_ref[...]
    hi = ext.astype(BF16)
    lo = (ext - hi.astype(F32)).astype(BF16)
    band = s_ref[0]
    win = (jnp.dot(band, hi, preferred_element_type=F32) + jnp.dot(band, lo, preferred_element_type=F32))
    pooled = win * inv_ref[0] - cur
    y = jnp.dot(pooled.astype(BF16), pw_ref[0], preferred_element_type=F32) * sc_ref[...]
    if has_halo:
        o_ref[0] = y.astype(o_ref.dtype)
    else:
        o_ref[...] = y.astype(o_ref.dtype)


def _pool_prompt(u, pool_w, pool_scale, *, tb=512):
    n, t, _ = u.shape
    halo = PROMPT_HALO
    band = np.zeros((N_POOL_GROUPS, tb, halo + tb), np.float32)
    inv = np.zeros((N_POOL_GROUPS, t, 1), np.float32)
    r = np.arange(tb)[:, None]
    c = np.arange(halo + tb)[None, :]
    for g, w in enumerate(POOL_WINDOWS):
        band[g] = ((c >= r + halo - w + 1) & (c <= r + halo)).astype(np.float32)
        inv[g, :, 0] = 1.0 / np.minimum(w, np.arange(t) + 1)
    hb = tb // halo
    return pl.pallas_call(
        functools.partial(_pool_body, has_halo=True),
        grid=(N_POOL_GROUPS, n, t // tb),
        in_specs=[pl.BlockSpec((1, halo, POOL_GROUP), lambda g, b, i: (b, jnp.maximum(i * hb - 1, 0), g)),
                  pl.BlockSpec((1, tb, POOL_GROUP), lambda g, b, i: (b, i, g)),
                  pl.BlockSpec((1, tb, halo + tb), lambda g, b, i: (g, 0, 0)),
                  pl.BlockSpec((1, tb, 1), lambda g, b, i: (g, i, 0)),
                  pl.BlockSpec((1, POOL_GROUP, POOL_GROUP), lambda g, b, i: (g, 0, 0)),
                  pl.BlockSpec((1, POOL_GROUP), lambda g, b, i: (0, g))],
        out_specs=pl.BlockSpec((1, tb, POOL_GROUP), lambda g, b, i: (b, i, g)),
        out_shape=jax.ShapeDtypeStruct((n, t, POOL_WIDTH), BF16),
        compiler_params=_cparams(("arbitrary", "parallel", "arbitrary")),
        name="pool_prompt",
    )(u, u, jnp.asarray(band, BF16), jnp.asarray(inv), pool_w, pool_scale.reshape(1, POOL_WIDTH))


def _pool_sample(u, prefix, pool_w, pool_scale, *, sb=16):
    n, t, _ = u.shape
    e = HALO + t
    ext = jnp.concatenate([jnp.zeros((n, HALO - POOL_PREFIX, POOL_WIDTH), F32), prefix.astype(F32), u], axis=1)
    ext = ext.reshape(n * e, POOL_WIDTH)
    band = np.zeros((N_POOL_GROUPS, sb * t, sb * e), np.float32)
    inv = np.zeros((N_POOL_GROUPS, sb * t, 1), np.float32)
    r = np.arange(sb * t)[:, None]
    c = np.arange(sb * e)[None, :]
    for g, w in enumerate(POOL_WINDOWS):
        pos = HALO + r % t
        band[g] = ((r // t == c // e) & (c % e >= pos - w + 1) & (c % e <= pos)).astype(np.float32)
        inv[g] = 1.0 / w
    out = pl.pallas_call(
        functools.partial(_pool_body, has_halo=False),
        grid=(n // sb, N_POOL_GROUPS),
        in_specs=[pl.BlockSpec((sb * e, POOL_GROUP), lambda i, g: (i, g)),
                  pl.BlockSpec((sb * t, POOL_GROUP), lambda i, g: (i, g)),
                  pl.BlockSpec((1, sb * t, sb * e), lambda i, g: (g, 0, 0)),
                  pl.BlockSpec((1, sb * t, 1), lambda i, g: (g, 0, 0)),
                  pl.BlockSpec((1, POOL_GROUP, POOL_GROUP), lambda i, g: (g, 0, 0)),
                  pl.BlockSpec((1, POOL_GROUP), lambda i, g: (0, g))],
        out_specs=pl.BlockSpec((sb * t, POOL_GROUP), lambda i, g: (i, g)),
        out_shape=jax.ShapeDtypeStruct((n * t, POOL_WIDTH), BF16),
        compiler_params=_cparams(("parallel", "arbitrary")),
        name="pool_sample",
    )(ext, u.reshape(n * t, POOL_WIDTH), jnp.asarray(band, BF16), jnp.asarray(inv), pool_w,
      pool_scale.reshape(1, POOL_WIDTH))
    return out


def _bucket_thresholds():
    d = np.arange(0, 4 * MAX_DISTANCE)
    max_exact = NUM_BUCKETS // 2
    df = np.maximum(d, 1).astype(np.float32)
    large = max_exact + (np.log(df / np.float32(max_exact)) / np.float32(math.log(MAX_DISTANCE / max_exact))
                         * np.float32(NUM_BUCKETS - max_exact)).astype(np.int32)
    bucket = np.where(d < max_exact, d, np.minimum(large, NUM_BUCKETS - 1))
    assert np.all(np.diff(bucket) >= 0) and np.all(bucket[MAX_DISTANCE + 1:] == NUM_BUCKETS - 1)
    return [int(np.argmax(bucket >= b)) for b in range(NUM_BUCKETS)]


def _bias_prompt_body(rb_ref, o_ref):
    h = pl.program_id(0)
    thr = _bucket_thresholds()
    base = lax.broadcasted_iota(I32, (LANES, LANES), 0) - lax.broadcasted_iota(I32, (LANES, LANES), 1)
    for j in range(2):
        dist = base + j * LANES
        tile = jnp.full((LANES, LANES), rb_ref[0, h], F32)
        for b in range(1, NUM_BUCKETS):
            tile = jnp.where(dist >= thr[b], rb_ref[b, h], tile)
        o_ref[0, j] = (tile - rb_ref[NUM_BUCKETS - 1, h]) * LOG2E


def _bias_sample_body(rbx_ref, page_ref, new_ref, *, t):
    thr = _bucket_thresholds()
    for o_ref, rows, off in ((page_ref, PAGE_SIZE, PAGE_SIZE), (new_ref, t, 0)):
        qq = lax.broadcasted_iota(I32, (rows, LANES), 1) % t
        dist = off + qq - lax.broadcasted_iota(I32, (rows, LANES), 0)
        tile = jnp.broadcast_to(rbx_ref[0:1, :], (rows, LANES))
        for b in range(1, NUM_BUCKETS):
            tile = jnp.where(dist >= thr[b], rbx_ref[b:b + 1, :], tile)
        o_ref[...] = tile - rbx_ref[NUM_BUCKETS - 1:NUM_BUCKETS, :]


def _bias_tables(rel_bias, t):
    rb = rel_bias.astype(F32)
    tp = pl.pallas_call(
        _bias_prompt_body,
        grid=(N_HEADS,),
        in_specs=[pl.BlockSpec(memory_space=pltpu.SMEM)],
        out_specs=pl.BlockSpec((1, 2, LANES, LANES), lambda h: (h, 0, 0, 0)),
        out_shape=jax.ShapeDtypeStruct((N_HEADS, 2, LANES, LANES), F32),
        name="bias_prompt",
    )(rb)
    rbx = jnp.repeat(rb, LANES // N_HEADS, axis=1)
    ts_page, ts_new = pl.pallas_call(
        functools.partial(_bias_sample_body, t=t),
        out_shape=[jax.ShapeDtypeStruct((PAGE_SIZE, LANES), F32), jax.ShapeDtypeStruct((t, LANES), F32)],
        name="bias_sample",
    )(rbx)
    return tp, ts_page, ts_new


SEL_RB = 128


def _sortable(score):
    bits = lax.bitcast_convert_type(score, I32)
    return bits ^ ((bits >> 31) & 0x7FFFFFFF)


def _select_topk(key_ref, jcut_ref, krow, nblk, tq, blocks_per_iter=1):
    def counts(pred):
        def body(r, acc):
            for u in range(blocks_per_iter):
                row0 = pl.multiple_of((r * blocks_per_iter + u) * SEL_RB, SEL_RB)
                hit = pred(key_ref[pl.ds(row0, SEL_RB), :], row0).astype(I32)
                acc = acc + jnp.sum(hit.reshape(SEL_RB // 8, 8, tq), axis=0)
            return acc
        acc = lax.fori_loop(0, nblk // blocks_per_iter, body, jnp.zeros((8, tq), I32))
        return jnp.sum(acc, axis=0, keepdims=True)

    def count_ge(cand):
        return counts(lambda blk, row0: blk >= cand)

    zero = jnp.zeros((1, tq), I32)
    v = jnp.where(count_ge(zero) >= krow, zero, jnp.full((1, tq), INT_MIN, I32))

    def bit_body(bi, v):
        cand = v | jnp.left_shift(jnp.int32(1), 30 - bi)
        return jnp.where(count_ge(cand) >= krow, cand, v)

    v = lax.fori_loop(0, 31, bit_body, v)

    n_gt = counts(lambda blk, row0: blk > v)
    n_ge = counts(lambda blk, row0: blk >= v)
    need = krow - n_gt
    jcut_ref[...] = jnp.full((8, tq), nblk * SEL_RB, I32)

    @pl.when(jnp.max(n_ge - krow) > 0)
    def _():
        def jbit(bi, x):
            cand = x + jnp.left_shift(jnp.int32(1), 15 - bi)

            def pred(blk, row0):
                row = row0 + lax.broadcasted_iota(I32, (SEL_RB, tq), 0)
                return (blk == v) & (row <= cand)
            return jnp.where(counts(pred) < need, cand, x)
        x = lax.fori_loop(0, 16, jbit, jnp.full((1, tq), -1, I32))
        jcut_ref[...] = jnp.broadcast_to(x + 2, (8, tq))

    return v


def _selected(key_ref, jcut_ref, v, row0, tq):
    blk = key_ref[pl.ds(row0, SEL_RB), :]
    row = row0 + lax.broadcasted_iota(I32, (SEL_RB, tq), 0)
    return (blk > v) | ((blk == v) & (row < jcut_ref[0:1, :]))


P1_TQ = 256
P1_CK = 256


def _p1_body(qi_ref, kbd_ref, wt_ref, amask_ref, key_ref, jcut_ref, acc_ref, *, s):
    i = pl.program_id(1)
    t0 = i * P1_TQ
    n_ch = (i + 1) * (P1_TQ // P1_CK)
    n_blk = (i + 1) * (P1_TQ // SEL_RB)
    tq_iota = t0 + lax.broadcasted_iota(I32, (P1_CK, P1_TQ), 1)

    def chunk_body(c, _):
        kb = kbd_ref[0, pl.ds(pl.multiple_of(c * 2 * P1_CK, 2 * P1_CK), 2 * P1_CK), :]
        for hp in range(IDX_HEADS // 2):
            d2 = lax.dot_general(kb, qi_ref[0, :, hp * LANES:(hp + 1) * LANES], (((1,), (1,)), ((), ())),
                                 preferred_element_type=F32)
            w0 = wt_ref[0, 2 * hp:2 * hp + 1, :] * IDX_W_SCALE
            w1 = wt_ref[0, 2 * hp + 1:2 * hp + 2, :] * IDX_W_SCALE
            part = w0 * jnp.maximum(d2[:P1_CK], 0.0) + w1 * jnp.maximum(d2[P1_CK:], 0.0)
            if hp == 0:
                acc_ref[...] = part
            else:
                acc_ref[...] += part
        row0 = pl.multiple_of(c * P1_CK, P1_CK)
        srow = row0 + lax.broadcasted_iota(I32, (P1_CK, P1_TQ), 0)
        key_ref[pl.ds(row0, P1_CK), :] = jnp.where(srow <= tq_iota, _sortable(acc_ref[...]), INT_MIN)
        return 0

    lax.fori_loop(0, n_ch, chunk_body, 0)

    krow = jnp.minimum(TOPK_MAX, t0 + lax.broadcasted_iota(I32, (1, P1_TQ), 1) + 1)
    v = _select_topk(key_ref, jcut_ref, krow, n_blk, P1_TQ, blocks_per_iter=P1_TQ // SEL_RB)

    def out_body(r, _):
        row0 = pl.multiple_of(r * SEL_RB, SEL_RB)
        am = jnp.where(_selected(key_ref, jcut_ref, v, row0, P1_TQ), 0.0, NEG)
        for h in range(P1_TQ // LANES):
            amask_ref[0, h * LANES:(h + 1) * LANES, pl.ds(row0, SEL_RB)] = jnp.transpose(
                am[:, h * LANES:(h + 1) * LANES])
        return 0

    lax.fori_loop(0, n_blk, out_body, 0)

    def fill_body(r, _):
        amask_ref[0, :, pl.ds(pl.multiple_of(r * SEL_RB, SEL_RB), SEL_RB)] = jnp.full((P1_TQ, SEL_RB), NEG, F32)
        return 0

    lax.fori_loop(n_blk, s // SEL_RB, fill_body, 0)


def _prompt_select(qi, kbd, wt):
    b, s, _ = qi.shape
    return pl.pallas_call(
        functools.partial(_p1_body, s=s),
        grid=(b, s // P1_TQ),
        in_specs=[pl.BlockSpec((1, P1_TQ, IDX_HEADS * IDX_DIM), lambda bb, i: (bb, i, 0)),
                  pl.BlockSpec((1, 2 * s, 2 * IDX_DIM), lambda bb, i: (bb, 0, 0)),
                  pl.BlockSpec((1, IDX_HEADS, P1_TQ), lambda bb, i: (bb, 0, i))],
        out_specs=pl.BlockSpec((1, P1_TQ, s), lambda bb, i: (bb, i, 0)),
        out_shape=jax.ShapeDtypeStruct((b, s, s), F32),
        scratch_shapes=[pltpu.VMEM((s, P1_TQ), I32), pltpu.VMEM((8, P1_TQ), I32),
                        pltpu.VMEM((P1_CK, P1_TQ), F32)],
        compiler_params=_cparams(("parallel", "arbitrary")),
        name="prompt_select",
    )(qi, kbd, wt)


P2_TQ = 512
P2_CK = 1024
P2_HEADS = 2


def _p2_body(q_ref, k_ref, v_ref, amask_ref, tb_ref, o_ref, *lg_refs, nc, i0):
    i = i0 + pl.program_id(1)
    tq = P2_TQ
    chunks = [(c * P2_CK, (c + 1) * P2_CK) for c in range(nc)]

    def lane_fold(x, op):
        out = x[:, :LANES]
        for j in range(1, x.shape[1] // LANES):
            out = op(out, x[:, j * LANES:(j + 1) * LANES])
        return out

    for hh, lg_ref in enumerate(lg_refs):
        hd = slice(hh * HEAD_DIM, (hh + 1) * HEAD_DIM)
        q = q_ref[0, :, hd]
        for lo, hi in chunks:
            sc = lax.dot_general(q, k_ref[0, lo:hi, hd], (((1,), (1,)), ((), ())), preferred_element_type=F32)
            lg_ref[:, lo:hi] = sc * (ATTN_SCALE * LOG2E) + amask_ref[0, :, lo:hi]

        for a in range(tq // LANES):
            rows = slice(a * LANES, (a + 1) * LANES)
            dcol = pl.multiple_of(i * tq + a * LANES, LANES)
            lg_ref[rows, pl.ds(dcol, LANES)] += tb_ref[hh, 0]
            if a > 0:
                scol = pl.multiple_of(i * tq + (a - 1) * LANES, LANES)
                lg_ref[rows, pl.ds(scol, LANES)] += tb_ref[hh, 1]

        @pl.when(i > 0)
        def _():
            scol = pl.multiple_of(i * tq - LANES, LANES)
            lg_ref[0:LANES, pl.ds(scol, LANES)] += tb_ref[hh, 1]

    for hh, lg_ref in enumerate(lg_refs):
        hd = slice(hh * HEAD_DIM, (hh + 1) * HEAD_DIM)
        m = jnp.full((tq, LANES), NEG, F32)
        for lo, hi in chunks:
            m = jnp.maximum(m, lane_fold(lg_ref[:, lo:hi], jnp.maximum))
        m = jnp.max(m, axis=1, keepdims=True)
        l = jnp.zeros((tq, LANES), F32)
        acc = jnp.zeros((tq, HEAD_DIM), F32)
        for lo, hi in chunks:
            p = jnp.exp2(lg_ref[:, lo:hi] - m)
            l = l + lane_fold(p, jnp.add)
            acc = acc + jnp.dot(p.astype(BF16), v_ref[0, lo:hi, hd], preferred_element_type=F32)
        o_ref[0, :, hd] = (acc / jnp.sum(l, axis=1, keepdims=True)).astype(o_ref.dtype)


def _prompt_attention(q, k, v, amask, tb):
    b, s, _ = q.shape
    wd = P2_HEADS * HEAD_DIM
    per_call = P2_CK // P2_TQ
    outs = []
    for nc in range(1, s // P2_CK + 1):
        i0 = (nc - 1) * per_call
        outs.append(pl.pallas_call(
            functools.partial(_p2_body, nc=nc, i0=i0),
            grid=(b, per_call, N_HEADS // P2_HEADS),
            in_specs=[pl.BlockSpec((1, P2_TQ, wd), lambda bb, i, h, i0=i0: (bb, i0 + i, h)),
                      pl.BlockSpec((1, nc * P2_CK, wd), lambda bb, i, h: (bb, 0, h)),
                      pl.BlockSpec((1, nc * P2_CK, wd), lambda bb, i, h: (bb, 0, h)),
                      pl.BlockSpec((1, P2_TQ, nc * P2_CK), lambda bb, i, h, i0=i0: (bb, i0 + i, 0)),
                      pl.BlockSpec((P2_HEADS, 2, LANES, LANES), lambda bb, i, h: (h, 0, 0, 0))],
            out_specs=pl.BlockSpec((1, P2_TQ, wd), lambda bb, i, h: (bb, i, h)),
            out_shape=jax.ShapeDtypeStruct((b, per_call * P2_TQ, ATTN_WIDTH), BF16),
            scratch_shapes=[pltpu.VMEM((P2_TQ, nc * P2_CK), F32) for _ in range(P2_HEADS)],
            compiler_params=_cparams(("parallel", "parallel", "arbitrary")),
            name=f"prompt_attention_{nc}",
        )(q, k, v, amask, tb))
    return jnp.concatenate(outs, axis=1)


def _s1_body(pt_ref, qi_ref, w_ref, *refs, n_pages, t):
    page_refs = refs[:n_pages + 1]
    o_ref = refs[n_pages + 1]
    qi = qi_ref[0]
    w = jnp.broadcast_to(w_ref[0] * IDX_W_SCALE, (IDX_HEADS * t, LANES))
    for p, kp_ref in enumerate(page_refs):
        d = jnp.dot(qi, kp_ref[0].astype(BF16), preferred_element_type=F32)
        wr = (w * jnp.maximum(d, 0.0)).reshape(IDX_HEADS, t, LANES)
        o_ref[0, :, p * LANES:(p + 1) * LANES] = jnp.sum(wr, axis=0)


def _sample_scores(qi_hq, wi_hq, cache_kidx, ki_new_pad, page_table):
    n, rows, _ = qi_hq.shape
    t = rows // IDX_HEADS
    n_pages = page_table.shape[1]
    lp = (n_pages + 1) * PAGE_SIZE

    def page_spec(p):
        return pl.BlockSpec((1, IDX_DIM, PAGE_SIZE), lambda b, pt: (pt[b, p], 0, 0))

    grid_spec = pltpu.PrefetchScalarGridSpec(
        num_scalar_prefetch=1,
        grid=(n,),
        in_specs=[pl.BlockSpec((1, rows, IDX_DIM), lambda b, pt: (b, 0, 0)),
                  pl.BlockSpec((1, rows, 1), lambda b, pt: (b, 0, 0))]
                 + [page_spec(p) for p in range(n_pages)]
                 + [pl.BlockSpec((1, IDX_DIM, PAGE_SIZE), lambda b, pt: (b, 0, 0))],
        out_specs=pl.BlockSpec((1, t, lp), lambda b, pt: (b, 0, 0)),
    )
    return pl.pallas_call(
        functools.partial(_s1_body, n_pages=n_pages, t=t),
        grid_spec=grid_spec,
        out_shape=jax.ShapeDtypeStruct((n, t, lp), F32),
        compiler_params=_cparams(("arbitrary",)),
        name="sample_scores",
    )(page_table, qi_hq, wi_hq, *([cache_kidx] * n_pages), ki_new_pad)


S2_TQ = 256


def _s2_body(sc_ref, mask_ref, key_ref, jcut_ref, *, t, lp):
    c0 = pl.program_id(0) * S2_TQ
    nblk = lp // SEL_RB
    lim = PAST_LEN + (c0 + lax.broadcasted_iota(I32, (SEL_RB, S2_TQ), 1)) % t + 1
    for r in range(nblk):
        rows = slice(r * SEL_RB, (r + 1) * SEL_RB)
        srow = r * SEL_RB + lax.broadcasted_iota(I32, (SEL_RB, S2_TQ), 0)
        key_ref[rows, :] = jnp.where(srow < lim, _sortable(sc_ref[rows, :]), INT_MIN)
    krow = jnp.full((1, S2_TQ), min(TOPK_MAX, (PAST_LEN + t) // 4), I32)
    v = _select_topk(key_ref, jcut_ref, krow, nblk, S2_TQ)
    for r in range(nblk):
        sel = _selected(key_ref, jcut_ref, v, r * SEL_RB, S2_TQ)
        mask_ref[r * SEL_RB:(r + 1) * SEL_RB, :] = jnp.where(sel, 1.0, 0.0).astype(BF16)


def _sample_select(scores_t, t):
    lp, r = scores_t.shape
    return pl.pallas_call(
        functools.partial(_s2_body, t=t, lp=lp),
        grid=(r // S2_TQ,),
        in_specs=[pl.BlockSpec((lp, S2_TQ), lambda i: (0, i))],
        out_specs=pl.BlockSpec((lp, S2_TQ), lambda i: (0, i)),
        out_shape=jax.ShapeDtypeStruct((lp, r), BF16),
        scratch_shapes=[pltpu.VMEM((lp, S2_TQ), I32), pltpu.VMEM((8, S2_TQ), I32)],
        compiler_params=_cparams(("parallel",)),
        name="sample_select",
    )(scores_t)


S3_GROUP = 4
S3_SLOTS = 4
S3_SEQS = LANES // 8
S3_PITCH = 24
S3_PAGE_ROWS = PAGE_SIZE * S3_PITCH


def _s3_body(pt_ref, qbd_ref, mask_ref, e_ref, knew_ref, vnew_ref, tsp_ref, tsn_ref, ck_hbm, cv_hbm, o_ref,
             buf_ref, sem, lg_ref, acc_ref, stage_ref, vpad_ref, *, n_seq, n_pages, t):
    ng = n_pages // S3_GROUP
    steps = 2 * ng
    b = pl.program_id(0)
    g = pl.program_id(1)
    st = b * steps + g
    past = n_pages * PAGE_SIZE
    rows_g = S3_GROUP * PAGE_SIZE
    buf3 = buf_ref.reshape(S3_SLOTS * S3_GROUP * PAGE_SIZE, S3_PITCH, HEAD_DIM)

    def page_copy(cache_hbm, pid, slot, j):
        dst = buf3.at[pl.ds((slot * S3_GROUP + j) * PAGE_SIZE, PAGE_SIZE), pl.ds(0, N_HEADS), :]
        return pltpu.make_async_copy(cache_hbm.at[pid], dst, sem.at[slot, j])

    def fetch(ahead):
        g2 = g + ahead
        b2 = b + g2 // steps
        g2 = g2 % steps
        slot2 = (st + ahead) % S3_SLOTS

        @pl.when((b2 < n_seq) & (g2 < ng))
        def _():
            for j in range(S3_GROUP):
                page_copy(ck_hbm, pt_ref[b2, g2 * S3_GROUP + j], slot2, j).start(priority=j % 2)

        @pl.when((b2 < n_seq) & (g2 >= ng))
        def _():
            for j in range(S3_GROUP):
                page_copy(cv_hbm, pt_ref[b2, (g2 - ng) * S3_GROUP + j], slot2, j).start(priority=j % 2)

    @pl.when(st == 0)
    def _():
        for ahead in range(S3_SLOTS - 1):
            fetch(ahead)

    fetch(S3_SLOTS - 1)
    slot = st % S3_SLOTS
    for j in range(S3_GROUP):
        page_copy(ck_hbm, 0, slot, j).wait()

    def stage():
        for j in range(S3_GROUP):
            base = (slot * S3_GROUP + j) * S3_PAGE_ROWS
            for h in range(N_HEADS):
                stage_ref[j * PAGE_SIZE:(j + 1) * PAGE_SIZE, h * HEAD_DIM:(h + 1) * HEAD_DIM] = (
                    buf_ref[pl.ds(base + h, PAGE_SIZE, stride=S3_PITCH), :].astype(BF16))

    def masked(lg, msk_rows):
        sel = jnp.dot(msk_rows, e_ref[0], preferred_element_type=F32)
        return jnp.where(sel > 0.5, lg, NEG)

    def logits(k16):
        return lax.dot_general(k16, qbd_ref[0], (((1,), (1,)), ((), ())), preferred_element_type=F32) * ATTN_SCALE

    @pl.when(g < ng)
    def _():
        stage()
        row0 = pl.multiple_of(g * rows_g, rows_g)
        lg_ref[pl.ds(row0, rows_g), :] = masked(logits(stage_ref[...]), mask_ref[pl.ds(row0, rows_g), :])

    @pl.when(g == ng - 1)
    def _():
        lg_ref[past - PAGE_SIZE:past, :] += tsp_ref[...]
        lgn = logits(knew_ref[0].astype(BF16)) + tsn_ref[...]
        lgn = jnp.concatenate([lgn, jnp.zeros((t, LANES), F32)], axis=0)
        lg_ref[past:past + t, :] = masked(lgn, mask_ref[past:past + 2 * t, :])[:t]
        lg_ref[past + t:, :] = jnp.full((PAGE_SIZE - t, LANES), NEG, F32)

    @pl.when(g == ng)
    def _():
        lg = lg_ref[...]
        p = jnp.exp(lg - jnp.max(lg, axis=0, keepdims=True))
        lg_ref[...] = p * (1.0 / jnp.sum(p, axis=0, keepdims=True))
        acc_ref[...] = jnp.zeros_like(acc_ref)

    def probs_t(row0, npages):
        return jnp.concatenate([jnp.transpose(lg_ref[pl.ds(row0 + j * PAGE_SIZE, PAGE_SIZE), :])
                                for j in range(npages)], axis=1).astype(BF16)

    @pl.when(g >= ng)
    def _():
        stage()
        row0 = pl.multiple_of((g - ng) * rows_g, rows_g)
        acc_ref[...] += jnp.dot(probs_t(row0, S3_GROUP), stage_ref[...], preferred_element_type=F32)

    @pl.when(g == steps - 1)
    def _():
        vpad_ref[...] = jnp.zeros_like(vpad_ref)
        vpad_ref[0:t, :] = vnew_ref[0]
        acc = acc_ref[...] + jnp.dot(probs_t(past, 1), vpad_ref[...].astype(BF16), preferred_element_type=F32)
        for h in range(N_HEADS):
            o_ref[0, :, h * HEAD_DIM:(h + 1) * HEAD_DIM] = acc[h * t:(h + 1) * t, h * HEAD_DIM:(h + 1) * HEAD_DIM]


def _sample_attention(qbd, mask_t, cache_k, cache_v, k_new, v_new, ts_page, ts_new, page_table):
    n, t, _ = k_new.shape
    n_pages = page_table.shape[1]
    ng = n_pages // S3_GROUP
    lp = mask_t.shape[0]
    assert t * S3_SEQS == LANES and t * N_HEADS == LANES
    sq = np.arange(LANES)
    expand = np.stack([(sq[:, None] // t == s) & (sq[:, None] % t == sq[None, :] % t) for s in range(S3_SEQS)])

    per_seq = lambda b, g, pt: (b, 0, 0)
    const2 = lambda b, g, pt: (0, 0)
    grid_spec = pltpu.PrefetchScalarGridSpec(
        num_scalar_prefetch=1,
        grid=(n, 2 * ng),
        in_specs=[pl.BlockSpec((1, LANES, ATTN_WIDTH), per_seq),
                  pl.BlockSpec((lp, LANES), lambda b, g, pt: (0, b // S3_SEQS)),
                  pl.BlockSpec((1, LANES, LANES), lambda b, g, pt: (b % S3_SEQS, 0, 0)),
                  pl.BlockSpec((1, t, ATTN_WIDTH), per_seq), pl.BlockSpec((1, t, ATTN_WIDTH), per_seq),
                  pl.BlockSpec((PAGE_SIZE, LANES), const2), pl.BlockSpec((t, LANES), const2),
                  pl.BlockSpec(memory_space=pl.ANY), pl.BlockSpec(memory_space=pl.ANY)],
        out_specs=pl.BlockSpec((1, t, ATTN_WIDTH), per_seq),
        scratch_shapes=[pltpu.VMEM((S3_SLOTS * S3_GROUP * S3_PAGE_ROWS, HEAD_DIM), F32),
                        pltpu.SemaphoreType.DMA((S3_SLOTS, S3_GROUP)),
                        pltpu.VMEM((lp, LANES), F32), pltpu.VMEM((LANES, ATTN_WIDTH), F32),
                        pltpu.VMEM((S3_GROUP * PAGE_SIZE, ATTN_WIDTH), BF16),
                        pltpu.VMEM((PAGE_SIZE, ATTN_WIDTH), F32)],
    )
    return pl.pallas_call(
        functools.partial(_s3_body, n_seq=n, n_pages=n_pages, t=t),
        grid_spec=grid_spec,
        out_shape=jax.ShapeDtypeStruct((n, t, ATTN_WIDTH), F32),
        compiler_params=_cparams(("arbitrary", "arbitrary")),
        name="sample_attention",
    )(page_table, qbd, mask_t, jnp.asarray(expand, BF16), k_new, v_new, ts_page, ts_new, cache_k, cache_v)


W_IN_SPLITS = (POOL_WIDTH, ATTN_WIDTH, ATTN_WIDTH, ATTN_WIDTH, IDX_HEADS * IDX_DIM, IDX_DIM, IDX_HEADS, D_MODEL, D_MODEL)
W_IN_OFFS = tuple(int(v) for v in np.cumsum((0,) + W_IN_SPLITS))


def _split_w_in(w):
    w16 = w.astype(BF16)
    return dict(all=w16, gp=w16[:, W_IN_OFFS[7]:W_IN_OFFS[8]], ga=w16[:, W_IN_OFFS[8]:W_IN_OFFS[9]])


def _project(x2d, ln1, wi):
    m = x2d.shape[0]
    tm = min(m, 1024)
    w, o = wi["all"], W_IN_OFFS
    xn = _rmsnorm(x2d, ln1, BF16)
    (u,) = _matmul(xn, w, [F32], tm=tm, tn=1024, name="proj_u", col0=o[0], n=POOL_WIDTH)
    (q,) = _matmul(xn, w, [BF16], tm=tm, tn=1024, name="proj_q", col0=o[1], n=ATTN_WIDTH)
    k, k16 = _matmul_heads(xn, w, tm=min(m, 512), tn=1024, name="proj_k", col0=o[2])
    v, v16 = _matmul_heads(xn, w, tm=min(m, 512), tn=1024, name="proj_v", col0=o[3])
    (qi,) = _matmul(xn, w, [BF16], tm=tm, tn=1024, name="proj_qi", col0=o[4], n=IDX_HEADS * IDX_DIM)
    (kiwi,) = _matmul(xn, w, [F32], tm=tm, tn=LANES, name="proj_kiwi", col0=o[5], n=LANES)
    ki = kiwi[:, :IDX_DIM]
    wgt = kiwi[:, IDX_DIM:IDX_DIM + IDX_HEADS]
    return xn, u, q, k, k16, v, v16, qi, ki, wgt


def _finish(x2d, xn, pool_o, attn_o, wi, w_bp, w_ba, w_out, ln2, w1, w2, ln_f):
    m = x2d.shape[0]
    tm = min(m, 1024)
    mg = _merge(xn, pool_o, attn_o, wi["gp"], wi["ga"], w_bp, w_ba)
    (x1,) = _matmul(mg, w_out, [F32], tm=tm, tn=1024, res=x2d, name="out_proj")
    hn = _rmsnorm(x1, ln2, BF16)
    (h,) = _matmul(hn, w1, [BF16], tm=tm, tn=512, act="relu2", name="mlp_in")
    x2 = _matmul_ktiled(h, w2, x1, tm=tm, tn=1024, tk=2048, name="mlp_out")
    return _rmsnorm(x2, ln_f, F32)


def kernel(x_prompt, x_sample, cache_k, cache_v, cache_kidx, state_pool, page_table, ln1, w_in, pool_w,
           pool_scale, rel_bias, w_branch_pool, w_branch_attn, w_out, ln2, w_mlp_in, w_mlp_out, ln_f):
    bn, s, _ = x_prompt.shape
    n, t, _ = x_sample.shape
    depth = w_in.shape[0]
    assert depth == 1, "one layer: the caches and pooling state of layer 0 are the only ones read"
    n_pages = page_table.shape[1]
    n_phys = cache_k.shape[1]

    wi = _split_w_in(w_in[0])
    pw = pool_w[0].astype(BF16)
    w_bp = w_branch_pool[0].astype(BF16)
    w_ba = w_branch_attn[0].astype(BF16)
    wo = w_out[0].astype(BF16)
    w1 = w_mlp_in[0]
    w2 = w_mlp_out[0].astype(BF16)
    tb_prompt, ts_page, ts_new = _bias_tables(rel_bias, t)

    xp = x_prompt.reshape(bn * s, D_MODEL)
    xn, u, q, k, k16, v, v16, qi, ki, wgt = _project(xp, ln1[0], wi)
    pool_o = _pool_prompt(u.reshape(bn, s, POOL_WIDTH), pw, pool_scale[0]).reshape(bn * s, POOL_WIDTH)
    ki_c = ki.astype(BF16).reshape(bn, s // P1_CK, P1_CK, IDX_DIM)
    zeros = jnp.zeros_like(ki_c)
    kbd = jnp.concatenate([jnp.concatenate([ki_c, zeros], axis=-1), jnp.concatenate([zeros, ki_c], axis=-1)],
                          axis=2).reshape(bn, 2 * s, 2 * IDX_DIM)
    amask = _prompt_select(qi.reshape(bn, s, IDX_HEADS * IDX_DIM), kbd,
                           jnp.swapaxes(wgt.reshape(bn, s, IDX_HEADS), 1, 2))
    attn_o = _prompt_attention(q.reshape(bn, s, ATTN_WIDTH), k16.reshape(bn, s, ATTN_WIDTH),
                               v16.reshape(bn, s, ATTN_WIDTH), amask, tb_prompt).reshape(bn * s, ATTN_WIDTH)
    y_prompt = _finish(xp, xn, pool_o, attn_o, wi, w_bp, w_ba, wo, ln2[0], w1, w2, ln_f).reshape(bn, s, D_MODEL)
    k_prompt = k.reshape(1, bn, s, N_HEADS, HEAD_DIM)
    v_prompt = v.reshape(1, bn, s, N_HEADS, HEAD_DIM)
    kidx_prompt = ki.reshape(1, bn, s, IDX_DIM)
    pool_prompt = u.reshape(bn, s, POOL_WIDTH)[:, s - POOL_PREFIX:, :][None]

    xs = x_sample.reshape(n * t, D_MODEL)
    xn, u, q, k, k16, v, v16, qi, ki, wgt = _project(xs, ln1[0], wi)
    u3 = u.reshape(n, t, POOL_WIDTH)
    pool_o = _pool_sample(u3, state_pool[0], pw, pool_scale[0])
    qi_hq = qi.reshape(n, t, IDX_HEADS, IDX_DIM).transpose(0, 2, 1, 3).reshape(n, IDX_HEADS * t, IDX_DIM)
    wi_hq = wgt.reshape(n, t, IDX_HEADS).transpose(0, 2, 1).reshape(n, IDX_HEADS * t, 1)
    ki_new_pad = jnp.pad(jnp.swapaxes(ki.reshape(n, t, IDX_DIM), 1, 2), ((0, 0), (0, 0), (0, PAGE_SIZE - t)))
    scores = _sample_scores(qi_hq, wi_hq, jnp.swapaxes(cache_kidx[0], 1, 2), ki_new_pad, page_table)
    lp = scores.shape[-1]
    mask_t = _sample_select(scores.reshape(n * t, lp).T, t)
    head_of_col = np.arange(ATTN_WIDTH) // HEAD_DIM
    diag = jnp.asarray(head_of_col[None, :] == np.arange(N_HEADS)[:, None], BF16)
    qbd = (q.reshape(n, 1, t, ATTN_WIDTH) * diag[None, :, None, :]).reshape(n, N_HEADS * t, ATTN_WIDTH)
    attn_o = _sample_attention(qbd, mask_t, cache_k[0], cache_v[0], k.reshape(n, t, ATTN_WIDTH),
                               v.reshape(n, t, ATTN_WIDTH), ts_page, ts_new, page_table)
    attn_o = attn_o.reshape(n * t, ATTN_WIDTH).astype(BF16)
    y_sample = _finish(xs, xn, pool_o, attn_o, wi, w_bp, w_ba, wo, ln2[0], w1, w2, ln_f).reshape(n, t, D_MODEL)
    k_sample = k.reshape(1, n, t, N_HEADS, HEAD_DIM)
    v_sample = v.reshape(1, n, t, N_HEADS, HEAD_DIM)
    kidx_sample = ki.reshape(1, n, t, IDX_DIM)
    pool_sample = jnp.concatenate([state_pool[0][:, t:, :], u3], axis=1)[None]

    return (y_prompt, y_sample, k_prompt, v_prompt, kidx_prompt, pool_prompt,
            k_sample, v_sample, kidx_sample, pool_sample)
```
